```python
import jax, jax.numpy as jnp
from jax import lax
import numpy as np

D_MODEL = 2048
BATCH = 2
SEQ = 8192
DEPTH = 1

ATT_HEADS = 16
ATT_KV_HEADS = 2
ATT_HEAD_DIM = 64
ATT_WIDTH = ATT_HEADS * ATT_HEAD_DIM
ATT_KV_WIDTH = ATT_KV_HEADS * ATT_HEAD_DIM
WINDOW = 128
ATT_BLOCK = 128
ROPE_DIM = ATT_HEAD_DIM // 4
ROPE_THETA = 500000.0
M_HEADS = 4
M_QK_DIM = 128
M_V_DIM = 256
M_QK_WIDTH = M_HEADS * M_QK_DIM
M_WIDTH = M_HEADS * M_V_DIM
M_CHUNK = 64
CONV_WIDTH = 4
D_FF = 5632
NORM_EPS = 1e-6
NEG_INF = -1e30
N_MOD = 9
PROJ_SIZES = (ATT_WIDTH, ATT_KV_WIDTH, ATT_KV_WIDTH,
              2 * M_QK_WIDTH, M_WIDTH,
              M_HEADS, M_HEADS, M_WIDTH,
              2 * D_MODEL)
PROJ_OUT = sum(PROJ_SIZES)

kernel_name = "hybrid_swa_mlstm_macaron_adaln"


def _split_cols(t, sizes):
    idx = np.cumsum(sizes)[:-1].tolist()
    return jnp.split(t, idx, axis=-1)


def _rmsnorm(x, g):
    x32 = x.astype(jnp.float32)
    y = x32 * lax.rsqrt(jnp.mean(x32 * x32, axis=-1, keepdims=True) + NORM_EPS)
    return y.astype(x.dtype) * g


def _modulate(x, shift, scale):
    return x * (1 + scale) + shift


def _swiglu(x, w_in, w_out):
    a, b = jnp.split(x @ w_in, 2, axis=-1)
    return (jax.nn.silu(a) * b) @ w_out


def _partial_rope(x, positions):
    half = ROPE_DIM // 2
    inv_freq = jnp.power(ROPE_THETA, -jnp.arange(half, dtype=jnp.float32) * 2.0 / ROPE_DIM)
    ang = positions.astype(jnp.float32)[..., None] * inv_freq
    cos, sin = jnp.cos(ang)[:, :, None, :], jnp.sin(ang)[:, :, None, :]
    xr = x[..., :ROPE_DIM].astype(jnp.float32)
    x1, x2 = xr[..., :half], xr[..., half:]
    rot = jnp.concatenate([x1 * cos - x2 * sin, x2 * cos + x1 * sin], axis=-1)
    return jnp.concatenate([rot.astype(x.dtype), x[..., ROPE_DIM:]], axis=-1)


def _sliding_window_attention(q, k, v, sinks):
    B, T, Hq, hd = q.shape
    nb = T // ATT_BLOCK
    G = Hq // ATT_KV_HEADS
    qb = q.reshape(B, nb, ATT_BLOCK, ATT_KV_HEADS, G, hd)

    def band(t):
        tp = jnp.pad(t, ((0, 0), (ATT_BLOCK, 0), (0, 0), (0, 0)))
        prev = tp[:, :T].reshape(B, nb, ATT_BLOCK, ATT_KV_HEADS, hd)
        cur = t.reshape(B, nb, ATT_BLOCK, ATT_KV_HEADS, hd)
        return jnp.concatenate([prev, cur], axis=2)

    kb, vb = band(k), band(v)
    s = jnp.einsum('bnqhgd,bnkhd->bnhgqk', qb, kb,
                   preferred_element_type=jnp.float32) * (hd ** -0.5)
    qi = jnp.arange(ATT_BLOCK)[:, None] + ATT_BLOCK
    kj = jnp.arange(2 * ATT_BLOCK)[None, :]
    band_ok = (kj <= qi) & (kj > qi - WINDOW)
    blk_ok = (jnp.arange(nb)[:, None, None] > 0) | (kj[None] >= ATT_BLOCK)
    mask = band_ok[None] & blk_ok
    s = jnp.where(mask[None, :, None, None], s, NEG_INF)
    sink = sinks.astype(jnp.float32).reshape(1, 1, ATT_KV_HEADS, G, 1, 1)
    m = jnp.maximum(jnp.max(s, axis=-1, keepdims=True), sink)
    p = jnp.exp(s - m)
    denom = jnp.sum(p, axis=-1, keepdims=True) + jnp.exp(sink - m)
    o = jnp.einsum('bnhgqk,bnkhd->bnqhgd', (p / denom).astype(v.dtype), vb)
    return o.reshape(B, T, Hq * hd)


def _causal_conv(x, w, b):
    C = x.shape[-1]
    y = lax.conv_general_dilated(x, w[:, None, :].astype(x.dtype), window_strides=(1,),
                                 padding=((CONV_WIDTH - 1, 0),),
                                 dimension_numbers=('NWC', 'WIO', 'NWC'),
                                 feature_group_count=C)
    return y + b


def _mlstm_chunkwise(q, k, v, i_pre, f_pre):
    B, T, H, dk = q.shape
    dv = v.shape[-1]
    L = M_CHUNK
    nc = T // L
    f32 = jnp.float32

    def chunks(t):
        return jnp.moveaxis(t.astype(f32).reshape(B, nc, L, H, -1), 3, 1)

    qc = chunks(q)
    kc = chunks(k) * (dk ** -0.5)
    vc = chunks(v)
    ig = jnp.moveaxis(i_pre.astype(f32).reshape(B, nc, L, H), 3, 1)
    logf = jax.nn.log_sigmoid(jnp.moveaxis(f_pre.astype(f32).reshape(B, nc, L, H), 3, 1))
    bcum = jnp.cumsum(logf, axis=-1)
    b_last = bcum[..., -1]
    g = b_last[..., None] - bcum + ig
    g_max = jnp.max(g, axis=-1)
    w = jnp.exp(g - g_max[..., None])
    C_loc = jnp.einsum('bhcl,bhcld,bhcle->bhcde', w, kc, vc)
    n_loc = jnp.einsum('bhcl,bhcld->bhcd', w, kc)

    def step(carry, inp):
        C, n, m = carry
        Cl, nl, gm, bl = inp
        m_new = jnp.maximum(bl + m, gm)
        a = jnp.exp(bl + m - m_new)
        s = jnp.exp(gm - m_new)
        C_new = a[..., None, None] * C + s[..., None, None] * Cl
        n_new = a[..., None] * n + s[..., None] * nl
        return (C_new, n_new, m_new), (C, n, m)

    init = (jnp.zeros((B, H, dk, dv), f32), jnp.zeros((B, H, dk), f32),
            jnp.full((B, H), NEG_INF, f32))
    xs = (jnp.moveaxis(C_loc, 2, 0), jnp.moveaxis(n_loc, 2, 0),
          jnp.moveaxis(g_max, 2, 0), jnp.moveaxis(b_last, 2, 0))
    _, (C_in, n_in, m_in) = lax.scan(step, init, xs)
    C_in = jnp.moveaxis(C_in, 0, 2)
    n_in = jnp.moveaxis(n_in, 0, 2)
    m_in = jnp.moveaxis(m_in, 0, 2)

    a_log = bcum + m_in[..., None]
    D = bcum[..., :, None] - bcum[..., None, :] + ig[..., None, :]
    causal = jnp.tril(jnp.ones((L, L), dtype=bool))
    D = jnp.where(causal, D, NEG_INF)
    m_t = jnp.maximum(a_log, jnp.max(D, axis=-1))
    inter = jnp.exp(a_log - m_t)
    qk = jnp.einsum('bhctd,bhcsd->bhcts', qc, kc) * jnp.exp(D - m_t[..., None])
    num = (inter[..., None] * jnp.einsum('bhctd,bhcde->bhcte', qc, C_in)
           + jnp.einsum('bhcts,bhcse->bhcte', qk, vc))
    den = inter * jnp.einsum('bhctd,bhcd->bhct', qc, n_in) + jnp.sum(qk, axis=-1)
    h = num / jnp.maximum(jnp.abs(den), jnp.exp(-m_t))[..., None]
    return jnp.moveaxis(h, 1, 3).reshape(B, T, H * dv).astype(v.dtype)


def _head_rmsnorm(h, g):
    B, T, _ = h.shape
    return _rmsnorm(h.reshape(B, T, M_HEADS, M_V_DIM), 1).reshape(B, T, M_WIDTH) * g


def _hybrid_mixer(u, positions, w_in, conv_w, conv_b, b_igate, b_fgate, sinks,
                  mlstm_norm_g, w_att_up, w_mlstm_up, w_out):
    B, T, _ = u.shape
    proj = u @ w_in
    aq, ak, av, mqk, mv, mi, mf, mo, gates = _split_cols(proj, PROJ_SIZES)
    q = _partial_rope(aq.reshape(B, T, ATT_HEADS, ATT_HEAD_DIM), positions)
    k = _partial_rope(ak.reshape(B, T, ATT_KV_HEADS, ATT_HEAD_DIM), positions)
    v = av.reshape(B, T, ATT_KV_HEADS, ATT_HEAD_DIM)
    y_att = _sliding_window_attention(q, k, v, sinks) @ w_att_up
    mqk = jax.nn.silu(_causal_conv(mqk, conv_w, conv_b))
    mq, mk = jnp.split(mqk, 2, axis=-1)
    hm = _mlstm_chunkwise(mq.reshape(B, T, M_HEADS, M_QK_DIM),
                          mk.reshape(B, T, M_HEADS, M_QK_DIM),
                          mv.reshape(B, T, M_HEADS, M_V_DIM),
                          mi + b_igate, mf + b_fgate)
    hm = _head_rmsnorm(hm, mlstm_norm_g) * jax.nn.sigmoid(mo)
    y_m = hm @ w_mlstm_up
    g_att, g_m = jnp.split(jax.nn.sigmoid(gates), 2, axis=-1)
    return (g_att * y_att + g_m * y_m) @ w_out


def setup_inputs(seed: int = 0) -> dict:
    key = jax.random.key(seed)
    ks = jax.random.split(key, 24)
    f32 = jnp.float32
    D = D_MODEL

    def nrm(k, shape, scale):
        return jax.random.normal(k, shape, f32) * scale

    x = nrm(ks[0], (BATCH, SEQ, D), 1.0)
    c = nrm(ks[1], (BATCH, D), 1.0)
    offset = jax.random.randint(ks[2], (BATCH, 1), 0, 4096, dtype=jnp.int32)
    positions = (offset + jnp.arange(SEQ, dtype=jnp.int32)[None, :]).astype(jnp.int32)
    return {
        'x': x,
        'c': c,
        'positions': positions,
        'w_ada': nrm(ks[3], (DEPTH, D, N_MOD * D), 0.5 * D ** -0.5),
        'b_ada': nrm(ks[4], (DEPTH, N_MOD * D), 0.02),
        'norm1_g': 1.0 + nrm(ks[5], (DEPTH, D), 0.02),
        'ffn1_w_in': nrm(ks[6], (DEPTH, D, 2 * D_FF), D ** -0.5),
        'ffn1_w_out': nrm(ks[7], (DEPTH, D_FF, D), D_FF ** -0.5),
        'norm2_g': 1.0 + nrm(ks[8], (DEPTH, D), 0.02),
        'w_in': nrm(ks[9], (DEPTH, D, PROJ_OUT), D ** -0.5),
        'conv_w': nrm(ks[10], (DEPTH, CONV_WIDTH, 2 * M_QK_WIDTH), CONV_WIDTH ** -0.5),
        'conv_b': nrm(ks[11], (DEPTH, 2 * M_QK_WIDTH), 0.02),
        'b_igate': nrm(ks[12], (DEPTH, M_HEADS), 0.1),
        'b_fgate': jnp.linspace(3.0, 6.0, M_HEADS, dtype=f32)[None, :] + nrm(ks[13], (DEPTH, M_HEADS), 0.1),
        'sinks': nrm(ks[14], (DEPTH, ATT_HEADS), 1.0),
        'mlstm_norm_g': 1.0 + nrm(ks[15], (DEPTH, M_WIDTH), 0.02),
        'w_att_up': nrm(ks[16], (DEPTH, ATT_WIDTH, D), ATT_WIDTH ** -0.5),
        'w_mlstm_up': nrm(ks[17], (DEPTH, M_WIDTH, D), M_WIDTH ** -0.5),
        'w_out': nrm(ks[18], (DEPTH, D, D), D ** -0.5),
        'norm3_g': 1.0 + nrm(ks[19], (DEPTH, D), 0.02),
        'ffn2_w_in': nrm(ks[20], (DEPTH, D, 2 * D_FF), D ** -0.5),
        'ffn2_w_out': nrm(ks[21], (DEPTH, D_FF, D), D_FF ** -0.5),
        'final_g': 1.0 + nrm(ks[22], (D,), 0.02),
    }


def reference(x, c, positions, w_ada, b_ada, norm1_g, ffn1_w_in, ffn1_w_out, norm2_g,
              w_in, conv_w, conv_b, b_igate, b_fgate, sinks, mlstm_norm_g, w_att_up,
              w_mlstm_up, w_out, norm3_g, ffn2_w_in, ffn2_w_out, final_g):
    h = x
    cond = jax.nn.silu(c)
    for l in range(DEPTH):
        mod = cond @ w_ada[l] + b_ada[l]
        sh1, sc1, gt1, sh2, sc2, gt2, sh3, sc3, gt3 = [m[:, None, :] for m in jnp.split(mod, N_MOD, axis=-1)]
        u = _modulate(_rmsnorm(h, norm1_g[l]), sh1, sc1)
        h = h + 0.5 * gt1 * _swiglu(u, ffn1_w_in[l], ffn1_w_out[l])
        u = _modulate(_rmsnorm(h, norm2_g[l]), sh2, sc2)
        h = h + gt2 * _hybrid_mixer(u, positions, w_in[l], conv_w[l], conv_b[l], b_igate[l],
                                    b_fgate[l], sinks[l], mlstm_norm_g[l], w_att_up[l],
                                    w_mlstm_up[l], w_out[l])
        u = _modulate(_rmsnorm(h, norm3_g[l]), sh3, sc3)
        h = h + 0.5 * gt3 * _swiglu(u, ffn2_w_in[l], ffn2_w_out[l])
    return _rmsnorm(h, final_g)
```

```python
import functools

import jax
import jax.numpy as jnp
from jax import lax
from jax.experimental import pallas as pl
from jax.experimental.pallas import tpu as pltpu

F32 = jnp.float32
BF16 = jnp.bfloat16

ATT_HEADS = 16
ATT_KV_HEADS = 2
ATT_HEAD_DIM = 64
WINDOW = 128
ROPE_DIM = ATT_HEAD_DIM // 4
ROPE_THETA = 500000.0
M_HEADS = 4
M_QK_DIM = 128
M_V_DIM = 256
M_CHUNK = 64
CONV_WIDTH = 4
NORM_EPS = 1e-6
NEG_INF = -1e30
N_MOD = 9

LANES = 128
SUBLANES = 8
VMEM_LIMIT = 56 * 1024 * 1024


def _dot(a, b):
    return jnp.dot(a, b, preferred_element_type=F32)


def _dot_nt(a, b):
    return lax.dot_general(a, b, (((1,), (1,)), ((), ())), preferred_element_type=F32)


def _dot_tn(a, b):
    return lax.dot_general(a, b, (((0,), (0,)), ((), ())), preferred_element_type=F32)


def _split3(x):
    hi = x.astype(BF16)
    r1 = x - hi.astype(F32)
    mid = r1.astype(BF16)
    lo = (r1 - mid.astype(F32)).astype(BF16)
    return hi, mid, lo


def _rms(x):
    return x * lax.rsqrt(jnp.mean(x * x, axis=-1, keepdims=True) + NORM_EPS)


def _params(sem):
    return pltpu.CompilerParams(dimension_semantics=sem, vmem_limit_bytes=VMEM_LIMIT)


def _ada_kernel(c_ref, w_ref, b_ref, o_ref):
    c = c_ref[...]
    hi, mid, lo = _split3(c * jax.nn.sigmoid(c))
    w = w_ref[...].astype(BF16)
    o_ref[...] = _dot(hi, w) + _dot(mid, w) + _dot(lo, w) + b_ref[...]


def _ada(c, w_ada, b_ada, tn=1024):
    bsz, d = c.shape
    n = w_ada.shape[1]
    cp = jnp.zeros((SUBLANES, d), F32).at[:bsz].set(c)
    out = pl.pallas_call(
        _ada_kernel,
        grid=(n // tn,),
        in_specs=[pl.BlockSpec((SUBLANES, d), lambda j: (0, 0)),
                  pl.BlockSpec((d, tn), lambda j: (0, j)),
                  pl.BlockSpec((1, tn), lambda j: (0, j))],
        out_specs=pl.BlockSpec((SUBLANES, tn), lambda j: (0, j)),
        out_shape=jax.ShapeDtypeStruct((SUBLANES, n), F32),
        compiler_params=_params(("arbitrary",)),
        name="ada",
    )(cp, w_ada, b_ada.reshape(1, n))
    return out[:bsz]


def _ffn_kernel(x_ref, g_ref, sh_ref, sc_ref, gt_ref, wa_ref, wb_ref, wo_ref, *rest, final_norm):
    if final_norm:
        fg_ref, o_ref, u_ref = rest
    else:
        o_ref, u_ref = rest
    j = pl.program_id(1)

    @pl.when(j == 0)
    def _():
        y = _rms(x_ref[...]) * g_ref[...]
        u_ref[...] = (y * (1.0 + sc_ref[0]) + sh_ref[0]).astype(BF16)
        o_ref[...] = jnp.zeros_like(o_ref)

    u = u_ref[...]
    a = _dot(u, wa_ref[...])
    b = _dot(u, wb_ref[...])
    o_ref[...] += _dot((a * jax.nn.sigmoid(a) * b).astype(BF16), wo_ref[...])

    @pl.when(j == pl.num_programs(1) - 1)
    def _():
        h = x_ref[...] + (0.5 * gt_ref[0]) * o_ref[...]
        if final_norm:
            h = _rms(h) * fg_ref[...]
        o_ref[...] = h


def _ffn(h, seq, norm_g, mod3, mod_base, w_in, w_out, final_g=None, tm=512, tf=512):
    n, d = h.shape
    dff = w_out.shape[0]
    nj = dff // tf
    tiles_per_seq = seq // tm

    def mod_spec(k):
        return pl.BlockSpec((1, 1, d), lambda i, j: ((i // tiles_per_seq) * N_MOD + mod_base + k, 0, 0))

    in_specs = [pl.BlockSpec((tm, d), lambda i, j: (i, 0)),
                pl.BlockSpec((1, d), lambda i, j: (0, 0)),
                mod_spec(0), mod_spec(1), mod_spec(2),
                pl.BlockSpec((d, tf), lambda i, j: (0, j)),
                pl.BlockSpec((d, tf), lambda i, j: (0, j + nj)),
                pl.BlockSpec((tf, d), lambda i, j: (j, 0))]
    args = [h, norm_g.reshape(1, d), mod3, mod3, mod3, w_in, w_in, w_out]
    if final_g is not None:
        in_specs.append(pl.BlockSpec((1, d), lambda i, j: (0, 0)))
        args.append(final_g.reshape(1, d))
    return pl.pallas_call(
        functools.partial(_ffn_kernel, final_norm=final_g is not None),
        grid=(n // tm, nj),
        in_specs=in_specs,
        out_specs=pl.BlockSpec((tm, d), lambda i, j: (i, 0)),
        out_shape=jax.ShapeDtypeStruct((n, d), F32),
        scratch_shapes=[pltpu.VMEM((tm, d), BF16)],
        compiler_params=_params(("arbitrary", "arbitrary")),
        name="ffn_final" if final_g is not None else "ffn",
    )(*args)


def _proj_kernel(x_ref, g_ref, sh_ref, sc_ref, w_ref, wif_ref, bif_ref, p_ref, if_ref, u_ref):
    @pl.when(pl.program_id(1) == 0)
    def _():
        y = _rms(x_ref[...]) * g_ref[...]
        u = (y * (1.0 + sc_ref[0]) + sh_ref[0]).astype(BF16)
        u_ref[...] = u
        if_ref[...] = _dot(u, wif_ref[...]) + bif_ref[...]

    p_ref[...] = _dot(u_ref[...], w_ref[...])


def _proj(h, seq, norm_g, mod3, mod_base, w, w_if, b_if, tm=512, tn=768):
    n, d = h.shape
    ncol = w.shape[1]
    tiles_per_seq = seq // tm

    def mod_spec(k):
        return pl.BlockSpec((1, 1, d), lambda i, j: ((i // tiles_per_seq) * N_MOD + mod_base + k, 0, 0))

    return pl.pallas_call(
        _proj_kernel,
        grid=(n // tm, ncol // tn),
        in_specs=[pl.BlockSpec((tm, d), lambda i, j: (i, 0)),
                  pl.BlockSpec((1, d), lambda i, j: (0, 0)),
                  mod_spec(0), mod_spec(1),
                  pl.BlockSpec((d, tn), lambda i, j: (0, j)),
                  pl.BlockSpec((d, LANES), lambda i, j: (0, 0)),
                  pl.BlockSpec((1, LANES), lambda i, j: (0, 0))],
        out_specs=[pl.BlockSpec((tm, tn), lambda i, j: (i, j)),
                   pl.BlockSpec((tm, LANES), lambda i, j: (i, 0))],
        out_shape=[jax.ShapeDtypeStruct((n, ncol), F32),
                   jax.ShapeDtypeStruct((n, LANES), F32)],
        scratch_shapes=[pltpu.VMEM((tm, d), BF16)],
        compiler_params=_params(("arbitrary", "arbitrary")),
        name="proj",
    )(h, norm_g.reshape(1, d), mod3, mod3, w, w_if, b_if)


def _attn_kernel(q_ref, kc_ref, vc_ref, kp_ref, vp_ref, posc_ref, posp_ref, invf_ref, sink_ref, o_ref):
    blk = WINDOW
    lane = lax.broadcasted_iota(jnp.int32, (1, LANES), 1)
    hl = lane % ATT_HEAD_DIM
    half = ROPE_DIM // 2
    lo_half = lane < ATT_HEAD_DIM

    def rope_tables(pos_ref):
        pos = pos_ref[0].astype(F32)
        ang = invf_ref[...] * pos
        reps = LANES // half
        ct = jnp.tile(jnp.cos(ang), (reps, 1)).T
        st = jnp.tile(jnp.sin(ang), (reps, 1)).T
        c = jnp.where(hl < ROPE_DIM, ct, 1.0)
        s = jnp.where(hl < half, -st, jnp.where(hl < ROPE_DIM, st, 0.0))
        return c, s

    def rope(x, c, s):
        partner = jnp.where(hl < half, pltpu.roll(x, LANES - half, 1), pltpu.roll(x, half, 1))
        return x * c + partner * s

    c_cur, s_cur = rope_tables(posc_ref)
    c_prv, s_prv = rope_tables(posp_ref)
    k = jnp.concatenate([rope(kp_ref[...], c_prv, s_prv), rope(kc_ref[...], c_cur, s_cur)], axis=0)
    v = jnp.concatenate([vp_ref[...], vc_ref[...]], axis=0)
    k_sw = pltpu.roll(k, ATT_HEAD_DIM, 1)
    v_sw = pltpu.roll(v, ATT_HEAD_DIM, 1)

    qi = lax.broadcasted_iota(jnp.int32, (blk, 2 * blk), 0) + blk
    kj = lax.broadcasted_iota(jnp.int32, (blk, 2 * blk), 1)
    k_min = jnp.where(pl.program_id(1) == 0, blk, 0)
    mask = (kj <= qi) & (kj > qi - WINDOW) & (kj >= k_min)

    scale = ATT_HEAD_DIM ** -0.5
    pairs_per_group = ATT_HEADS // ATT_KV_HEADS // 2
    for g in range(ATT_KV_HEADS):
        k_g, k_o = (k, k_sw) if g == 0 else (k_sw, k)
        v_g, v_o = (v, v_sw) if g == 0 else (v_sw, v)
        k_rhs = jnp.concatenate([jnp.where(lo_half, k_g, 0.0), jnp.where(lo_half, 0.0, k_o)], axis=0).astype(BF16)
        v_rhs = jnp.concatenate([jnp.where(lo_half, v_g, 0.0), jnp.where(lo_half, 0.0, v_o)], axis=0).astype(BF16)
        q_rows = []
        for pp in range(pairs_per_group):
            p = g * pairs_per_group + pp
            q_p = rope(q_ref[:, p * LANES:(p + 1) * LANES], c_cur, s_cur) * scale
            q_rows.append(q_p.astype(BF16))
        s_all = _dot_nt(jnp.concatenate(q_rows, axis=0), k_rhs)
        p_rows = []
        for pp in range(pairs_per_group):
            halves = []
            for e in range(2):
                head = 2 * (g * pairs_per_group + pp) + e
                sink = sink_ref[head]
                s = s_all[pp * blk:(pp + 1) * blk, e * 2 * blk:(e + 1) * 2 * blk]
                s = jnp.where(mask, s, NEG_INF)
                m = jnp.maximum(jnp.max(s, axis=-1, keepdims=True), sink)
                pr = jnp.exp(s - m)
                denom = jnp.sum(pr, axis=-1, keepdims=True) + jnp.exp(sink - m)
                halves.append((pr * (1.0 / denom)).astype(BF16))
            p_rows.append(jnp.concatenate(halves, axis=1))
        o_all = _dot(jnp.concatenate(p_rows, axis=0), v_rhs)
        for pp in range(pairs_per_group):
            p = g * pairs_per_group + pp
            o_ref[:, p * LANES:(p + 1) * LANES] = o_all[pp * blk:(pp + 1) * blk].astype(BF16)


def _attn(proj, pos3, inv_freq, sinks, bsz, seq, q_col, k_col, v_col):
    blk = WINDOW
    aw = ATT_HEADS * ATT_HEAD_DIM
    nb = seq // blk
    n = bsz * seq
    half = ROPE_DIM // 2

    def cur(b, i):
        return b * nb + i

    def prv(b, i):
        return b * nb + jnp.maximum(i - 1, 0)

    return pl.pallas_call(
        _attn_kernel,
        grid=(bsz, nb),
        in_specs=[pl.BlockSpec((blk, aw), lambda b, i: (cur(b, i), q_col // aw)),
                  pl.BlockSpec((blk, LANES), lambda b, i: (cur(b, i), k_col // LANES)),
                  pl.BlockSpec((blk, LANES), lambda b, i: (cur(b, i), v_col // LANES)),
                  pl.BlockSpec((blk, LANES), lambda b, i: (prv(b, i), k_col // LANES)),
                  pl.BlockSpec((blk, LANES), lambda b, i: (prv(b, i), v_col // LANES)),
                  pl.BlockSpec((1, 1, blk), lambda b, i: (cur(b, i), 0, 0)),
                  pl.BlockSpec((1, 1, blk), lambda b, i: (prv(b, i), 0, 0)),
                  pl.BlockSpec((half, blk), lambda b, i: (0, 0)),
                  pl.BlockSpec(memory_space=pltpu.SMEM)],
        out_specs=pl.BlockSpec((blk, aw), lambda b, i: (cur(b, i), 0)),
        out_shape=jax.ShapeDtypeStruct((n, aw), BF16),
        compiler_params=_params(("arbitrary", "arbitrary")),
        name="attn",
    )(proj, proj, proj, proj, proj, pos3, pos3, inv_freq, sinks)


def _log_sigmoid(x):
    return jnp.minimum(x, 0.0) - jnp.log1p(jnp.exp(-jnp.abs(x)))


def _mlstm_kernel(xqk_ref, xprev_ref, v_ref, mo_ref, if_ref, cw_ref, cb_ref, ng_ref, o_ref,
                  xbuf_ref, c_ref, n_ref, m_ref, *, tc):
    heads, dk, dv, ch = M_HEADS, M_QK_DIM, M_V_DIM, M_CHUNK
    qkw = heads * dk
    t_idx = pl.program_id(1)

    @pl.when(t_idx == 0)
    def _():
        c_ref[...] = jnp.zeros_like(c_ref)
        n_ref[...] = jnp.zeros_like(n_ref)
        m_ref[...] = jnp.full_like(m_ref, NEG_INF)

    pad = SUBLANES
    xbuf_ref[0:pad, :] = jnp.where(t_idx > 0, xprev_ref[...], 0.0)
    xbuf_ref[pad:pad + tc, :] = xqk_ref[...]
    y = cb_ref[...]
    for j in range(CONV_WIDTH):
        off = pad - (CONV_WIDTH - 1) + j
        y = y + cw_ref[j:j + 1, :] * xbuf_ref[off:off + tc, :]
    y = y * jax.nn.sigmoid(y)
    q_all = y[:, :qkw].astype(BF16)
    k_all = y[:, qkw:] * (dk ** -0.5)
    k_all_b = k_all.astype(BF16)

    gates = if_ref[...]
    logf = _log_sigmoid(gates)
    row = lax.broadcasted_iota(jnp.int32, (tc, tc), 0)
    col = lax.broadcasted_iota(jnp.int32, (tc, tc), 1)
    tri = jnp.where((row // ch == col // ch) & (col <= row), 1.0, 0.0).astype(BF16)
    hi, mid, lo = _split3(logf)
    bcum = _dot(tri, hi) + _dot(tri, mid) + _dot(tri, lo)
    lane = lax.broadcasted_iota(jnp.int32, (1, LANES), 1)
    packed_t = jnp.where(lane < heads, gates, bcum).T

    causal = (lax.broadcasted_iota(jnp.int32, (ch, ch), 1) <= lax.broadcasted_iota(jnp.int32, (ch, ch), 0))
    ng = ng_ref[...]
    for h in range(heads):
        c_state = c_ref[h]
        n_state = n_ref[h]
        m_state = m_ref[h][0:1, 0:1]
        for c in range(tc // ch):
            sl = slice(c * ch, (c + 1) * ch)
            ig = gates[sl, h:h + 1]
            bc = bcum[sl, heads + h:heads + h + 1]
            b_last = bc[ch - 1:ch, :]
            r_col = ig - bc
            r_row = packed_t[h:h + 1, sl] - packed_t[heads + h:heads + h + 1, sl]
            q = q_all[sl, h * dk:(h + 1) * dk]
            k = k_all[sl, h * dk:(h + 1) * dk]
            kb = k_all_b[sl, h * dk:(h + 1) * dk]
            v = v_ref[sl, h * dv:(h + 1) * dv].astype(BF16)

            a_log = bc + m_state
            d = jnp.where(causal, bc + r_row, NEG_INF)
            m_t = jnp.maximum(a_log, jnp.max(d, axis=-1, keepdims=True))
            inter = jnp.exp(a_log - m_t)
            qk = _dot_nt(q, kb) * jnp.exp(d - m_t)
            num = inter * _dot(q, c_state.astype(BF16)) + _dot(qk.astype(BF16), v)
            qn = jnp.sum(q.astype(F32) * n_state, axis=-1, keepdims=True)
            den = inter * qn + jnp.sum(qk, axis=-1, keepdims=True)
            hout = num / jnp.maximum(jnp.abs(den), jnp.exp(-m_t))
            hn = _rms(hout) * ng[:, h * dv:(h + 1) * dv] * jax.nn.sigmoid(mo_ref[sl, h * dv:(h + 1) * dv])
            o_ref[sl, h * dv:(h + 1) * dv] = hn.astype(BF16)

            g = b_last + r_col
            g_max = jnp.max(g, axis=0, keepdims=True)
            wk = jnp.exp(g - g_max) * k
            c_loc = _dot_tn(wk.astype(BF16), v)
            n_loc = jnp.sum(wk, axis=0, keepdims=True)
            m_new = jnp.maximum(b_last + m_state, g_max)
            a = jnp.exp(b_last + m_state - m_new)
            s = jnp.exp(g_max - m_new)
            c_state = a * c_state + s * c_loc
            n_state = a * n_state + s * n_loc
            m_state = m_new
        c_ref[h] = c_state
        n_ref[h] = n_state
        m_ref[h] = jnp.broadcast_to(m_state, m_ref.shape[1:])


def _mlstm(proj, gate_pre, conv_w, conv_b, norm_g, bsz, seq, qk_col, v_col, o_col, tc=256):
    heads, dk, dv = M_HEADS, M_QK_DIM, M_V_DIM
    qk2 = 2 * heads * dk
    mw = heads * dv
    nt = seq // tc
    n = bsz * seq

    def rows(b, t):
        return b * nt + t

    def prev_rows(b, t):
        return jnp.maximum((b * nt + t) * (tc // SUBLANES) - 1, 0)

    return pl.pallas_call(
        functools.partial(_mlstm_kernel, tc=tc),
        grid=(bsz, nt),
        in_specs=[pl.BlockSpec((tc, qk2), lambda b, t: (rows(b, t), qk_col // qk2)),
                  pl.BlockSpec((SUBLANES, qk2), lambda b, t: (prev_rows(b, t), qk_col // qk2)),
                  pl.BlockSpec((tc, mw), lambda b, t: (rows(b, t), v_col // mw)),
                  pl.BlockSpec((tc, mw), lambda b, t: (rows(b, t), o_col // mw)),
                  pl.BlockSpec((tc, LANES), lambda b, t: (rows(b, t), 0)),
                  pl.BlockSpec((CONV_WIDTH, qk2), lambda b, t: (0, 0)),
                  pl.BlockSpec((1, qk2), lambda b, t: (0, 0)),
                  pl.BlockSpec((1, mw), lambda b, t: (0, 0))],
        out_specs=pl.BlockSpec((tc, mw), lambda b, t: (rows(b, t), 0)),
        out_shape=jax.ShapeDtypeStruct((n, mw), BF16),
        scratch_shapes=[pltpu.VMEM((SUBLANES + tc, qk2), F32),
                        pltpu.VMEM((heads, dk, dv), F32),
                        pltpu.VMEM((heads, 1, dk), F32),
                        pltpu.VMEM((heads, SUBLANES, LANES), F32)],
        compiler_params=_params(("arbitrary", "arbitrary")),
        name="mlstm",
    )(proj, proj, proj, proj, gate_pre, conv_w, conv_b.reshape(1, qk2), norm_g.reshape(1, mw))


def _merge_kernel(oa_ref, hm_ref, gates_ref, h_ref, gt_ref, wa_ref, wm_ref, wo_ref, o_ref):
    d = h_ref.shape[1]
    y_att = _dot(oa_ref[...], wa_ref[...])
    y_m = _dot(hm_ref[...], wm_ref[...])
    merged = jax.nn.sigmoid(gates_ref[:, :d]) * y_att + jax.nn.sigmoid(gates_ref[:, d:]) * y_m
    o_ref[...] = h_ref[...] + gt_ref[0] * _dot(merged.astype(BF16), wo_ref[...])


def _merge(o_att, hm, proj, h, seq, mod3, mod_idx, w_att_up, w_mlstm_up, w_out, gates_col, tm=256):
    n, d = h.shape
    aw = o_att.shape[1]
    mw = hm.shape[1]
    tiles_per_seq = seq // tm
    const = dict(pipeline_mode=pl.Buffered(1))
    return pl.pallas_call(
        _merge_kernel,
        grid=(n // tm,),
        in_specs=[pl.BlockSpec((tm, aw), lambda i: (i, 0)),
                  pl.BlockSpec((tm, mw), lambda i: (i, 0)),
                  pl.BlockSpec((tm, 2 * d), lambda i: (i, gates_col // (2 * d))),
                  pl.BlockSpec((tm, d), lambda i: (i, 0)),
                  pl.BlockSpec((1, 1, d), lambda i: ((i // tiles_per_seq) * N_MOD + mod_idx, 0, 0)),
                  pl.BlockSpec((aw, d), lambda i: (0, 0), **const),
                  pl.BlockSpec((mw, d), lambda i: (0, 0), **const),
                  pl.BlockSpec((d, d), lambda i: (0, 0), **const)],
        out_specs=pl.BlockSpec((tm, d), lambda i: (i, 0)),
        out_shape=jax.ShapeDtypeStruct((n, d), F32),
        compiler_params=_params(("arbitrary",)),
        name="merge",
    )(o_att, hm, proj, h, mod3, w_att_up, w_mlstm_up, w_out)


def kernel(x, c, positions, w_ada, b_ada, norm1_g, ffn1_w_in, ffn1_w_out, norm2_g, w_in, conv_w, conv_b,
           b_igate, b_fgate, sinks, mlstm_norm_g, w_att_up, w_mlstm_up, w_out, norm3_g, ffn2_w_in,
           ffn2_w_out, final_g):
    bsz, seq, d = x.shape
    depth = w_ada.shape[0]
    n = bsz * seq
    aw = ATT_HEADS * ATT_HEAD_DIM
    kvw = ATT_KV_HEADS * ATT_HEAD_DIM
    qk2 = 2 * M_HEADS * M_QK_DIM
    mw = M_HEADS * M_V_DIM

    sizes = (aw, kvw, kvw, qk2, mw, M_HEADS, M_HEADS, mw, 2 * d)
    offs = [0]
    for s in sizes:
        offs.append(offs[-1] + s)
    (o_q, o_k, o_v, o_qk, o_mv, o_i, o_f, o_mo, o_g) = offs[:-1]
    q_col, qk_col, mv_col, mo_col, g_col = 0, aw, aw + qk2, aw + qk2 + mw, aw + qk2 + 2 * mw
    k_col = g_col + 2 * d
    v_col = k_col + kvw

    half = ROPE_DIM // 2
    inv_freq = jnp.power(ROPE_THETA, -jnp.arange(half, dtype=F32) * 2.0 / ROPE_DIM)
    inv_freq = jnp.broadcast_to(inv_freq[:, None], (half, WINDOW))
    pos3 = positions.reshape(n // WINDOW, 1, WINDOW)

    h = x.reshape(n, d)
    for l in range(depth):
        mod3 = _ada(c, w_ada[l], b_ada[l]).reshape(bsz * N_MOD, 1, d)

        h = _ffn(h, seq, norm1_g[l], mod3, 0, ffn1_w_in[l].astype(BF16), ffn1_w_out[l].astype(BF16))

        wl = w_in[l]
        w_packed = jnp.concatenate(
            [wl[:, o_q:o_q + aw], wl[:, o_qk:o_qk + qk2], wl[:, o_mv:o_mv + mw], wl[:, o_mo:o_mo + mw],
             wl[:, o_g:o_g + 2 * d], wl[:, o_k:o_k + kvw], wl[:, o_v:o_v + kvw]], axis=1).astype(BF16)
        w_if = jnp.zeros((d, LANES), F32).at[:, :M_HEADS].set(wl[:, o_i:o_i + M_HEADS])
        w_if = w_if.at[:, M_HEADS:2 * M_HEADS].set(wl[:, o_f:o_f + M_HEADS]).astype(BF16)
        b_if = jnp.zeros((1, LANES), F32).at[0, :M_HEADS].set(b_igate[l]).at[0, M_HEADS:2 * M_HEADS].set(b_fgate[l])
        proj, gate_pre = _proj(h, seq, norm2_g[l], mod3, 3, w_packed, w_if, b_if)

        o_att = _attn(proj, pos3, inv_freq, sinks[l], bsz, seq, q_col, k_col, v_col)
        hm = _mlstm(proj, gate_pre, conv_w[l], conv_b[l], mlstm_norm_g[l], bsz, seq, qk_col, mv_col, mo_col)
        h = _merge(o_att, hm, proj, h, seq, mod3, 5, w_att_up[l].astype(BF16), w_mlstm_up[l].astype(BF16),
                   w_out[l].astype(BF16), g_col)

        last = l == depth - 1
        h = _ffn(h, seq, norm3_g[l], mod3, 6, ffn2_w_in[l].astype(BF16), ffn2_w_out[l].astype(BF16),
                 final_g=final_g if last else None)
    if depth == 0:
        h = _rms(h) * final_g
    return h.reshape(bsz, seq, d)
```

```python
import functools

import jax
import jax.numpy as jnp
from jax import lax
from jax.experimental import pallas as pl
from jax.experimental.pallas import tpu as pltpu

F32 = jnp.float32
BF16 = jnp.bfloat16

ATT_HEADS = 16
ATT_KV_HEADS = 2
ATT_HEAD_DIM = 64
WINDOW = 128
ROPE_DIM = ATT_HEAD_DIM // 4
ROPE_THETA = 500000.0
M_HEADS = 4
M_QK_DIM = 128
M_V_DIM = 256
M_CHUNK = 64
CONV_WIDTH = 4
NORM_EPS = 1e-6
NEG_INF = -1e30
N_MOD = 9

LANES = 128
SUBLANES = 8
VMEM_LIMIT = 56 * 1024 * 1024


def _dot(a, b):
    return jnp.dot(a, b, preferred_element_type=F32)


def _dot_nt(a, b):
    return lax.dot_general(a, b, (((1,), (1,)), ((), ())), preferred_element_type=F32)


def _dot_tn(a, b):
    return lax.dot_general(a, b, (((0,), (0,)), ((), ())), preferred_element_type=F32)


def _split3(x):
    hi = x.astype(BF16)
    r1 = x - hi.astype(F32)
    mid = r1.astype(BF16)
    lo = (r1 - mid.astype(F32)).astype(BF16)
    return hi, mid, lo


def _rms(x):
    return x * lax.rsqrt(jnp.mean(x * x, axis=-1, keepdims=True) + NORM_EPS)


def _params(sem):
    return pltpu.CompilerParams(dimension_semantics=sem, vmem_limit_bytes=VMEM_LIMIT)


def _ada_kernel(c_ref, w_ref, b_ref, o_ref):
    c = c_ref[...]
    hi, mid, lo = _split3(c * jax.nn.sigmoid(c))
    w = w_ref[...].astype(BF16)
    o_ref[...] = _dot(hi, w) + _dot(mid, w) + _dot(lo, w) + b_ref[...]


def _ada(c, w_ada, b_ada, tn=1024):
    bsz, d = c.shape
    n = w_ada.shape[1]
    cp = jnp.zeros((SUBLANES, d), F32).at[:bsz].set(c)
    out = pl.pallas_call(
        _ada_kernel,
        grid=(n // tn,),
        in_specs=[pl.BlockSpec((SUBLANES, d), lambda j: (0, 0)),
                  pl.BlockSpec((d, tn), lambda j: (0, j)),
                  pl.BlockSpec((1, tn), lambda j: (0, j))],
        out_specs=pl.BlockSpec((SUBLANES, tn), lambda j: (0, j)),
        out_shape=jax.ShapeDtypeStruct((SUBLANES, n), F32),
        compiler_params=_params(("arbitrary",)),
        name="ada",
    )(cp, w_ada, b_ada.reshape(1, n))
    return out[:bsz]


def _ffn_kernel(x_ref, g_ref, sh_ref, sc_ref, gt_ref, wa_ref, wb_ref, wo_ref, *rest, final_norm):
    if final_norm:
        fg_ref, o_ref, u_ref = rest
    else:
        o_ref, u_ref = rest
    j = pl.program_id(1)

    @pl.when(j == 0)
    def _():
        y = _rms(x_ref[...]) * g_ref[...]
        u_ref[...] = (y * (1.0 + sc_ref[0]) + sh_ref[0]).astype(BF16)
        o_ref[...] = jnp.zeros_like(o_ref)

    u = u_ref[...]
    a = _dot(u, wa_ref[...])
    b = _dot(u, wb_ref[...])
    o_ref[...] += _dot((a * jax.nn.sigmoid(a) * b).astype(BF16), wo_ref[...])

    @pl.when(j == pl.num_programs(1) - 1)
    def _():
        h = x_ref[...] + (0.5 * gt_ref[0]) * o_ref[...]
        if final_norm:
            h = _rms(h) * fg_ref[...]
        o_ref[...] = h


def _ffn(h, seq, norm_g, mod3, mod_base, w_in, w_out, final_g=None, tm=512, tf=512):
    n, d = h.shape
    dff = w_out.shape[0]
    nj = dff // tf
    tiles_per_seq = seq // tm

    def mod_spec(k):
        return pl.BlockSpec((1, 1, d), lambda i, j: ((i // tiles_per_seq) * N_MOD + mod_base + k, 0, 0))

    in_specs = [pl.BlockSpec((tm, d), lambda i, j: (i, 0)),
                pl.BlockSpec((1, d), lambda i, j: (0, 0)),
                mod_spec(0), mod_spec(1), mod_spec(2),
                pl.BlockSpec((d, tf), lambda i, j: (0, j)),
                pl.BlockSpec((d, tf), lambda i, j: (0, j + nj)),
                pl.BlockSpec((tf, d), lambda i, j: (j, 0))]
    args = [h, norm_g.reshape(1, d), mod3, mod3, mod3, w_in, w_in, w_out]
    if final_g is not None:
        in_specs.append(pl.BlockSpec((1, d), lambda i, j: (0, 0)))
        args.append(final_g.reshape(1, d))
    return pl.pallas_call(
        functools.partial(_ffn_kernel, final_norm=final_g is not None),
        grid=(n // tm, nj),
        in_specs=in_specs,
        out_specs=pl.BlockSpec((tm, d), lambda i, j: (i, 0)),
        out_shape=jax.ShapeDtypeStruct((n, d), F32),
        scratch_shapes=[pltpu.VMEM((tm, d), BF16)],
        compiler_params=_params(("arbitrary", "arbitrary")),
        name="ffn_final" if final_g is not None else "ffn",
    )(*args)


def _proj_kernel(x_ref, g_ref, sh_ref, sc_ref, w_ref, ws_ref, bs_ref, p_ref, s_ref, u_ref):
    @pl.when(pl.program_id(1) == 0)
    def _():
        y = _rms(x_ref[...]) * g_ref[...]
        u = (y * (1.0 + sc_ref[0]) + sh_ref[0]).astype(BF16)
        u_ref[...] = u
        s_ref[...] = _dot(u, ws_ref[...]) + bs_ref[...]

    p_ref[...] = _dot(u_ref[...], w_ref[...])


def _proj(h, seq, norm_g, mod3, mod_base, w, w_small, b_small, tm=1024, tn=1024):
    n, d = h.shape
    ncol = w.shape[1]
    nsmall = w_small.shape[1]
    tiles_per_seq = seq // tm

    def mod_spec(k):
        return pl.BlockSpec((1, 1, d), lambda i, j: ((i // tiles_per_seq) * N_MOD + mod_base + k, 0, 0))

    return pl.pallas_call(
        _proj_kernel,
        grid=(n // tm, ncol // tn),
        in_specs=[pl.BlockSpec((tm, d), lambda i, j: (i, 0)),
                  pl.BlockSpec((1, d), lambda i, j: (0, 0)),
                  mod_spec(0), mod_spec(1),
                  pl.BlockSpec((d, tn), lambda i, j: (0, j)),
                  pl.BlockSpec((d, nsmall), lambda i, j: (0, 0)),
                  pl.BlockSpec((1, nsmall), lambda i, j: (0, 0))],
        out_specs=[pl.BlockSpec((tm, tn), lambda i, j: (i, j)),
                   pl.BlockSpec((tm, nsmall), lambda i, j: (i, 0)),
                   pl.BlockSpec((tm, d), lambda i, j: (i, 0))],
        out_shape=[jax.ShapeDtypeStruct((n, ncol), F32),
                   jax.ShapeDtypeStruct((n, nsmall), F32),
                   jax.ShapeDtypeStruct((n, d), BF16)],
        compiler_params=_params(("arbitrary", "arbitrary")),
        name="proj",
    )(h, norm_g.reshape(1, d), mod3, mod3, w, w_small, b_small)


def _attn_kernel(q_ref, kc_ref, vc_ref, kp_ref, vp_ref, posc_ref, posp_ref, invf_ref, sink_ref, o_ref):
    blk = WINDOW
    lane = lax.broadcasted_iota(jnp.int32, (1, LANES), 1)
    hl = lane % ATT_HEAD_DIM
    half = ROPE_DIM // 2
    lo_half = lane < ATT_HEAD_DIM

    def rope_tables(pos_ref):
        pos = pos_ref[0].astype(F32)
        ang = invf_ref[...] * pos
        reps = LANES // half
        ct = jnp.tile(jnp.cos(ang), (reps, 1)).T
        st = jnp.tile(jnp.sin(ang), (reps, 1)).T
        c = jnp.where(hl < ROPE_DIM, ct, 1.0)
        s = jnp.where(hl < half, -st, jnp.where(hl < ROPE_DIM, st, 0.0))
        return c, s

    def rope(x, c, s):
        partner = jnp.where(hl < half, pltpu.roll(x, LANES - half, 1), pltpu.roll(x, half, 1))
        return x * c + partner * s

    c_cur, s_cur = rope_tables(posc_ref)
    c_prv, s_prv = rope_tables(posp_ref)
    k = jnp.concatenate([rope(kp_ref[...], c_prv, s_prv), rope(kc_ref[...], c_cur, s_cur)], axis=0)
    v = jnp.concatenate([vp_ref[...], vc_ref[...]], axis=0)
    k_sw = pltpu.roll(k, ATT_HEAD_DIM, 1)
    v_sw = pltpu.roll(v, ATT_HEAD_DIM, 1)

    qi = lax.broadcasted_iota(jnp.int32, (blk, 2 * blk), 0) + blk
    kj = lax.broadcasted_iota(jnp.int32, (blk, 2 * blk), 1)
    k_min = jnp.where(pl.program_id(1) == 0, blk, 0)
    mask = (kj <= qi) & (kj > qi - WINDOW) & (kj >= k_min)

    scale = ATT_HEAD_DIM ** -0.5
    pairs_per_group = ATT_HEADS // ATT_KV_HEADS // 2
    ones = jnp.ones((2 * blk, LANES), F32)
    for g in range(ATT_KV_HEADS):
        k_g, k_o = (k, k_sw) if g == 0 else (k_sw, k)
        v_g, v_o = (v, v_sw) if g == 0 else (v_sw, v)
        k_rhs = jnp.concatenate([jnp.where(lo_half, k_g, 0.0), jnp.where(lo_half, 0.0, k_o)], axis=0).astype(BF16)
        v_rhs = jnp.concatenate([jnp.where(lo_half, v_g, 0.0), jnp.where(lo_half, 0.0, v_o)], axis=0)
        one_rhs = jnp.concatenate([jnp.where(lo_half, ones, 0.0), jnp.where(lo_half, 0.0, ones)], axis=0)
        v_rhs = jnp.concatenate([v_rhs, one_rhs], axis=1).astype(BF16)
        q_rows = []
        for pp in range(pairs_per_group):
            p = g * pairs_per_group + pp
            q_p = rope(q_ref[:, p * LANES:(p + 1) * LANES], c_cur, s_cur) * scale
            q_rows.append(q_p.astype(BF16))
        s_all = _dot_nt(jnp.concatenate(q_rows, axis=0), k_rhs)
        p_rows, sink_rows = [], []
        for pp in range(pairs_per_group):
            p = g * pairs_per_group + pp
            halves, m_pair = [], []
            for e in range(2):
                s = s_all[pp * blk:(pp + 1) * blk, e * 2 * blk:(e + 1) * 2 * blk]
                s = jnp.where(mask, s, NEG_INF)
                m = jnp.maximum(jnp.max(s, axis=-1, keepdims=True), sink_ref[2 * p + e])
                halves.append(jnp.exp(s - m).astype(BF16))
                m_pair.append(m)
            p_rows.append(jnp.concatenate(halves, axis=1))
            sink_rows.append(jnp.exp(jnp.where(lo_half, sink_ref[2 * p] - m_pair[0], sink_ref[2 * p + 1] - m_pair[1])))
        o_all = _dot(jnp.concatenate(p_rows, axis=0), v_rhs)
        denom = o_all[:, LANES:] + jnp.concatenate(sink_rows, axis=0)
        o_all = o_all[:, :LANES] * (1.0 / denom)
        for pp in range(pairs_per_group):
            p = g * pairs_per_group + pp
            o_ref[:, p * LANES:(p + 1) * LANES] = o_all[pp * blk:(pp + 1) * blk].astype(BF16)


def _attn(q_src, kv_src, pos3, inv_freq, sinks, bsz, seq, q_col, k_col, v_col):
    blk = WINDOW
    aw = ATT_HEADS * ATT_HEAD_DIM
    nb = seq // blk
    n = bsz * seq
    half = ROPE_DIM // 2

    def cur(b, i):
        return b * nb + i

    def prv(b, i):
        return b * nb + jnp.maximum(i - 1, 0)

    return pl.pallas_call(
        _attn_kernel,
        grid=(bsz, nb),
        in_specs=[pl.BlockSpec((blk, aw), lambda b, i: (cur(b, i), q_col // aw)),
                  pl.BlockSpec((blk, LANES), lambda b, i: (cur(b, i), k_col // LANES)),
                  pl.BlockSpec((blk, LANES), lambda b, i: (cur(b, i), v_col // LANES)),
                  pl.BlockSpec((blk, LANES), lambda b, i: (prv(b, i), k_col // LANES)),
                  pl.BlockSpec((blk, LANES), lambda b, i: (prv(b, i), v_col // LANES)),
                  pl.BlockSpec((1, 1, blk), lambda b, i: (cur(b, i), 0, 0)),
                  pl.BlockSpec((1, 1, blk), lambda b, i: (prv(b, i), 0, 0)),
                  pl.BlockSpec((half, blk), lambda b, i: (0, 0)),
                  pl.BlockSpec(memory_space=pltpu.SMEM)],
        out_specs=pl.BlockSpec((blk, aw), lambda b, i: (cur(b, i), 0)),
        out_shape=jax.ShapeDtypeStruct((n, aw), BF16),
        compiler_params=_params(("arbitrary", "arbitrary")),
        name="attn",
    )(q_src, kv_src, kv_src, kv_src, kv_src, pos3, pos3, inv_freq, sinks)


def _log_sigmoid(x):
    return jnp.minimum(x, 0.0) - jnp.log1p(jnp.exp(-jnp.abs(x)))


def _mlstm_kernel(xqk_ref, xprev_ref, v_ref, mo_ref, if_ref, cw_ref, cb_ref, ng_ref, o_ref,
                  xbuf_ref, c_ref, n_ref, m_ref, *, tc):
    heads, dk, dv, ch = M_HEADS, M_QK_DIM, M_V_DIM, M_CHUNK
    qkw = heads * dk
    t_idx = pl.program_id(1)

    @pl.when(t_idx == 0)
    def _():
        c_ref[...] = jnp.zeros_like(c_ref)
        n_ref[...] = jnp.zeros_like(n_ref)
        m_ref[...] = jnp.full_like(m_ref, NEG_INF)

    pad = SUBLANES
    xbuf_ref[0:pad, :] = jnp.where(t_idx > 0, xprev_ref[...], 0.0)
    xbuf_ref[pad:pad + tc, :] = xqk_ref[...]
    y = cb_ref[...]
    for j in range(CONV_WIDTH):
        off = pad - (CONV_WIDTH - 1) + j
        y = y + cw_ref[j:j + 1, :] * xbuf_ref[off:off + tc, :]
    y = y * jax.nn.sigmoid(y)
    q_all = y[:, :qkw].astype(BF16)
    k_all = y[:, qkw:] * (dk ** -0.5)
    k_all_b = k_all.astype(BF16)

    gates = if_ref[...]
    logf = _log_sigmoid(gates)
    row = lax.broadcasted_iota(jnp.int32, (tc, tc), 0)
    col = lax.broadcasted_iota(jnp.int32, (tc, tc), 1)
    tri = jnp.where((row // ch == col // ch) & (col <= row), 1.0, 0.0).astype(BF16)
    hi, mid, lo = _split3(logf)
    bcum = _dot(tri, hi) + _dot(tri, mid) + _dot(tri, lo)
    lane = lax.broadcasted_iota(jnp.int32, (1, LANES), 1)
    packed_t = jnp.where(lane < heads, gates, bcum).T

    nc = tc // ch
    causal = (lax.broadcasted_iota(jnp.int32, (ch, ch), 1) <= lax.broadcasted_iota(jnp.int32, (ch, ch), 0))
    ng = ng_ref[...]

    def chunks(a):
        return a.reshape(nc, ch, a.shape[-1])

    for h in range(heads):
        ig = chunks(gates[:, h:h + 1])
        bc = chunks(bcum[:, heads + h:heads + h + 1])
        b_last = bc[:, ch - 1:ch, :]
        r_row = packed_t[h:h + 1, :] - packed_t[heads + h:heads + h + 1, :]
        r_row = jnp.stack([r_row[:, c * ch:(c + 1) * ch] for c in range(nc)])
        q = chunks(q_all[:, h * dk:(h + 1) * dk])
        k = chunks(k_all[:, h * dk:(h + 1) * dk])
        kb = chunks(k_all_b[:, h * dk:(h + 1) * dk])
        v = chunks(v_ref[:, h * dv:(h + 1) * dv].astype(BF16))

        g = b_last + ig - bc
        g_max = jnp.max(g, axis=1, keepdims=True)
        wk = jnp.exp(g - g_max) * k
        wkb = wk.astype(BF16)
        c_loc = [_dot_tn(wkb[c], v[c]) for c in range(nc)]
        n_loc = jnp.sum(wk, axis=1, keepdims=True)

        c_state = c_ref[h]
        n_state = n_ref[h]
        m_state = m_ref[h][0:1, 0:1]
        c_in, n_in, m_in = [], [], []
        for c in range(nc):
            c_in.append(c_state.astype(BF16))
            n_in.append(n_state)
            m_in.append(m_state)
            m_new = jnp.maximum(b_last[c] + m_state, g_max[c])
            a = jnp.exp(b_last[c] + m_state - m_new)
            s = jnp.exp(g_max[c] - m_new)
            c_state = a * c_state + s * c_loc[c]
            n_state = a * n_state + s * n_loc[c]
            m_state = m_new
        c_ref[h] = c_state
        n_ref[h] = n_state
        m_ref[h] = jnp.broadcast_to(m_state, m_ref.shape[1:])
        n_in = jnp.stack(n_in)
        m_in = jnp.stack(m_in)

        a_log = bc + m_in
        d = jnp.where(causal, bc + r_row, NEG_INF)
        m_t = jnp.maximum(a_log, jnp.max(d, axis=-1, keepdims=True))
        inter = jnp.exp(a_log - m_t)
        qk = jnp.stack([_dot_nt(q[c], kb[c]) for c in range(nc)]) * jnp.exp(d - m_t)
        qkb = qk.astype(BF16)
        num = inter * jnp.stack([_dot(q[c], c_in[c]) for c in range(nc)]) \
            + jnp.stack([_dot(qkb[c], v[c]) for c in range(nc)])
        qn = jnp.sum(q.astype(F32) * n_in, axis=-1, keepdims=True)
        den = inter * qn + jnp.sum(qk, axis=-1, keepdims=True)
        hout = (num / jnp.maximum(jnp.abs(den), jnp.exp(-m_t))).reshape(tc, dv)
        hn = _rms(hout) * ng[:, h * dv:(h + 1) * dv] * jax.nn.sigmoid(mo_ref[:, h * dv:(h + 1) * dv])
        o_ref[:, h * dv:(h + 1) * dv] = hn.astype(BF16)


def _mlstm(proj, small, conv_w, conv_b, norm_g, bsz, seq, qk_col, v_col, o_col, if_col, tc=256):
    heads, dk, dv = M_HEADS, M_QK_DIM, M_V_DIM
    qk2 = 2 * heads * dk
    mw = heads * dv
    nt = seq // tc
    n = bsz * seq

    def rows(b, t):
        return b * nt + t

    def prev_rows(b, t):
        return jnp.maximum((b * nt + t) * (tc // SUBLANES) - 1, 0)

    return pl.pallas_call(
        functools.partial(_mlstm_kernel, tc=tc),
        grid=(bsz, nt),
        in_specs=[pl.BlockSpec((tc, qk2), lambda b, t: (rows(b, t), qk_col // qk2)),
                  pl.BlockSpec((SUBLANES, qk2), lambda b, t: (prev_rows(b, t), qk_col // qk2)),
                  pl.BlockSpec((tc, mw), lambda b, t: (rows(b, t), v_col // mw)),
                  pl.BlockSpec((tc, mw), lambda b, t: (rows(b, t), o_col // mw)),
                  pl.BlockSpec((tc, LANES), lambda b, t: (rows(b, t), if_col // LANES)),
                  pl.BlockSpec((CONV_WIDTH, qk2), lambda b, t: (0, 0)),
                  pl.BlockSpec((1, qk2), lambda b, t: (0, 0)),
                  pl.BlockSpec((1, mw), lambda b, t: (0, 0))],
        out_specs=pl.BlockSpec((tc, mw), lambda b, t: (rows(b, t), 0)),
        out_shape=jax.ShapeDtypeStruct((n, mw), BF16),
        scratch_shapes=[pltpu.VMEM((SUBLANES + tc, qk2), F32),
                        pltpu.VMEM((heads, dk, dv), F32),
                        pltpu.VMEM((heads, 1, dk), F32),
                        pltpu.VMEM((heads, SUBLANES, LANES), F32)],
        compiler_params=_params(("arbitrary", "arbitrary")),
        name="mlstm",
    )(proj, proj, proj, proj, small, conv_w, conv_b.reshape(1, qk2), norm_g.reshape(1, mw))


def _merge_kernel(oa_ref, hm_ref, u_ref, h_ref, gt_ref, wa_ref, wm_ref, wga_ref, wgm_ref, wo_ref, o_ref,
                  merged_ref, *, nj):
    j = pl.program_id(1)

    @pl.when(j < nj)
    def _():
        u = u_ref[...]
        y_att = _dot(oa_ref[...], wa_ref[...])
        y_m = _dot(hm_ref[...], wm_ref[...])
        g_att = jax.nn.sigmoid(_dot(u, wga_ref[...]))
        g_m = jax.nn.sigmoid(_dot(u, wgm_ref[...]))
        merged_ref[j] = (g_att * y_att + g_m * y_m).astype(BF16)

    @pl.when(j >= nj)
    def _():
        tn = wo_ref.shape[1]
        acc = _dot(merged_ref[0], wo_ref[0:tn, :])
        for k in range(1, nj):
            acc = acc + _dot(merged_ref[k], wo_ref[k * tn:(k + 1) * tn, :])
        o_ref[...] = h_ref[...] + gt_ref[0] * acc


def _merge(o_att, hm, u, h, seq, mod3, mod_idx, w_att_up, w_mlstm_up, w_gates, w_out, tm=512, tn=512):
    n, d = h.shape
    aw = o_att.shape[1]
    mw = hm.shape[1]
    nj = d // tn
    tiles_per_seq = seq // tm

    def first(j):
        return jnp.minimum(j, nj - 1)

    def second(j):
        return jnp.maximum(j - nj, 0)

    return pl.pallas_call(
        functools.partial(_merge_kernel, nj=nj),
        grid=(n // tm, 2 * nj),
        in_specs=[pl.BlockSpec((tm, aw), lambda i, j: (i, 0)),
                  pl.BlockSpec((tm, mw), lambda i, j: (i, 0)),
                  pl.BlockSpec((tm, d), lambda i, j: (i, 0)),
                  pl.BlockSpec((tm, tn), lambda i, j: (i, second(j))),
                  pl.BlockSpec((1, 1, tn), lambda i, j: ((i // tiles_per_seq) * N_MOD + mod_idx, 0, second(j))),
                  pl.BlockSpec((aw, tn), lambda i, j: (0, first(j))),
                  pl.BlockSpec((mw, tn), lambda i, j: (0, first(j))),
                  pl.BlockSpec((d, tn), lambda i, j: (0, first(j))),
                  pl.BlockSpec((d, tn), lambda i, j: (0, nj + first(j))),
                  pl.BlockSpec((d, tn), lambda i, j: (0, second(j)))],
        out_specs=pl.BlockSpec((tm, tn), lambda i, j: (i, second(j))),
        out_shape=jax.ShapeDtypeStruct((n, d), F32),
        scratch_shapes=[pltpu.VMEM((nj, tm, tn), BF16)],
        compiler_params=_params(("arbitrary", "arbitrary")),
        name="merge",
    )(o_att, hm, u, h, mod3, w_att_up, w_mlstm_up, w_gates, w_gates, w_out)


def kernel(x, c, positions, w_ada, b_ada, norm1_g, ffn1_w_in, ffn1_w_out, norm2_g, w_in, conv_w, conv_b,
           b_igate, b_fgate, sinks, mlstm_norm_g, w_att_up, w_mlstm_up, w_out, norm3_g, ffn2_w_in,
           ffn2_w_out, final_g):
    bsz, seq, d = x.shape
    depth = w_ada.shape[0]
    n = bsz * seq
    aw = ATT_HEADS * ATT_HEAD_DIM
    kvw = ATT_KV_HEADS * ATT_HEAD_DIM
    qk2 = 2 * M_HEADS * M_QK_DIM
    mw = M_HEADS * M_V_DIM

    sizes = (aw, kvw, kvw, qk2, mw, M_HEADS, M_HEADS, mw, 2 * d)
    offs = [0]
    for s in sizes:
        offs.append(offs[-1] + s)
    (o_q, o_k, o_v, o_qk, o_mv, o_i, o_f, o_mo, o_g) = offs[:-1]
    q_col, qk_col, mv_col, mo_col = 0, aw, aw + qk2, aw + qk2 + mw
    k_col, v_col, if_col = 0, kvw, 2 * kvw

    half = ROPE_DIM // 2
    inv_freq = jnp.power(ROPE_THETA, -jnp.arange(half, dtype=F32) * 2.0 / ROPE_DIM)
    inv_freq = jnp.broadcast_to(inv_freq[:, None], (half, WINDOW))
    pos3 = positions.reshape(n // WINDOW, 1, WINDOW)

    h = x.reshape(n, d)
    for l in range(depth):
        mod3 = _ada(c, w_ada[l], b_ada[l]).reshape(bsz * N_MOD, 1, d)

        h = _ffn(h, seq, norm1_g[l], mod3, 0, ffn1_w_in[l].astype(BF16), ffn1_w_out[l].astype(BF16))

        wl = w_in[l]
        w_main = jnp.concatenate(
            [wl[:, o_q:o_q + aw], wl[:, o_qk:o_qk + qk2], wl[:, o_mv:o_mv + mw], wl[:, o_mo:o_mo + mw]],
            axis=1).astype(BF16)
        w_small = jnp.concatenate(
            [wl[:, o_k:o_k + 2 * kvw], wl[:, o_i:o_i + 2 * M_HEADS], jnp.zeros((d, LANES - 2 * M_HEADS), F32)],
            axis=1).astype(BF16)
        b_small = jnp.concatenate(
            [jnp.zeros((2 * kvw,), F32), b_igate[l], b_fgate[l], jnp.zeros((LANES - 2 * M_HEADS,), F32)])[None, :]
        w_gates = wl[:, o_g:o_g + 2 * d].astype(BF16)
        proj, small, u = _proj(h, seq, norm2_g[l], mod3, 3, w_main, w_small, b_small)

        o_att = _attn(proj, small, pos3, inv_freq, sinks[l], bsz, seq, q_col, k_col, v_col)
        hm = _mlstm(proj, small, conv_w[l], conv_b[l], mlstm_norm_g[l], bsz, seq, qk_col, mv_col, mo_col, if_col)
        h = _merge(o_att, hm, u, h, seq, mod3, 5, w_att_up[l].astype(BF16), w_mlstm_up[l].astype(BF16),
                   w_gates, w_out[l].astype(BF16))

        last = l == depth - 1
        h = _ffn(h, seq, norm3_g[l], mod3, 6, ffn2_w_in[l].astype(BF16), ffn2_w_out[l].astype(BF16),
                 final_g=final_g if last else None)
    return h.reshape(bsz, seq, d)
```

```python
import functools

import jax
import jax.numpy as jnp
from jax import lax
from jax.experimental import pallas as pl
from jax.experimental.pallas import tpu as pltpu

F32 = jnp.float32
BF16 = jnp.bfloat16

ATT_HEADS = 16
ATT_KV_HEADS = 2
ATT_HEAD_DIM = 64
WINDOW = 128
ROPE_DIM = ATT_HEAD_DIM // 4
ROPE_THETA = 500000.0
M_HEADS = 4
M_QK_DIM = 128
M_V_DIM = 256
M_CHUNK = 64
CONV_WIDTH = 4
NORM_EPS = 1e-6
NEG_INF = -1e30
N_MOD = 9

LANES = 128
SUBLANES = 8
VMEM_LIMIT = 56 * 1024 * 1024


def _dot(a, b):
    return jnp.dot(a, b, preferred_element_type=F32)


def _dot_nt(a, b):
    return lax.dot_general(a, b, (((1,), (1,)), ((), ())), preferred_element_type=F32)


def _dot_tn(a, b):
    return lax.dot_general(a, b, (((0,), (0,)), ((), ())), preferred_element_type=F32)


def _split3(x):
    hi = x.astype(BF16)
    r1 = x - hi.astype(F32)
    mid = r1.astype(BF16)
    lo = (r1 - mid.astype(F32)).astype(BF16)
    return hi, mid, lo


def _rms(x):
    return x * lax.rsqrt(jnp.mean(x * x, axis=-1, keepdims=True) + NORM_EPS)


def _params(sem):
    return pltpu.CompilerParams(dimension_semantics=sem, vmem_limit_bytes=VMEM_LIMIT)


def _ada_kernel(c_ref, w_ref, b_ref, o_ref):
    c = c_ref[...]
    hi, mid, lo = _split3(c * jax.nn.sigmoid(c))
    w = w_ref[...].astype(BF16)
    o_ref[...] = _dot(hi, w) + _dot(mid, w) + _dot(lo, w) + b_ref[...]


def _ada(c, w_ada, b_ada, tn=1024):
    bsz, d = c.shape
    n = w_ada.shape[1]
    cp = jnp.zeros((SUBLANES, d), F32).at[:bsz].set(c)
    out = pl.pallas_call(
        _ada_kernel,
        grid=(n // tn,),
        in_specs=[pl.BlockSpec((SUBLANES, d), lambda j: (0, 0)),
                  pl.BlockSpec((d, tn), lambda j: (0, j)),
                  pl.BlockSpec((1, tn), lambda j: (0, j))],
        out_specs=pl.BlockSpec((SUBLANES, tn), lambda j: (0, j)),
        out_shape=jax.ShapeDtypeStruct((SUBLANES, n), F32),
        compiler_params=_params(("arbitrary",)),
        name="ada",
    )(cp, w_ada, b_ada.reshape(1, n))
    return out[:bsz]


def _ffn_kernel(x_ref, g_ref, sh_ref, sc_ref, gt_ref, wa_ref, wb_ref, wo_ref, *rest, final_norm, rows):
    if final_norm:
        fg_ref, o_ref, u_ref = rest
    else:
        o_ref, u_ref = rest
    j = pl.program_id(1)
    nj = pl.num_programs(1)
    tm = x_ref.shape[0]

    def step(first, last):
        chunk = rows if (first or last) else tm
        for r in range(tm // chunk):
            sl = slice(r * chunk, (r + 1) * chunk)
            if first:
                y = _rms(x_ref[sl, :]) * g_ref[...]
                u = (y * (1.0 + sc_ref[0]) + sh_ref[0]).astype(BF16)
                u_ref[sl, :] = u
            else:
                u = u_ref[sl, :]
            a = _dot(u, wa_ref[...])
            b = _dot(u, wb_ref[...])
            acc = _dot((a * jax.nn.sigmoid(a) * b).astype(BF16), wo_ref[...])
            if not first:
                acc = o_ref[sl, :] + acc
            if last:
                acc = x_ref[sl, :] + (0.5 * gt_ref[0]) * acc
                if final_norm:
                    acc = _rms(acc) * fg_ref[...]
            o_ref[sl, :] = acc

    pl.when(j == 0)(lambda: step(True, False))
    pl.when((j > 0) & (j < nj - 1))(lambda: step(False, False))
    pl.when(j == nj - 1)(lambda: step(False, True))


def _ffn(h, seq, norm_g, mod3, mod_base, w_in, w_out, final_g=None, tm=1024, tf=512, rows=256):
    n, d = h.shape
    dff = w_out.shape[0]
    nj = dff // tf
    assert nj >= 2
    tiles_per_seq = seq // tm

    def mod_spec(k):
        return pl.BlockSpec((1, 1, d), lambda i, j: ((i // tiles_per_seq) * N_MOD + mod_base + k, 0, 0))

    in_specs = [pl.BlockSpec((tm, d), lambda i, j: (i, 0)),
                pl.BlockSpec((1, d), lambda i, j: (0, 0)),
                mod_spec(0), mod_spec(1), mod_spec(2),
                pl.BlockSpec((d, tf), lambda i, j: (0, j)),
                pl.BlockSpec((d, tf), lambda i, j: (0, j + nj)),
                pl.BlockSpec((tf, d), lambda i, j: (j, 0))]
    args = [h, norm_g.reshape(1, d), mod3, mod3, mod3, w_in, w_in, w_out]
    if final_g is not None:
        in_specs.append(pl.BlockSpec((1, d), lambda i, j: (0, 0)))
        args.append(final_g.reshape(1, d))
    return pl.pallas_call(
        functools.partial(_ffn_kernel, final_norm=final_g is not None, rows=rows),
        grid=(n // tm, nj),
        in_specs=in_specs,
        out_specs=pl.BlockSpec((tm, d), lambda i, j: (i, 0)),
        out_shape=jax.ShapeDtypeStruct((n, d), F32),
        scratch_shapes=[pltpu.VMEM((tm, d), BF16)],
        compiler_params=_params(("arbitrary", "arbitrary")),
        name="ffn_final" if final_g is not None else "ffn",
    )(*args)


def _proj_kernel(x_ref, g_ref, sh_ref, sc_ref, w_ref, ws_ref, bs_ref, p_ref, s_ref, u_ref, *, rows):
    j = pl.program_id(1)
    tm = x_ref.shape[0]

    @pl.when(j == 0)
    def _():
        for r in range(tm // rows):
            sl = slice(r * rows, (r + 1) * rows)
            y = _rms(x_ref[sl, :]) * g_ref[...]
            u = (y * (1.0 + sc_ref[0]) + sh_ref[0]).astype(BF16)
            u_ref[sl, :] = u
            s_ref[sl, :] = _dot(u, ws_ref[...]) + bs_ref[...]
            p_ref[sl, :] = _dot(u, w_ref[...])

    @pl.when(j > 0)
    def _():
        p_ref[...] = _dot(u_ref[...], w_ref[...])


def _proj(h, seq, norm_g, mod3, mod_base, w, w_small, b_small, tm=1024, tn=1024, rows=256):
    n, d = h.shape
    ncol = w.shape[1]
    nsmall = w_small.shape[1]
    tiles_per_seq = seq // tm

    def mod_spec(k):
        return pl.BlockSpec((1, 1, d), lambda i, j: ((i // tiles_per_seq) * N_MOD + mod_base + k, 0, 0))

    return pl.pallas_call(
        functools.partial(_proj_kernel, rows=rows),
        grid=(n // tm, ncol // tn),
        in_specs=[pl.BlockSpec((tm, d), lambda i, j: (i, 0)),
                  pl.BlockSpec((1, d), lambda i, j: (0, 0)),
                  mod_spec(0), mod_spec(1),
                  pl.BlockSpec((d, tn), lambda i, j: (0, j)),
                  pl.BlockSpec((d, nsmall), lambda i, j: (0, 0)),
                  pl.BlockSpec((1, nsmall), lambda i, j: (0, 0))],
        out_specs=[pl.BlockSpec((tm, tn), lambda i, j: (i, j)),
                   pl.BlockSpec((tm, nsmall), lambda i, j: (i, 0)),
                   pl.BlockSpec((tm, d), lambda i, j: (i, 0))],
        out_shape=[jax.ShapeDtypeStruct((n, ncol), F32),
                   jax.ShapeDtypeStruct((n, nsmall), F32),
                   jax.ShapeDtypeStruct((n, d), BF16)],
        compiler_params=_params(("arbitrary", "arbitrary")),
        name="proj",
    )(h, norm_g.reshape(1, d), mod3, mod3, w, w_small, b_small)


def _attn_kernel(q_ref, kc_ref, vc_ref, kp_ref, vp_ref, posc_ref, posp_ref, invf_ref, sink_ref, o_ref):
    blk = WINDOW
    lane = lax.broadcasted_iota(jnp.int32, (1, LANES), 1)
    hl = lane % ATT_HEAD_DIM
    half = ROPE_DIM // 2
    lo_half = lane < ATT_HEAD_DIM

    def rope_tables(pos_ref):
        pos = pos_ref[0].astype(F32)
        ang = invf_ref[...] * pos
        reps = LANES // half
        ct = jnp.tile(jnp.cos(ang), (reps, 1)).T
        st = jnp.tile(jnp.sin(ang), (reps, 1)).T
        c = jnp.where(hl < ROPE_DIM, ct, 1.0)
        s = jnp.where(hl < half, -st, jnp.where(hl < ROPE_DIM, st, 0.0))
        return c, s

    def rope(x, c, s):
        partner = jnp.where(hl < half, pltpu.roll(x, LANES - half, 1), pltpu.roll(x, half, 1))
        return x * c + partner * s

    c_cur, s_cur = rope_tables(posc_ref)
    c_prv, s_prv = rope_tables(posp_ref)
    k = jnp.concatenate([rope(kp_ref[...], c_prv, s_prv), rope(kc_ref[...], c_cur, s_cur)], axis=0)
    v = jnp.concatenate([vp_ref[...], vc_ref[...]], axis=0)
    k_sw = pltpu.roll(k, ATT_HEAD_DIM, 1)
    v_sw = pltpu.roll(v, ATT_HEAD_DIM, 1)

    qi = lax.broadcasted_iota(jnp.int32, (blk, 2 * blk), 0) + blk
    kj = lax.broadcasted_iota(jnp.int32, (blk, 2 * blk), 1)
    k_min = jnp.where(pl.program_id(1) == 0, blk, 0)
    mask = (kj <= qi) & (kj > qi - WINDOW) & (kj >= k_min)

    scale = ATT_HEAD_DIM ** -0.5
    pairs_per_group = ATT_HEADS // ATT_KV_HEADS // 2
    ones = jnp.ones((2 * blk, LANES), F32)
    for g in range(ATT_KV_HEADS):
        k_g, k_o = (k, k_sw) if g == 0 else (k_sw, k)
        v_g, v_o = (v, v_sw) if g == 0 else (v_sw, v)
        k_rhs = jnp.concatenate([jnp.where(lo_half, k_g, 0.0), jnp.where(lo_half, 0.0, k_o)], axis=0).astype(BF16)
        v_rhs = jnp.concatenate([jnp.where(lo_half, v_g, 0.0), jnp.where(lo_half, 0.0, v_o)], axis=0)
        one_rhs = jnp.concatenate([jnp.where(lo_half, ones, 0.0), jnp.where(lo_half, 0.0, ones)], axis=0)
        v_rhs = jnp.concatenate([v_rhs, one_rhs], axis=1).astype(BF16)
        q_rows = []
        for pp in range(pairs_per_group):
            p = g * pairs_per_group + pp
            q_p = rope(q_ref[:, p * LANES:(p + 1) * LANES], c_cur, s_cur) * scale
            q_rows.append(q_p.astype(BF16))
        s_all = _dot_nt(jnp.concatenate(q_rows, axis=0), k_rhs)
        p_rows, sink_rows = [], []
        for pp in range(pairs_per_group):
            p = g * pairs_per_group + pp
            halves, m_pair = [], []
            for e in range(2):
                s = s_all[pp * blk:(pp + 1) * blk, e * 2 * blk:(e + 1) * 2 * blk]
                s = jnp.where(mask, s, NEG_INF)
                m = jnp.maximum(jnp.max(s, axis=-1, keepdims=True), sink_ref[2 * p + e])
                halves.append(jnp.exp(s - m).astype(BF16))
                m_pair.append(m)
            p_rows.append(jnp.concatenate(halves, axis=1))
            sink_rows.append(jnp.exp(jnp.where(lo_half, sink_ref[2 * p] - m_pair[0], sink_ref[2 * p + 1] - m_pair[1])))
        o_all = _dot(jnp.concatenate(p_rows, axis=0), v_rhs)
        denom = o_all[:, LANES:] + jnp.concatenate(sink_rows, axis=0)
        o_all = o_all[:, :LANES] * (1.0 / denom)
        for pp in range(pairs_per_group):
            p = g * pairs_per_group + pp
            o_ref[:, p * LANES:(p + 1) * LANES] = o_all[pp * blk:(pp + 1) * blk].astype(BF16)


def _attn(q_src, kv_src, pos3, inv_freq, sinks, bsz, seq, q_col, k_col, v_col):
    blk = WINDOW
    aw = ATT_HEADS * ATT_HEAD_DIM
    nb = seq // blk
    n = bsz * seq
    half = ROPE_DIM // 2

    def cur(b, i):
        return b * nb + i

    def prv(b, i):
        return b * nb + jnp.maximum(i - 1, 0)

    return pl.pallas_call(
        _attn_kernel,
        grid=(bsz, nb),
        in_specs=[pl.BlockSpec((blk, aw), lambda b, i: (cur(b, i), q_col // aw)),
                  pl.BlockSpec((blk, LANES), lambda b, i: (cur(b, i), k_col // LANES)),
                  pl.BlockSpec((blk, LANES), lambda b, i: (cur(b, i), v_col // LANES)),
                  pl.BlockSpec((blk, LANES), lambda b, i: (prv(b, i), k_col // LANES)),
                  pl.BlockSpec((blk, LANES), lambda b, i: (prv(b, i), v_col // LANES)),
                  pl.BlockSpec((1, 1, blk), lambda b, i: (cur(b, i), 0, 0)),
                  pl.BlockSpec((1, 1, blk), lambda b, i: (prv(b, i), 0, 0)),
                  pl.BlockSpec((half, blk), lambda b, i: (0, 0)),
                  pl.BlockSpec(memory_space=pltpu.SMEM)],
        out_specs=pl.BlockSpec((blk, aw), lambda b, i: (cur(b, i), 0)),
        out_shape=jax.ShapeDtypeStruct((n, aw), BF16),
        compiler_params=_params(("arbitrary", "arbitrary")),
        name="attn",
    )(q_src, kv_src, kv_src, kv_src, kv_src, pos3, pos3, inv_freq, sinks)


def _log_sigmoid(x):
    return jnp.minimum(x, 0.0) - jnp.log1p(jnp.exp(-jnp.abs(x)))


def _mlstm_kernel(xqk_ref, xprev_ref, v_ref, mo_ref, if_ref, cw_ref, cb_ref, ng_ref, o_ref,
                  xbuf_ref, c_ref, n_ref, m_ref, *, tc):
    heads, dk, dv, ch = M_HEADS, M_QK_DIM, M_V_DIM, M_CHUNK
    qkw = heads * dk
    t_idx = pl.program_id(1)

    @pl.when(t_idx == 0)
    def _():
        c_ref[...] = jnp.zeros_like(c_ref)
        n_ref[...] = jnp.zeros_like(n_ref)
        m_ref[...] = jnp.full_like(m_ref, NEG_INF)

    pad = SUBLANES
    xbuf_ref[0:pad, :] = jnp.where(t_idx > 0, xprev_ref[...], 0.0)
    xbuf_ref[pad:pad + tc, :] = xqk_ref[...]
    y = cb_ref[...]
    for j in range(CONV_WIDTH):
        off = pad - (CONV_WIDTH - 1) + j
        y = y + cw_ref[j:j + 1, :] * xbuf_ref[off:off + tc, :]
    y = y * jax.nn.sigmoid(y)
    q_all = y[:, :qkw].astype(BF16)
    k_all = y[:, qkw:] * (dk ** -0.5)
    k_all_b = k_all.astype(BF16)

    gates = if_ref[...]
    logf = _log_sigmoid(gates)
    row = lax.broadcasted_iota(jnp.int32, (tc, tc), 0)
    col = lax.broadcasted_iota(jnp.int32, (tc, tc), 1)
    tri = jnp.where((row // ch == col // ch) & (col <= row), 1.0, 0.0).astype(BF16)
    hi, mid, lo = _split3(logf)
    bcum = _dot(tri, hi) + _dot(tri, mid) + _dot(tri, lo)
    lane = lax.broadcasted_iota(jnp.int32, (1, LANES), 1)
    packed_t = jnp.where(lane < heads, gates, bcum).T

    nc = tc // ch
    causal = (lax.broadcasted_iota(jnp.int32, (ch, ch), 1) <= lax.broadcasted_iota(jnp.int32, (ch, ch), 0))
    ng = ng_ref[...]

    def chunks(a):
        return a.reshape(nc, ch, a.shape[-1])

    for h in range(heads):
        ig = chunks(gates[:, h:h + 1])
        bc = chunks(bcum[:, heads + h:heads + h + 1])
        b_last = bc[:, ch - 1:ch, :]
        r_row = packed_t[h:h + 1, :] - packed_t[heads + h:heads + h + 1, :]
        r_row = jnp.stack([r_row[:, c * ch:(c + 1) * ch] for c in range(nc)])
        q = chunks(q_all[:, h * dk:(h + 1) * dk])
        k = chunks(k_all[:, h * dk:(h + 1) * dk])
        kb = chunks(k_all_b[:, h * dk:(h + 1) * dk])
        v = chunks(v_ref[:, h * dv:(h + 1) * dv].astype(BF16))

        g = b_last + ig - bc
        g_max = jnp.max(g, axis=1, keepdims=True)
        wk = jnp.exp(g - g_max) * k
        wkb = wk.astype(BF16)
        c_loc = [_dot_tn(wkb[c], v[c]) for c in range(nc)]
        n_loc = jnp.sum(wk, axis=1, keepdims=True)

        c_state = c_ref[h]
        n_state = n_ref[h]
        m_state = m_ref[h][0:1, 0:1]
        c_in, n_in, m_in = [], [], []
        for c in range(nc):
            c_in.append(c_state.astype(BF16))
            n_in.append(n_state)
            m_in.append(m_state)
            m_new = jnp.maximum(b_last[c] + m_state, g_max[c])
            a = jnp.exp(b_last[c] + m_state - m_new)
            s = jnp.exp(g_max[c] - m_new)
            c_state = a * c_state + s * c_loc[c]
            n_state = a * n_state + s * n_loc[c]
            m_state = m_new
        c_ref[h] = c_state
        n_ref[h] = n_state
        m_ref[h] = jnp.broadcast_to(m_state, m_ref.shape[1:])
        n_in = jnp.stack(n_in)
        m_in = jnp.stack(m_in)

        a_log = bc + m_in
        d = jnp.where(causal, bc + r_row, NEG_INF)
        m_t = jnp.maximum(a_log, jnp.max(d, axis=-1, keepdims=True))
        inter = jnp.exp(a_log - m_t)
        qk = jnp.stack([_dot_nt(q[c], kb[c]) for c in range(nc)]) * jnp.exp(d - m_t)
        qkb = qk.astype(BF16)
        num = inter * jnp.stack([_dot(q[c], c_in[c]) for c in range(nc)]) \
            + jnp.stack([_dot(qkb[c], v[c]) for c in range(nc)])
        qn = jnp.sum(q.astype(F32) * n_in, axis=-1, keepdims=True)
        den = inter * qn + jnp.sum(qk, axis=-1, keepdims=True)
        hout = (num / jnp.maximum(jnp.abs(den), jnp.exp(-m_t))).reshape(tc, dv)
        hn = _rms(hout) * ng[:, h * dv:(h + 1) * dv] * jax.nn.sigmoid(mo_ref[:, h * dv:(h + 1) * dv])
        o_ref[:, h * dv:(h + 1) * dv] = hn.astype(BF16)


def _mlstm(proj, small, conv_w, conv_b, norm_g, bsz, seq, qk_col, v_col, o_col, if_col, tc=512):
    heads, dk, dv = M_HEADS, M_QK_DIM, M_V_DIM
    qk2 = 2 * heads * dk
    mw = heads * dv
    nt = seq // tc
    n = bsz * seq

    def rows(b, t):
        return b * nt + t

    def prev_rows(b, t):
        return jnp.maximum((b * nt + t) * (tc // SUBLANES) - 1, 0)

    return pl.pallas_call(
        functools.partial(_mlstm_kernel, tc=tc),
        grid=(bsz, nt),
        in_specs=[pl.BlockSpec((tc, qk2), lambda b, t: (rows(b, t), qk_col // qk2)),
                  pl.BlockSpec((SUBLANES, qk2), lambda b, t: (prev_rows(b, t), qk_col // qk2)),
                  pl.BlockSpec((tc, mw), lambda b, t: (rows(b, t), v_col // mw)),
                  pl.BlockSpec((tc, mw), lambda b, t: (rows(b, t), o_col // mw)),
                  pl.BlockSpec((tc, LANES), lambda b, t: (rows(b, t), if_col // LANES)),
                  pl.BlockSpec((CONV_WIDTH, qk2), lambda b, t: (0, 0)),
                  pl.BlockSpec((1, qk2), lambda b, t: (0, 0)),
                  pl.BlockSpec((1, mw), lambda b, t: (0, 0))],
        out_specs=pl.BlockSpec((tc, mw), lambda b, t: (rows(b, t), 0)),
        out_shape=jax.ShapeDtypeStruct((n, mw), BF16),
        scratch_shapes=[pltpu.VMEM((SUBLANES + tc, qk2), F32),
                        pltpu.VMEM((heads, dk, dv), F32),
                        pltpu.VMEM((heads, 1, dk), F32),
                        pltpu.VMEM((heads, SUBLANES, LANES), F32)],
        compiler_params=_params(("arbitrary", "arbitrary")),
        name="mlstm",
    )(proj, proj, proj, proj, small, conv_w, conv_b.reshape(1, qk2), norm_g.reshape(1, mw))


def _merge_kernel(oa_ref, hm_ref, u_ref, h_ref, gt_ref, wa_ref, wm_ref, wga_ref, wgm_ref, wo_ref, o_ref, *, rows):
    j = pl.program_id(1)
    nj = pl.num_programs(1)
    tm = h_ref.shape[0]

    def step(first, last):
        for r in range(tm // rows):
            sl = slice(r * rows, (r + 1) * rows)
            u = u_ref[sl, :]
            y_att = _dot(oa_ref[sl, :], wa_ref[...])
            y_m = _dot(hm_ref[sl, :], wm_ref[...])
            g_att = jax.nn.sigmoid(_dot(u, wga_ref[...]))
            g_m = jax.nn.sigmoid(_dot(u, wgm_ref[...]))
            acc = _dot((g_att * y_att + g_m * y_m).astype(BF16), wo_ref[...])
            if not first:
                acc = o_ref[sl, :] + acc
            if last:
                acc = h_ref[sl, :] + gt_ref[0] * acc
            o_ref[sl, :] = acc

    pl.when(j == 0)(lambda: step(True, False))
    pl.when((j > 0) & (j < nj - 1))(lambda: step(False, False))
    pl.when(j == nj - 1)(lambda: step(False, True))


def _merge(o_att, hm, u, h, seq, mod3, mod_idx, w_att_up, w_mlstm_up, w_gates, w_out, tm=512, tn=512, rows=256):
    n, d = h.shape
    aw = o_att.shape[1]
    mw = hm.shape[1]
    nj = d // tn
    assert nj >= 2
    tiles_per_seq = seq // tm
    return pl.pallas_call(
        functools.partial(_merge_kernel, rows=rows),
        grid=(n // tm, nj),
        in_specs=[pl.BlockSpec((tm, aw), lambda i, j: (i, 0)),
                  pl.BlockSpec((tm, mw), lambda i, j: (i, 0)),
                  pl.BlockSpec((tm, d), lambda i, j: (i, 0)),
                  pl.BlockSpec((tm, d), lambda i, j: (i, 0)),
                  pl.BlockSpec((1, 1, d), lambda i, j: ((i // tiles_per_seq) * N_MOD + mod_idx, 0, 0)),
                  pl.BlockSpec((aw, tn), lambda i, j: (0, j)),
                  pl.BlockSpec((mw, tn), lambda i, j: (0, j)),
                  pl.BlockSpec((d, tn), lambda i, j: (0, j)),
                  pl.BlockSpec((d, tn), lambda i, j: (0, nj + j)),
                  pl.BlockSpec((tn, d), lambda i, j: (j, 0))],
        out_specs=pl.BlockSpec((tm, d), lambda i, j: (i, 0)),
        out_shape=jax.ShapeDtypeStruct((n, d), F32),
        compiler_params=_params(("arbitrary", "arbitrary")),
        name="merge",
    )(o_att, hm, u, h, mod3, w_att_up, w_mlstm_up, w_gates, w_gates, w_out)


def kernel(x, c, positions, w_ada, b_ada, norm1_g, ffn1_w_in, ffn1_w_out, norm2_g, w_in, conv_w, conv_b,
           b_igate, b_fgate, sinks, mlstm_norm_g, w_att_up, w_mlstm_up, w_out, norm3_g, ffn2_w_in,
           ffn2_w_out, final_g):
    bsz, seq, d = x.shape
    depth = w_ada.shape[0]
    n = bsz * seq
    aw = ATT_HEADS * ATT_HEAD_DIM
    kvw = ATT_KV_HEADS * ATT_HEAD_DIM
    qk2 = 2 * M_HEADS * M_QK_DIM
    mw = M_HEADS * M_V_DIM

    sizes = (aw, kvw, kvw, qk2, mw, M_HEADS, M_HEADS, mw, 2 * d)
    offs = [0]
    for s in sizes:
        offs.append(offs[-1] + s)
    (o_q, o_k, o_v, o_qk, o_mv, o_i, o_f, o_mo, o_g) = offs[:-1]
    q_col, qk_col, mv_col, mo_col = 0, aw, aw + qk2, aw + qk2 + mw
    k_col, v_col, if_col = 0, kvw, 2 * kvw

    half = ROPE_DIM // 2
    inv_freq = jnp.power(ROPE_THETA, -jnp.arange(half, dtype=F32) * 2.0 / ROPE_DIM)
    inv_freq = jnp.broadcast_to(inv_freq[:, None], (half, WINDOW))
    pos3 = positions.reshape(n // WINDOW, 1, WINDOW)

    h = x.reshape(n, d)
    for l in range(depth):
        mod3 = _ada(c, w_ada[l], b_ada[l]).reshape(bsz * N_MOD, 1, d)

        h = _ffn(h, seq, norm1_g[l], mod3, 0, ffn1_w_in[l].astype(BF16), ffn1_w_out[l].astype(BF16))

        wl = w_in[l]
        w_main = jnp.concatenate(
            [wl[:, o_q:o_q + aw], wl[:, o_qk:o_qk + qk2], wl[:, o_mv:o_mv + mw], wl[:, o_mo:o_mo + mw]],
            axis=1).astype(BF16)
        w_small = jnp.concatenate(
            [wl[:, o_k:o_k + 2 * kvw], wl[:, o_i:o_i + 2 * M_HEADS], jnp.zeros((d, LANES - 2 * M_HEADS), F32)],
            axis=1).astype(BF16)
        b_small = jnp.concatenate(
            [jnp.zeros((2 * kvw,), F32), b_igate[l], b_fgate[l], jnp.zeros((LANES - 2 * M_HEADS,), F32)])[None, :]
        w_gates = wl[:, o_g:o_g + 2 * d].astype(BF16)
        proj, small, u = _proj(h, seq, norm2_g[l], mod3, 3, w_main, w_small, b_small)

        o_att = _attn(proj, small, pos3, inv_freq, sinks[l], bsz, seq, q_col, k_col, v_col)
        hm = _mlstm(proj, small, conv_w[l], conv_b[l], mlstm_norm_g[l], bsz, seq, qk_col, mv_col, mo_col, if_col)
        h = _merge(o_att, hm, u, h, seq, mod3, 5, w_att_up[l].astype(BF16), w_mlstm_up[l].astype(BF16),
                   w_gates, w_out[l].astype(BF16))

        last = l == depth - 1
        h = _ffn(h, seq, norm3_g[l], mod3, 6, ffn2_w_in[l].astype(BF16), ffn2_w_out[l].astype(BF16),
                 final_g=final_g if last else None)
    return h.reshape(bsz, seq, d)
```

```python
import functools

import jax
import jax.numpy as jnp
from jax import lax
from jax.experimental import pallas as pl
from jax.experimental.pallas import tpu as pltpu

F32 = jnp.float32
BF16 = jnp.bfloat16

ATT_HEADS = 16
ATT_KV_HEADS = 2
ATT_HEAD_DIM = 64
WINDOW = 128
ROPE_DIM = ATT_HEAD_DIM // 4
ROPE_THETA = 500000.0
M_HEADS = 4
M_QK_DIM = 128
M_V_DIM = 256
M_CHUNK = 64
CONV_WIDTH = 4
NORM_EPS = 1e-6
NEG_INF = -1e30
N_MOD = 9

LANES = 128
SUBLANES = 8
VMEM_LIMIT = 56 * 1024 * 1024


def _dot(a, b):
    return jnp.dot(a, b, preferred_element_type=F32)


def _dot_nt(a, b):
    return lax.dot_general(a, b, (((1,), (1,)), ((), ())), preferred_element_type=F32)


def _dot_tn(a, b):
    return lax.dot_general(a, b, (((0,), (0,)), ((), ())), preferred_element_type=F32)


def _split3(x):
    hi = x.astype(BF16)
    r1 = x - hi.astype(F32)
    mid = r1.astype(BF16)
    lo = (r1 - mid.astype(F32)).astype(BF16)
    return hi, mid, lo


def _rms(x):
    return x * lax.rsqrt(jnp.mean(x * x, axis=-1, keepdims=True) + NORM_EPS)


def _params(sem):
    return pltpu.CompilerParams(dimension_semantics=sem, vmem_limit_bytes=VMEM_LIMIT)


def _ada_kernel(c_ref, w_ref, b_ref, o_ref):
    c = c_ref[...]
    hi, mid, lo = _split3(c * jax.nn.sigmoid(c))
    w = w_ref[...].astype(BF16)
    o_ref[...] = _dot(hi, w) + _dot(mid, w) + _dot(lo, w) + b_ref[...]


def _ada(c, w_ada, b_ada, tn=1024):
    bsz, d = c.shape
    n = w_ada.shape[1]
    cp = jnp.zeros((SUBLANES, d), F32).at[:bsz].set(c)
    out = pl.pallas_call(
        _ada_kernel,
        grid=(n // tn,),
        in_specs=[pl.BlockSpec((SUBLANES, d), lambda j: (0, 0)),
                  pl.BlockSpec((d, tn), lambda j: (0, j)),
                  pl.BlockSpec((1, tn), lambda j: (0, j))],
        out_specs=pl.BlockSpec((SUBLANES, tn), lambda j: (0, j)),
        out_shape=jax.ShapeDtypeStruct((SUBLANES, n), F32),
        compiler_params=_params(("arbitrary",)),
        name="ada",
    )(cp, w_ada, b_ada.reshape(1, n))
    return out[:bsz]


def _ffn_kernel(x_ref, g_ref, sh_ref, sc_ref, gt_ref, wa_ref, wb_ref, wo_ref, *rest, final_norm, rows):
    if final_norm:
        fg_ref, o_ref, u_ref = rest
    else:
        o_ref, u_ref = rest
    j = pl.program_id(1)
    nj = pl.num_programs(1)
    tm = x_ref.shape[0]

    def step(first, last):
        chunk = rows if (first or last) else tm
        for r in range(tm // chunk):
            sl = slice(r * chunk, (r + 1) * chunk)
            if first:
                y = _rms(x_ref[sl, :]) * g_ref[...]
                u = (y * (1.0 + sc_ref[0]) + sh_ref[0]).astype(BF16)
                u_ref[sl, :] = u
            else:
                u = u_ref[sl, :]
            a = _dot(u, wa_ref[...])
            b = _dot(u, wb_ref[...])
            acc = _dot((a * jax.nn.sigmoid(a) * b).astype(BF16), wo_ref[...])
            if not first:
                acc = o_ref[sl, :] + acc
            if last:
                acc = x_ref[sl, :] + (0.5 * gt_ref[0]) * acc
                if final_norm:
                    acc = _rms(acc) * fg_ref[...]
            o_ref[sl, :] = acc

    pl.when(j == 0)(lambda: step(True, False))
    pl.when((j > 0) & (j < nj - 1))(lambda: step(False, False))
    pl.when(j == nj - 1)(lambda: step(False, True))


def _ffn(h, seq, norm_g, mod3, mod_base, w_in, w_out, final_g=None, tm=1024, tf=512, rows=256):
    n, d = h.shape
    dff = w_out.shape[0]
    nj = dff // tf
    assert nj >= 2
    tiles_per_seq = seq // tm

    def mod_spec(k):
        return pl.BlockSpec((1, 1, d), lambda i, j: ((i // tiles_per_seq) * N_MOD + mod_base + k, 0, 0))

    in_specs = [pl.BlockSpec((tm, d), lambda i, j: (i, 0)),
                pl.BlockSpec((1, d), lambda i, j: (0, 0)),
                mod_spec(0), mod_spec(1), mod_spec(2),
                pl.BlockSpec((d, tf), lambda i, j: (0, j)),
                pl.BlockSpec((d, tf), lambda i, j: (0, j + nj)),
                pl.BlockSpec((tf, d), lambda i, j: (j, 0))]
    args = [h, norm_g.reshape(1, d), mod3, mod3, mod3, w_in, w_in, w_out]
    if final_g is not None:
        in_specs.append(pl.BlockSpec((1, d), lambda i, j: (0, 0)))
        args.append(final_g.reshape(1, d))
    return pl.pallas_call(
        functools.partial(_ffn_kernel, final_norm=final_g is not None, rows=rows),
        grid=(n // tm, nj),
        in_specs=in_specs,
        out_specs=pl.BlockSpec((tm, d), lambda i, j: (i, 0)),
        out_shape=jax.ShapeDtypeStruct((n, d), F32),
        scratch_shapes=[pltpu.VMEM((tm, d), BF16)],
        compiler_params=_params(("arbitrary", "arbitrary")),
        name="ffn_final" if final_g is not None else "ffn",
    )(*args)


def _proj_kernel(x_ref, g_ref, sh_ref, sc_ref, w_ref, ws_ref, bs_ref, p_ref, s_ref, u_ref, *, rows):
    j = pl.program_id(1)
    tm = x_ref.shape[0]

    @pl.when(j == 0)
    def _():
        for r in range(tm // rows):
            sl = slice(r * rows, (r + 1) * rows)
            y = _rms(x_ref[sl, :]) * g_ref[...]
            u = (y * (1.0 + sc_ref[0]) + sh_ref[0]).astype(BF16)
            u_ref[sl, :] = u
            s_ref[sl, :] = _dot(u, ws_ref[...]) + bs_ref[...]
            p_ref[sl, :] = _dot(u, w_ref[...])

    @pl.when(j > 0)
    def _():
        p_ref[...] = _dot(u_ref[...], w_ref[...])


def _proj(h, seq, norm_g, mod3, mod_base, w, w_small, b_small, tm=1024, tn=1024, rows=256):
    n, d = h.shape
    ncol = w.shape[1]
    nsmall = w_small.shape[1]
    tiles_per_seq = seq // tm

    def mod_spec(k):
        return pl.BlockSpec((1, 1, d), lambda i, j: ((i // tiles_per_seq) * N_MOD + mod_base + k, 0, 0))

    return pl.pallas_call(
        functools.partial(_proj_kernel, rows=rows),
        grid=(n // tm, ncol // tn),
        in_specs=[pl.BlockSpec((tm, d), lambda i, j: (i, 0)),
                  pl.BlockSpec((1, d), lambda i, j: (0, 0)),
                  mod_spec(0), mod_spec(1),
                  pl.BlockSpec((d, tn), lambda i, j: (0, j)),
                  pl.BlockSpec((d, nsmall), lambda i, j: (0, 0)),
                  pl.BlockSpec((1, nsmall), lambda i, j: (0, 0))],
        out_specs=[pl.BlockSpec((tm, tn), lambda i, j: (i, j)),
                   pl.BlockSpec((tm, nsmall), lambda i, j: (i, 0)),
                   pl.BlockSpec((tm, d), lambda i, j: (i, 0))],
        out_shape=[jax.ShapeDtypeStruct((n, ncol), F32),
                   jax.ShapeDtypeStruct((n, nsmall), F32),
                   jax.ShapeDtypeStruct((n, d), BF16)],
        compiler_params=_params(("arbitrary", "arbitrary")),
        name="proj",
    )(h, norm_g.reshape(1, d), mod3, mod3, w, w_small, b_small)


def _attn_kernel(q_ref, kc_ref, vc_ref, kp_ref, vp_ref, posc_ref, posp_ref, invf_ref, sink_ref, o_ref, *, nsub):
    blk = WINDOW
    lane = lax.broadcasted_iota(jnp.int32, (1, LANES), 1)
    hl = lane % ATT_HEAD_DIM
    half = ROPE_DIM // 2
    lo_half = lane < ATT_HEAD_DIM

    def rope_tables(pos):
        pos = pos.astype(F32)
        ang = invf_ref[...] * pos
        reps = LANES // half
        ct = jnp.tile(jnp.cos(ang), (reps, 1)).T
        st = jnp.tile(jnp.sin(ang), (reps, 1)).T
        c = jnp.where(hl < ROPE_DIM, ct, 1.0)
        s = jnp.where(hl < ROPE_DIM, st, 0.0)
        return c, s

    def swap(x):
        return jnp.where(hl < half, pltpu.roll(x, LANES - half, 1),
                         jnp.where(hl < ROPE_DIM, -pltpu.roll(x, half, 1), 0.0))

    def rope(x, c, s):
        return x * c - swap(x) * s

    tabs = [rope_tables(posp_ref[0])] + [rope_tables(posc_ref[t]) for t in range(nsub)]
    k = jnp.concatenate([rope(kp_ref[...], *tabs[0])]
                        + [rope(kc_ref[t * blk:(t + 1) * blk, :], *tabs[t + 1]) for t in range(nsub)], axis=0)
    v = jnp.concatenate([vp_ref[...], vc_ref[...]], axis=0)
    ks = swap(k)
    k_sw, ks_sw = pltpu.roll(k, ATT_HEAD_DIM, 1), pltpu.roll(ks, ATT_HEAD_DIM, 1)
    v_sw = pltpu.roll(v, ATT_HEAD_DIM, 1)

    qi = lax.broadcasted_iota(jnp.int32, (blk, 2 * blk), 0) + blk
    kj = lax.broadcasted_iota(jnp.int32, (blk, 2 * blk), 1)
    band = (kj <= qi) & (kj > qi - WINDOW)
    k_min = jnp.where(pl.program_id(1) == 0, blk, 0)
    band_first = band & (kj >= k_min)

    scale = ATT_HEAD_DIM ** -0.5
    pairs_per_group = ATT_HEADS // ATT_KV_HEADS // 2
    ones = jnp.ones((2 * blk, LANES), F32)
    one_rhs = jnp.concatenate([jnp.where(lo_half, ones, 0.0), jnp.where(lo_half, 0.0, ones)], axis=0)
    for g in range(ATT_KV_HEADS):
        k_g, k_o = (k, k_sw) if g == 0 else (k_sw, k)
        ks_g, ks_o = (ks, ks_sw) if g == 0 else (ks_sw, ks)
        v_g, v_o = (v, v_sw) if g == 0 else (v_sw, v)
        k_lo = jnp.concatenate([jnp.where(lo_half, k_g, 0.0), jnp.where(lo_half, ks_g, 0.0)], axis=1).astype(BF16)
        k_hi = jnp.concatenate([jnp.where(lo_half, 0.0, k_o), jnp.where(lo_half, 0.0, ks_o)], axis=1).astype(BF16)
        v_lo, v_hi = jnp.where(lo_half, v_g, 0.0), jnp.where(lo_half, 0.0, v_o)
        for t in range(nsub):
            rows = slice(t * blk, (t + 1) * blk)
            keys = slice(t * blk, (t + 2) * blk)
            mask = band_first if t == 0 else band
            k_rhs = jnp.concatenate([k_lo[keys], k_hi[keys]], axis=0)
            c_q, s_q = tabs[t + 1][0] * scale, tabs[t + 1][1] * scale
            v_rhs = jnp.concatenate([jnp.concatenate([v_lo[keys], v_hi[keys]], axis=0), one_rhs],
                                    axis=1).astype(BF16)
            q_rows = []
            for pp in range(pairs_per_group):
                p = g * pairs_per_group + pp
                q_p = q_ref[rows, p * LANES:(p + 1) * LANES]
                q_rows.append(jnp.concatenate([q_p * c_q, q_p * s_q], axis=1).astype(BF16))
            s_all = _dot_nt(jnp.concatenate(q_rows, axis=0), k_rhs)
            p_rows, sink_rows = [], []
            for pp in range(pairs_per_group):
                p = g * pairs_per_group + pp
                halves, m_pair = [], []
                for e in range(2):
                    s = s_all[pp * blk:(pp + 1) * blk, e * 2 * blk:(e + 1) * 2 * blk]
                    s = jnp.where(mask, s, NEG_INF)
                    m = jnp.maximum(jnp.max(s, axis=-1, keepdims=True), sink_ref[2 * p + e])
                    halves.append(jnp.exp(s - m).astype(BF16))
                    m_pair.append(m)
                p_rows.append(jnp.concatenate(halves, axis=1))
                sink_rows.append(
                    jnp.exp(jnp.where(lo_half, sink_ref[2 * p] - m_pair[0], sink_ref[2 * p + 1] - m_pair[1])))
            o_all = _dot(jnp.concatenate(p_rows, axis=0), v_rhs)
            denom = o_all[:, LANES:] + jnp.concatenate(sink_rows, axis=0)
            o_all = o_all[:, :LANES] * (1.0 / denom)
            for pp in range(pairs_per_group):
                p = g * pairs_per_group + pp
                o_ref[rows, p * LANES:(p + 1) * LANES] = o_all[pp * blk:(pp + 1) * blk].astype(BF16)


def _attn(q_src, kv_src, pos3, inv_freq, sinks, bsz, seq, q_col, k_col, v_col, nsub=1):
    blk = WINDOW
    aw = ATT_HEADS * ATT_HEAD_DIM
    ns = seq // (nsub * blk)
    n = bsz * seq
    half = ROPE_DIM // 2

    def cur(b, i):
        return b * ns + i

    def prv(b, i):
        return b * ns * nsub + jnp.maximum(i * nsub - 1, 0)

    return pl.pallas_call(
        functools.partial(_attn_kernel, nsub=nsub),
        grid=(bsz, ns),
        in_specs=[pl.BlockSpec((nsub * blk, aw), lambda b, i: (cur(b, i), q_col // aw)),
                  pl.BlockSpec((nsub * blk, LANES), lambda b, i: (cur(b, i), k_col // LANES)),
                  pl.BlockSpec((nsub * blk, LANES), lambda b, i: (cur(b, i), v_col // LANES)),
                  pl.BlockSpec((blk, LANES), lambda b, i: (prv(b, i), k_col // LANES)),
                  pl.BlockSpec((blk, LANES), lambda b, i: (prv(b, i), v_col // LANES)),
                  pl.BlockSpec((nsub, 1, blk), lambda b, i: (cur(b, i), 0, 0)),
                  pl.BlockSpec((1, 1, blk), lambda b, i: (prv(b, i), 0, 0)),
                  pl.BlockSpec((half, blk), lambda b, i: (0, 0)),
                  pl.BlockSpec(memory_space=pltpu.SMEM)],
        out_specs=pl.BlockSpec((nsub * blk, aw), lambda b, i: (cur(b, i), 0)),
        out_shape=jax.ShapeDtypeStruct((n, aw), BF16),
        compiler_params=_params(("arbitrary", "arbitrary")),
        name="attn",
    )(q_src, kv_src, kv_src, kv_src, kv_src, pos3, pos3, inv_freq, sinks)


def _log_sigmoid(x):
    return jnp.minimum(x, 0.0) - jnp.log1p(jnp.exp(-jnp.abs(x)))


def _mlstm_kernel(xqk_ref, xprev_ref, v_ref, mo_ref, if_ref, cw_ref, cb_ref, ng_ref, o_ref,
                  xbuf_ref, c_ref, m_ref, *, tc):
    heads, dk, dv, ch = M_HEADS, M_QK_DIM, M_V_DIM, M_CHUNK
    qkw = heads * dk
    nc = tc // ch
    assert ch * 2 == LANES and nc % 2 == 0 and 2 * heads <= SUBLANES
    t_idx = pl.program_id(1)

    @pl.when(t_idx == 0)
    def _():
        c_ref[...] = jnp.zeros_like(c_ref)
        m_ref[...] = jnp.full_like(m_ref, NEG_INF)

    pad = SUBLANES
    xbuf_ref[0:pad, :] = jnp.where(t_idx > 0, xprev_ref[...], 0.0)
    xbuf_ref[pad:pad + tc, :] = xqk_ref[...]
    y = cb_ref[...]
    for j in range(CONV_WIDTH):
        off = pad - (CONV_WIDTH - 1) + j
        y = y + cw_ref[j:j + 1, :] * xbuf_ref[off:off + tc, :]
    y = y * jax.nn.sigmoid(y)
    q_all = y[:, :qkw].astype(BF16)
    k_all = y[:, qkw:] * (dk ** -0.5)
    k_all_b = k_all.astype(BF16)

    gates = if_ref[...]
    logf = _log_sigmoid(gates)
    row = lax.broadcasted_iota(jnp.int32, (tc, tc), 0)
    col = lax.broadcasted_iota(jnp.int32, (tc, tc), 1)
    tri = jnp.where((row // ch == col // ch) & (col <= row), 1.0, 0.0).astype(BF16)
    hi, mid, lo = _split3(logf)
    bcum = _dot(tri, hi) + _dot(tri, mid) + _dot(tri, lo)
    lane = lax.broadcasted_iota(jnp.int32, (1, LANES), 1)
    packed_t = jnp.where(lane < heads, gates, bcum).T

    gt8 = packed_t[0:SUBLANES, :]
    bc_t = pltpu.roll(gt8, heads, 0)
    r_t = gt8 - bc_t
    pos = lax.broadcasted_iota(jnp.int32, (1, tc), 1) % ch
    nblk = tc // LANES

    def lane_blocks(a, f):
        return jnp.concatenate([f(a[:, j * LANES:(j + 1) * LANES]) for j in range(nblk)], axis=1)

    rmax_t = r_t
    for sh in (1, 2, 4, 8, 16, 32):
        back = lane_blocks(rmax_t, lambda blk, sh=sh: pltpu.roll(blk, sh, 1))
        rmax_t = jnp.maximum(rmax_t, jnp.where(pos >= sh, back, NEG_INF))

    last = [c * ch + ch - 1 for c in range(nc)]
    b_last = [bc_t[:, i:i + 1] for i in last]
    g_max = [bc_t[:, i:i + 1] + rmax_t[:, i:i + 1] for i in last]
    m_state = m_ref[:, 0:1]
    m_in, m_out = [], []
    for c in range(nc):
        m_in.append(m_state)
        m_state = jnp.maximum(b_last[c] + m_state, g_max[c])
        m_out.append(m_state)
    m_ref[...] = jnp.broadcast_to(m_state, m_ref.shape)
    decay = [jnp.exp(b_last[c] + m_in[c] - m_out[c]) for c in range(nc)]

    def per_time(cols):
        return jnp.concatenate([jnp.where(lane < ch, cols[2 * j], cols[2 * j + 1]) for j in range(nblk)], axis=1)

    m_in_t, m_out_t, b_last_t = per_time(m_in), per_time(m_out), per_time(b_last)
    w_t = jnp.exp(b_last_t + r_t - m_out_t)
    a_log_t = bc_t + m_in_t
    m_t_t = jnp.maximum(a_log_t, bc_t + rmax_t)
    inter_t = jnp.exp(a_log_t - m_t_t)
    eneg_t = jnp.exp(-m_t_t)
    sub = lax.broadcasted_iota(jnp.int32, (SUBLANES, 1), 0)
    cols_t = jnp.concatenate(
        [jnp.where(sub < heads, w_t, pltpu.roll(inter_t, heads, 0)),
         jnp.where(sub < heads, eneg_t, pltpu.roll(m_t_t, heads, 0)),
         jnp.zeros((LANES - 2 * SUBLANES, tc), F32)], axis=0)
    cols = cols_t.T

    causal = (lax.broadcasted_iota(jnp.int32, (ch, ch), 1) <= lax.broadcasted_iota(jnp.int32, (ch, ch), 0))
    ng = ng_ref[...]
    ones_v = jnp.ones((tc, LANES), BF16)
    ones_sq = jnp.ones((dv, LANES), BF16)

    def chunks(a):
        return a.reshape(nc, ch, a.shape[-1])

    for h in range(heads):
        w_col = cols[:, h:h + 1]
        inter = cols[:, heads + h:heads + h + 1]
        eneg = cols[:, 2 * heads + h:2 * heads + h + 1]
        m_t = cols[:, 3 * heads + h:3 * heads + h + 1]
        bc = bcum[:, heads + h:heads + h + 1]
        r_row = jnp.stack([r_t[h:h + 1, c * ch:(c + 1) * ch] for c in range(nc)])
        q = chunks(q_all[:, h * dk:(h + 1) * dk])
        kb = chunks(k_all_b[:, h * dk:(h + 1) * dk])
        v = chunks(jnp.concatenate([v_ref[:, h * dv:(h + 1) * dv].astype(BF16), ones_v], axis=1))
        wkb = chunks((w_col * k_all[:, h * dk:(h + 1) * dk]).astype(BF16))

        c_state = c_ref[h]
        c_in = []
        for c in range(nc):
            c_in.append(c_state.astype(BF16))
            c_state = decay[c][h:h + 1, :] * c_state + _dot_tn(wkb[c], v[c])
        c_ref[h] = c_state

        d = jnp.where(causal, chunks(bc) + r_row, NEG_INF)
        qk = jnp.stack([_dot_nt(q[c], kb[c]) for c in range(nc)]) * jnp.exp(d - chunks(m_t))
        qkb = qk.astype(BF16)
        num = chunks(inter) * jnp.stack([_dot(q[c], c_in[c]) for c in range(nc)]) \
            + jnp.stack([_dot(qkb[c], v[c]) for c in range(nc)])
        num = num.reshape(tc, dv + LANES)
        inv = 1.0 / jnp.maximum(jnp.abs(num[:, dv:]), eneg)
        halves = [num[:, i * LANES:(i + 1) * LANES] * inv for i in range(dv // LANES)]
        sq = jnp.concatenate([x * x for x in halves], axis=1).astype(BF16)
        rs = lax.rsqrt(_dot(sq, ones_sq) * (1.0 / dv) + NORM_EPS)
        for i, x in enumerate(halves):
            sl = slice(h * dv + i * LANES, h * dv + (i + 1) * LANES)
            o_ref[:, sl] = (x * rs * ng[:, sl] * jax.nn.sigmoid(mo_ref[:, sl])).astype(BF16)


def _mlstm(proj, small, conv_w, conv_b, norm_g, bsz, seq, qk_col, v_col, o_col, if_col, tc=512):
    heads, dk, dv = M_HEADS, M_QK_DIM, M_V_DIM
    qk2 = 2 * heads * dk
    mw = heads * dv
    nt = seq // tc
    n = bsz * seq

    def rows(b, t):
        return b * nt + t

    def prev_rows(b, t):
        return jnp.maximum((b * nt + t) * (tc // SUBLANES) - 1, 0)

    return pl.pallas_call(
        functools.partial(_mlstm_kernel, tc=tc),
        grid=(bsz, nt),
        in_specs=[pl.BlockSpec((tc, qk2), lambda b, t: (rows(b, t), qk_col // qk2)),
                  pl.BlockSpec((SUBLANES, qk2), lambda b, t: (prev_rows(b, t), qk_col // qk2)),
                  pl.BlockSpec((tc, mw), lambda b, t: (rows(b, t), v_col // mw)),
                  pl.BlockSpec((tc, mw), lambda b, t: (rows(b, t), o_col // mw)),
                  pl.BlockSpec((tc, LANES), lambda b, t: (rows(b, t), if_col // LANES)),
                  pl.BlockSpec((CONV_WIDTH, qk2), lambda b, t: (0, 0)),
                  pl.BlockSpec((1, qk2), lambda b, t: (0, 0)),
                  pl.BlockSpec((1, mw), lambda b, t: (0, 0))],
        out_specs=pl.BlockSpec((tc, mw), lambda b, t: (rows(b, t), 0)),
        out_shape=jax.ShapeDtypeStruct((n, mw), BF16),
        scratch_shapes=[pltpu.VMEM((SUBLANES + tc, qk2), F32),
                        pltpu.VMEM((heads, dk, dv + LANES), F32),
                        pltpu.VMEM((SUBLANES, LANES), F32)],
        compiler_params=_params(("arbitrary", "arbitrary")),
        name="mlstm",
    )(proj, proj, proj, proj, small, conv_w, conv_b.reshape(1, qk2), norm_g.reshape(1, mw))


def _merge_kernel(oa_ref, hm_ref, u_ref, h_ref, gt_ref, wa_ref, wm_ref, wga_ref, wgm_ref, wo_ref, o_ref, *, rows):
    j = pl.program_id(1)
    nj = pl.num_programs(1)
    tm = h_ref.shape[0]

    def step(first, last):
        for r in range(tm // rows):
            sl = slice(r * rows, (r + 1) * rows)
            u = u_ref[sl, :]
            y_att = _dot(oa_ref[sl, :], wa_ref[...])
            y_m = _dot(hm_ref[sl, :], wm_ref[...])
            g_att = jax.nn.sigmoid(_dot(u, wga_ref[...]))
            g_m = jax.nn.sigmoid(_dot(u, wgm_ref[...]))
            acc = _dot((g_att * y_att + g_m * y_m).astype(BF16), wo_ref[...])
            if not first:
                acc = o_ref[sl, :] + acc
            if last:
                acc = h_ref[sl, :] + gt_ref[0] * acc
            o_ref[sl, :] = acc

    pl.when(j == 0)(lambda: step(True, False))
    pl.when((j > 0) & (j < nj - 1))(lambda: step(False, False))
    pl.when(j == nj - 1)(lambda: step(False, True))


def _merge(o_att, hm, u, h, seq, mod3, mod_idx, w_att_up, w_mlstm_up, w_gates, w_out, tm=512, tn=512, rows=256):
    n, d = h.shape
    aw = o_att.shape[1]
    mw = hm.shape[1]
    nj = d // tn
    assert nj >= 2
    tiles_per_seq = seq // tm
    return pl.pallas_call(
        functools.partial(_merge_kernel, rows=rows),
        grid=(n // tm, nj),
        in_specs=[pl.BlockSpec((tm, aw), lambda i, j: (i, 0)),
                  pl.BlockSpec((tm, mw), lambda i, j: (i, 0)),
                  pl.BlockSpec((tm, d), lambda i, j: (i, 0)),
                  pl.BlockSpec((tm, d), lambda i, j: (i, 0)),
                  pl.BlockSpec((1, 1, d), lambda i, j: ((i // tiles_per_seq) * N_MOD + mod_idx, 0, 0)),
                  pl.BlockSpec((aw, tn), lambda i, j: (0, j)),
                  pl.BlockSpec((mw, tn), lambda i, j: (0, j)),
                  pl.BlockSpec((d, tn), lambda i, j: (0, j)),
                  pl.BlockSpec((d, tn), lambda i, j: (0, nj + j)),
                  pl.BlockSpec((tn, d), lambda i, j: (j, 0))],
        out_specs=pl.BlockSpec((tm, d), lambda i, j: (i, 0)),
        out_shape=jax.ShapeDtypeStruct((n, d), F32),
        compiler_params=_params(("arbitrary", "arbitrary")),
        name="merge",
    )(o_att, hm, u, h, mod3, w_att_up, w_mlstm_up, w_gates, w_gates, w_out)


def kernel(x, c, positions, w_ada, b_ada, norm1_g, ffn1_w_in, ffn1_w_out, norm2_g, w_in, conv_w, conv_b,
           b_igate, b_fgate, sinks, mlstm_norm_g, w_att_up, w_mlstm_up, w_out, norm3_g, ffn2_w_in,
           ffn2_w_out, final_g):
    bsz, seq, d = x.shape
    depth = w_ada.shape[0]
    n = bsz * seq
    aw = ATT_HEADS * ATT_HEAD_DIM
    kvw = ATT_KV_HEADS * ATT_HEAD_DIM
    qk2 = 2 * M_HEADS * M_QK_DIM
    mw = M_HEADS * M_V_DIM

    sizes = (aw, kvw, kvw, qk2, mw, M_HEADS, M_HEADS, mw, 2 * d)
    offs = [0]
    for s in sizes:
        offs.append(offs[-1] + s)
    (o_q, o_k, o_v, o_qk, o_mv, o_i, o_f, o_mo, o_g) = offs[:-1]
    q_col, qk_col, mv_col, mo_col = 0, aw, aw + qk2, aw + qk2 + mw
    k_col, v_col, if_col = 0, kvw, 2 * kvw

    half = ROPE_DIM // 2
    inv_freq = jnp.power(ROPE_THETA, -jnp.arange(half, dtype=F32) * 2.0 / ROPE_DIM)
    inv_freq = jnp.broadcast_to(inv_freq[:, None], (half, WINDOW))
    pos3 = positions.reshape(n // WINDOW, 1, WINDOW)

    h = x.reshape(n, d)
    for l in range(depth):
        mod3 = _ada(c, w_ada[l], b_ada[l]).reshape(bsz * N_MOD, 1, d)

        h = _ffn(h, seq, norm1_g[l], mod3, 0, ffn1_w_in[l].astype(BF16), ffn1_w_out[l].astype(BF16))

        wl = w_in[l]
        w_main = jnp.concatenate(
            [wl[:, o_q:o_q + aw], wl[:, o_qk:o_qk + qk2], wl[:, o_mv:o_mv + mw], wl[:, o_mo:o_mo + mw]],
            axis=1).astype(BF16)
        w_small = jnp.concatenate(
            [wl[:, o_k:o_k + 2 * kvw], wl[:, o_i:o_i + 2 * M_HEADS], jnp.zeros((d, LANES - 2 * M_HEADS), F32)],
            axis=1).astype(BF16)
        b_small = jnp.concatenate(
            [jnp.zeros((2 * kvw,), F32), b_igate[l], b_fgate[l], jnp.zeros((LANES - 2 * M_HEADS,), F32)])[None, :]
        w_gates = wl[:, o_g:o_g + 2 * d].astype(BF16)
        proj, small, u = _proj(h, seq, norm2_g[l], mod3, 3, w_main, w_small, b_small)

        o_att = _attn(proj, small, pos3, inv_freq, sinks[l], bsz, seq, q_col, k_col, v_col)
        hm = _mlstm(proj, small, conv_w[l], conv_b[l], mlstm_norm_g[l], bsz, seq, qk_col, mv_col, mo_col, if_col)
        h = _merge(o_att, hm, u, h, seq, mod3, 5, w_att_up[l].astype(BF16), w_mlstm_up[l].astype(BF16),
                   w_gates, w_out[l].astype(BF16))

        last = l == depth - 1
        h = _ffn(h, seq, norm3_g[l], mod3, 6, ffn2_w_in[l].astype(BF16), ffn2_w_out[l].astype(BF16),
                 final_g=final_g if last else None)
    return h.reshape(bsz, seq, d)
```

```python
import functools

import jax
import jax.numpy as jnp
from jax import lax
from jax.experimental import pallas as pl
from jax.experimental.pallas import tpu as pltpu

F32 = jnp.float32
BF16 = jnp.bfloat16

ATT_HEADS = 16
ATT_KV_HEADS = 2
ATT_HEAD_DIM = 64
WINDOW = 128
ROPE_DIM = ATT_HEAD_DIM // 4
ROPE_THETA = 500000.0
M_HEADS = 4
M_QK_DIM = 128
M_V_DIM = 256
M_CHUNK = 64
CONV_WIDTH = 4
NORM_EPS = 1e-6
NEG_INF = -1e30
N_MOD = 9

LANES = 128
SUBLANES = 8
VMEM_LIMIT = 60 * 1024 * 1024


def _dot(a, b):
    return jnp.dot(a, b, preferred_element_type=F32)


def _dot_nt(a, b):
    return lax.dot_general(a, b, (((1,), (1,)), ((), ())), preferred_element_type=F32)


def _dot_tn(a, b):
    return lax.dot_general(a, b, (((0,), (0,)), ((), ())), preferred_element_type=F32)


def _split3(x):
    hi = x.astype(BF16)
    r1 = x - hi.astype(F32)
    mid = r1.astype(BF16)
    lo = (r1 - mid.astype(F32)).astype(BF16)
    return hi, mid, lo


def _rms(x):
    return x * lax.rsqrt(jnp.mean(x * x, axis=-1, keepdims=True) + NORM_EPS)


def _params(sem):
    return pltpu.CompilerParams(dimension_semantics=sem, vmem_limit_bytes=VMEM_LIMIT)


def _ada_kernel(c_ref, w_ref, b_ref, o_ref):
    c = c_ref[...]
    hi, mid, lo = _split3(c * jax.nn.sigmoid(c))
    w = w_ref[...].astype(BF16)
    o_ref[...] = _dot(hi, w) + _dot(mid, w) + _dot(lo, w) + b_ref[...]


def _ada(c, w_ada, b_ada, tn=2048):
    bsz, d = c.shape
    n = w_ada.shape[1]
    cp = jnp.zeros((SUBLANES, d), F32).at[:bsz].set(c)
    out = pl.pallas_call(
        _ada_kernel,
        grid=(n // tn,),
        in_specs=[pl.BlockSpec((SUBLANES, d), lambda j: (0, 0)),
                  pl.BlockSpec((d, tn), lambda j: (0, j)),
                  pl.BlockSpec((1, tn), lambda j: (0, j))],
        out_specs=pl.BlockSpec((SUBLANES, tn), lambda j: (0, j)),
        out_shape=jax.ShapeDtypeStruct((SUBLANES, n), F32),
        compiler_params=_params(("arbitrary",)),
        name="ada",
    )(cp, w_ada, b_ada.reshape(1, n))
    return out[:bsz]


def _ffn_kernel(x_ref, g_ref, sh_ref, sc_ref, gt_ref, wa_ref, wb_ref, wo_ref, *rest, final_norm, rows, n_cast):
    rest = list(rest)
    fg_ref = rest.pop(0) if final_norm else None
    cast_in, o_ref, cast_out, u_ref = rest[:n_cast], rest[n_cast], rest[n_cast + 1:2 * n_cast + 1], rest[-1]
    j = pl.program_id(1)
    nj = pl.num_programs(1)
    tm = x_ref.shape[0]

    for src, dst in zip(cast_in, cast_out):
        dst[...] = src[...].astype(BF16)

    def step(first, last):
        chunk = rows if (first or last) else tm
        for r in range(tm // chunk):
            sl = slice(r * chunk, (r + 1) * chunk)
            if first:
                y = _rms(x_ref[sl, :]) * g_ref[...]
                u = (y * (1.0 + sc_ref[0]) + sh_ref[0]).astype(BF16)
                u_ref[sl, :] = u
            else:
                u = u_ref[sl, :]
            a = _dot(u, wa_ref[...])
            b = _dot(u, wb_ref[...])
            acc = _dot((a * jax.nn.sigmoid(a) * b).astype(BF16), wo_ref[...])
            if not first:
                acc = o_ref[sl, :] + acc
            if last:
                acc = x_ref[sl, :] + (0.5 * gt_ref[0]) * acc
                if final_norm:
                    acc = _rms(acc) * fg_ref[...]
            o_ref[sl, :] = acc

    pl.when(j == 0)(lambda: step(True, False))
    pl.when((j > 0) & (j < nj - 1))(lambda: step(False, False))
    pl.when(j == nj - 1)(lambda: step(False, True))


def _cast_spec(shape, n_i, n_j):
    r, c = shape
    sub, lanes = 2 * SUBLANES, LANES
    if r % n_i == 0 and c % n_j == 0 and (r // n_i) % sub == 0 and (c // n_j) % lanes == 0:
        return pl.BlockSpec((r // n_i, c // n_j), lambda i, j: (i, j))
    if r % n_j == 0 and c % n_i == 0 and (r // n_j) % sub == 0 and (c // n_i) % lanes == 0:
        return pl.BlockSpec((r // n_j, c // n_i), lambda i, j: (j, i))
    assert r % n_i == 0 and (r // n_i) % sub == 0, shape
    return pl.BlockSpec((r // n_i, c), lambda i, j: (i, 0))


def _ffn(h, seq, norm_g, mod3, mod_base, w_in, w_out, final_g=None, casts=(), tm=1024, tf=512, rows=256):
    n, d = h.shape
    dff = w_out.shape[0]
    nj = dff // tf
    assert nj >= 2
    tiles_per_seq = seq // tm

    def mod_spec(k):
        return pl.BlockSpec((1, 1, d), lambda i, j: ((i // tiles_per_seq) * N_MOD + mod_base + k, 0, 0))

    in_specs = [pl.BlockSpec((tm, d), lambda i, j: (i, 0)),
                pl.BlockSpec((1, d), lambda i, j: (0, 0)),
                mod_spec(0), mod_spec(1), mod_spec(2),
                pl.BlockSpec((d, tf), lambda i, j: (0, j)),
                pl.BlockSpec((d, tf), lambda i, j: (0, j + nj)),
                pl.BlockSpec((tf, d), lambda i, j: (j, 0))]
    args = [h, norm_g.reshape(1, d), mod3, mod3, mod3, w_in, w_in, w_out]
    if final_g is not None:
        in_specs.append(pl.BlockSpec((1, d), lambda i, j: (0, 0)))
        args.append(final_g.reshape(1, d))
    cast_specs = [_cast_spec(w.shape, n // tm, nj) for w in casts]
    outs = pl.pallas_call(
        functools.partial(_ffn_kernel, final_norm=final_g is not None, rows=rows, n_cast=len(casts)),
        grid=(n // tm, nj),
        in_specs=in_specs + cast_specs,
        out_specs=[pl.BlockSpec((tm, d), lambda i, j: (i, 0))] + [_cast_spec(w.shape, n // tm, nj) for w in casts],
        out_shape=[jax.ShapeDtypeStruct((n, d), F32)] + [jax.ShapeDtypeStruct(w.shape, BF16) for w in casts],
        scratch_shapes=[pltpu.VMEM((tm, d), BF16)],
        compiler_params=_params(("arbitrary", "arbitrary")),
        name="ffn_final" if final_g is not None else "ffn",
    )(*args, *casts)
    return outs[0], tuple(outs[1:])


def _proj_kernel(x_ref, g_ref, sh_ref, sc_ref, w_ref, ws_ref, bs_ref, p_ref, s_ref, u_ref, *, rows):
    j = pl.program_id(1)
    tm = x_ref.shape[0]

    @pl.when(j == 0)
    def _():
        for r in range(tm // rows):
            sl = slice(r * rows, (r + 1) * rows)
            y = _rms(x_ref[sl, :]) * g_ref[...]
            u = (y * (1.0 + sc_ref[0]) + sh_ref[0]).astype(BF16)
            u_ref[sl, :] = u
            s_ref[sl, :] = _dot(u, ws_ref[...]) + bs_ref[...]
            p_ref[sl, :] = _dot(u, w_ref[...])

    @pl.when(j > 0)
    def _():
        p_ref[...] = _dot(u_ref[...], w_ref[...])


def _proj(h, seq, norm_g, mod3, mod_base, w, w_small, b_small, tm=1024, tn=1024, rows=256):
    n, d = h.shape
    ncol = w.shape[1]
    nsmall = w_small.shape[1]
    tiles_per_seq = seq // tm

    def mod_spec(k):
        return pl.BlockSpec((1, 1, d), lambda i, j: ((i // tiles_per_seq) * N_MOD + mod_base + k, 0, 0))

    return pl.pallas_call(
        functools.partial(_proj_kernel, rows=rows),
        grid=(n // tm, ncol // tn),
        in_specs=[pl.BlockSpec((tm, d), lambda i, j: (i, 0)),
                  pl.BlockSpec((1, d), lambda i, j: (0, 0)),
                  mod_spec(0), mod_spec(1),
                  pl.BlockSpec((d, tn), lambda i, j: (0, j)),
                  pl.BlockSpec((d, nsmall), lambda i, j: (0, 0)),
                  pl.BlockSpec((1, nsmall), lambda i, j: (0, 0))],
        out_specs=[pl.BlockSpec((tm, tn), lambda i, j: (i, j)),
                   pl.BlockSpec((tm, nsmall), lambda i, j: (i, 0)),
                   pl.BlockSpec((tm, d), lambda i, j: (i, 0))],
        out_shape=[jax.ShapeDtypeStruct((n, ncol), F32),
                   jax.ShapeDtypeStruct((n, nsmall), F32),
                   jax.ShapeDtypeStruct((n, d), BF16)],
        compiler_params=_params(("arbitrary", "arbitrary")),
        name="proj",
    )(h, norm_g.reshape(1, d), mod3, mod3, w, w_small, b_small)


def _attn_kernel(q_ref, kc_ref, vc_ref, kp_ref, vp_ref, posc_ref, posp_ref, invf_ref, sink_ref, o_ref, *, nsub):
    blk = WINDOW
    lane = lax.broadcasted_iota(jnp.int32, (1, LANES), 1)
    hl = lane % ATT_HEAD_DIM
    half = ROPE_DIM // 2
    lo_half = lane < ATT_HEAD_DIM

    def rope_tables(pos):
        pos = pos.astype(F32)
        ang = invf_ref[...] * pos
        reps = LANES // half
        ct = jnp.tile(jnp.cos(ang), (reps, 1)).T
        st = jnp.tile(jnp.sin(ang), (reps, 1)).T
        c = jnp.where(hl < ROPE_DIM, ct, 1.0)
        s = jnp.where(hl < ROPE_DIM, st, 0.0)
        return c, s

    def swap(x):
        return jnp.where(hl < half, pltpu.roll(x, LANES - half, 1),
                         jnp.where(hl < ROPE_DIM, -pltpu.roll(x, half, 1), 0.0))

    def rope(x, c, s):
        return x * c - swap(x) * s

    tabs = [rope_tables(posp_ref[0])] + [rope_tables(posc_ref[t]) for t in range(nsub)]
    k = jnp.concatenate([rope(kp_ref[...], *tabs[0])]
                        + [rope(kc_ref[t * blk:(t + 1) * blk, :], *tabs[t + 1]) for t in range(nsub)], axis=0)
    v = jnp.concatenate([vp_ref[...], vc_ref[...]], axis=0)
    ks = swap(k)
    k_sw, ks_sw = pltpu.roll(k, ATT_HEAD_DIM, 1), pltpu.roll(ks, ATT_HEAD_DIM, 1)
    v_sw = pltpu.roll(v, ATT_HEAD_DIM, 1)

    qi = lax.broadcasted_iota(jnp.int32, (blk, 2 * blk), 0) + blk
    kj = lax.broadcasted_iota(jnp.int32, (blk, 2 * blk), 1)
    band = (kj <= qi) & (kj > qi - WINDOW)
    k_min = jnp.where(pl.program_id(1) == 0, blk, 0)
    band_first = band & (kj >= k_min)

    scale = ATT_HEAD_DIM ** -0.5
    pairs_per_group = ATT_HEADS // ATT_KV_HEADS // 2
    ones = jnp.ones((2 * blk, LANES), F32)
    one_rhs = jnp.concatenate([jnp.where(lo_half, ones, 0.0), jnp.where(lo_half, 0.0, ones)], axis=0)
    for g in range(ATT_KV_HEADS):
        k_g, k_o = (k, k_sw) if g == 0 else (k_sw, k)
        ks_g, ks_o = (ks, ks_sw) if g == 0 else (ks_sw, ks)
        v_g, v_o = (v, v_sw) if g == 0 else (v_sw, v)
        k_lo = jnp.concatenate([jnp.where(lo_half, k_g, 0.0), jnp.where(lo_half, ks_g, 0.0)], axis=1).astype(BF16)
        k_hi = jnp.concatenate([jnp.where(lo_half, 0.0, k_o), jnp.where(lo_half, 0.0, ks_o)], axis=1).astype(BF16)
        v_lo, v_hi = jnp.where(lo_half, v_g, 0.0), jnp.where(lo_half, 0.0, v_o)
        for t in range(nsub):
            rows = slice(t * blk, (t + 1) * blk)
            keys = slice(t * blk, (t + 2) * blk)
            mask = band_first if t == 0 else band
            k_rhs = jnp.concatenate([k_lo[keys], k_hi[keys]], axis=0)
            c_q, s_q = tabs[t + 1][0] * scale, tabs[t + 1][1] * scale
            v_rhs = jnp.concatenate([jnp.concatenate([v_lo[keys], v_hi[keys]], axis=0), one_rhs],
                                    axis=1).astype(BF16)
            q_rows = []
            for pp in range(pairs_per_group):
                p = g * pairs_per_group + pp
                q_p = q_ref[rows, p * LANES:(p + 1) * LANES]
                q_rows.append(jnp.concatenate([q_p * c_q, q_p * s_q], axis=1).astype(BF16))
            s_all = _dot_nt(jnp.concatenate(q_rows, axis=0), k_rhs)
            p_rows, sink_rows = [], []
            for pp in range(pairs_per_group):
                p = g * pairs_per_group + pp
                halves, m_pair = [], []
                for e in range(2):
                    s = s_all[pp * blk:(pp + 1) * blk, e * 2 * blk:(e + 1) * 2 * blk]
                    s = jnp.where(mask, s, NEG_INF)
                    m = jnp.maximum(jnp.max(s, axis=-1, keepdims=True), sink_ref[2 * p + e])
                    halves.append(jnp.exp(s - m).astype(BF16))
                    m_pair.append(m)
                p_rows.append(jnp.concatenate(halves, axis=1))
                sink_rows.append(
                    jnp.exp(jnp.where(lo_half, sink_ref[2 * p] - m_pair[0], sink_ref[2 * p + 1] - m_pair[1])))
            o_all = _dot(jnp.concatenate(p_rows, axis=0), v_rhs)
            denom = o_all[:, LANES:] + jnp.concatenate(sink_rows, axis=0)
            o_all = o_all[:, :LANES] * (1.0 / denom)
            for pp in range(pairs_per_group):
                p = g * pairs_per_group + pp
                o_ref[rows, p * LANES:(p + 1) * LANES] = o_all[pp * blk:(pp + 1) * blk].astype(BF16)


def _attn(q_src, kv_src, pos3, inv_freq, sinks, bsz, seq, q_col, k_col, v_col, nsub=1):
    blk = WINDOW
    aw = ATT_HEADS * ATT_HEAD_DIM
    ns = seq // (nsub * blk)
    n = bsz * seq
    half = ROPE_DIM // 2

    def cur(b, i):
        return b * ns + i

    def prv(b, i):
        return b * ns * nsub + jnp.maximum(i * nsub - 1, 0)

    return pl.pallas_call(
        functools.partial(_attn_kernel, nsub=nsub),
        grid=(bsz, ns),
        in_specs=[pl.BlockSpec((nsub * blk, aw), lambda b, i: (cur(b, i), q_col // aw)),
                  pl.BlockSpec((nsub * blk, LANES), lambda b, i: (cur(b, i), k_col // LANES)),
                  pl.BlockSpec((nsub * blk, LANES), lambda b, i: (cur(b, i), v_col // LANES)),
                  pl.BlockSpec((blk, LANES), lambda b, i: (prv(b, i), k_col // LANES)),
                  pl.BlockSpec((blk, LANES), lambda b, i: (prv(b, i), v_col // LANES)),
                  pl.BlockSpec((nsub, 1, blk), lambda b, i: (cur(b, i), 0, 0)),
                  pl.BlockSpec((1, 1, blk), lambda b, i: (prv(b, i), 0, 0)),
                  pl.BlockSpec((half, blk), lambda b, i: (0, 0)),
                  pl.BlockSpec(memory_space=pltpu.SMEM)],
        out_specs=pl.BlockSpec((nsub * blk, aw), lambda b, i: (cur(b, i), 0)),
        out_shape=jax.ShapeDtypeStruct((n, aw), BF16),
        compiler_params=_params(("arbitrary", "arbitrary")),
        name="attn",
    )(q_src, kv_src, kv_src, kv_src, kv_src, pos3, pos3, inv_freq, sinks)


def _log_sigmoid(x):
    return jnp.minimum(x, 0.0) - jnp.log1p(jnp.exp(-jnp.abs(x)))


def _mlstm_kernel(xqk_ref, xprev_ref, v_ref, mo_ref, if_ref, cw_ref, cb_ref, ng_ref, o_ref,
                  xbuf_ref, c_ref, m_ref, *, tc):
    heads, dk, dv, ch = M_HEADS, M_QK_DIM, M_V_DIM, M_CHUNK
    qkw = heads * dk
    nc = tc // ch
    assert ch * 2 == LANES and nc % 2 == 0 and 2 * heads <= SUBLANES
    t_idx = pl.program_id(1)

    @pl.when(t_idx == 0)
    def _():
        c_ref[...] = jnp.zeros_like(c_ref)
        m_ref[...] = jnp.full_like(m_ref, NEG_INF)

    pad = SUBLANES
    xbuf_ref[0:pad, :] = jnp.where(t_idx > 0, xprev_ref[...], 0.0)
    xbuf_ref[pad:pad + tc, :] = xqk_ref[...]
    y = cb_ref[...]
    for j in range(CONV_WIDTH):
        off = pad - (CONV_WIDTH - 1) + j
        y = y + cw_ref[j:j + 1, :] * xbuf_ref[off:off + tc, :]
    y = y * jax.nn.sigmoid(y)
    q_all = y[:, :qkw].astype(BF16)
    k_all = y[:, qkw:] * (dk ** -0.5)
    k_all_b = k_all.astype(BF16)

    gates = if_ref[...]
    logf = _log_sigmoid(gates)
    row = lax.broadcasted_iota(jnp.int32, (tc, tc), 0)
    col = lax.broadcasted_iota(jnp.int32, (tc, tc), 1)
    tri = jnp.where((row // ch == col // ch) & (col <= row), 1.0, 0.0).astype(BF16)
    hi, mid, lo = _split3(logf)
    bcum = _dot(tri, hi) + _dot(tri, mid) + _dot(tri, lo)
    lane = lax.broadcasted_iota(jnp.int32, (1, LANES), 1)
    packed_t = jnp.where(lane < heads, gates, bcum).T

    gt8 = packed_t[0:SUBLANES, :]
    bc_t = pltpu.roll(gt8, heads, 0)
    r_t = gt8 - bc_t
    pos = lax.broadcasted_iota(jnp.int32, (1, tc), 1) % ch
    nblk = tc // LANES

    def lane_blocks(a, f):
        return jnp.concatenate([f(a[:, j * LANES:(j + 1) * LANES]) for j in range(nblk)], axis=1)

    rmax_t = r_t
    for sh in (1, 2, 4, 8, 16, 32):
        back = lane_blocks(rmax_t, lambda blk, sh=sh: pltpu.roll(blk, sh, 1))
        rmax_t = jnp.maximum(rmax_t, jnp.where(pos >= sh, back, NEG_INF))

    last = [c * ch + ch - 1 for c in range(nc)]
    b_last = [bc_t[:, i:i + 1] for i in last]
    g_max = [bc_t[:, i:i + 1] + rmax_t[:, i:i + 1] for i in last]
    m_state = m_ref[:, 0:1]
    m_in, m_out = [], []
    for c in range(nc):
        m_in.append(m_state)
        m_state = jnp.maximum(b_last[c] + m_state, g_max[c])
        m_out.append(m_state)
    m_ref[...] = jnp.broadcast_to(m_state, m_ref.shape)
    decay = [jnp.exp(b_last[c] + m_in[c] - m_out[c]) for c in range(nc)]

    def per_time(cols):
        return jnp.concatenate([jnp.where(lane < ch, cols[2 * j], cols[2 * j + 1]) for j in range(nblk)], axis=1)

    m_in_t, m_out_t, b_last_t = per_time(m_in), per_time(m_out), per_time(b_last)
    w_t = jnp.exp(b_last_t + r_t - m_out_t)
    a_log_t = bc_t + m_in_t
    m_t_t = jnp.maximum(a_log_t, bc_t + rmax_t)
    inter_t = jnp.exp(a_log_t - m_t_t)
    eneg_t = jnp.exp(-m_t_t)
    sub = lax.broadcasted_iota(jnp.int32, (SUBLANES, 1), 0)
    cols_t = jnp.concatenate(
        [jnp.where(sub < heads, w_t, pltpu.roll(inter_t, heads, 0)),
         jnp.where(sub < heads, eneg_t, pltpu.roll(m_t_t, heads, 0)),
         jnp.zeros((LANES - 2 * SUBLANES, tc), F32)], axis=0)
    cols = cols_t.T

    causal = (lax.broadcasted_iota(jnp.int32, (ch, ch), 1) <= lax.broadcasted_iota(jnp.int32, (ch, ch), 0))
    ng = ng_ref[...]
    ones_v = jnp.ones((tc, LANES), BF16)
    ones_sq = jnp.ones((dv, LANES), BF16)

    def chunks(a):
        return a.reshape(nc, ch, a.shape[-1])

    for h in range(heads):
        w_col = cols[:, h:h + 1]
        inter = cols[:, heads + h:heads + h + 1]
        eneg = cols[:, 2 * heads + h:2 * heads + h + 1]
        m_t = cols[:, 3 * heads + h:3 * heads + h + 1]
        bc = bcum[:, heads + h:heads + h + 1]
        r_row = jnp.stack([r_t[h:h + 1, c * ch:(c + 1) * ch] for c in range(nc)])
        q = chunks(q_all[:, h * dk:(h + 1) * dk])
        kb = chunks(k_all_b[:, h * dk:(h + 1) * dk])
        v = chunks(jnp.concatenate([v_ref[:, h * dv:(h + 1) * dv].astype(BF16), ones_v], axis=1))
        wkb = chunks((w_col * k_all[:, h * dk:(h + 1) * dk]).astype(BF16))

        c_state = c_ref[h]
        c_in = []
        for c in range(nc):
            c_in.append(c_state.astype(BF16))
            c_state = decay[c][h:h + 1, :] * c_state + _dot_tn(wkb[c], v[c])
        c_ref[h] = c_state

        d = jnp.where(causal, chunks(bc) + r_row, NEG_INF)
        qk = jnp.stack([_dot_nt(q[c], kb[c]) for c in range(nc)]) * jnp.exp(d - chunks(m_t))
        qkb = qk.astype(BF16)
        num = chunks(inter) * jnp.stack([_dot(q[c], c_in[c]) for c in range(nc)]) \
            + jnp.stack([_dot(qkb[c], v[c]) for c in range(nc)])
        num = num.reshape(tc, dv + LANES)
        inv = 1.0 / jnp.maximum(jnp.abs(num[:, dv:]), eneg)
        halves = [num[:, i * LANES:(i + 1) * LANES] * inv for i in range(dv // LANES)]
        sq = jnp.concatenate([x * x for x in halves], axis=1).astype(BF16)
        rs = lax.rsqrt(_dot(sq, ones_sq) * (1.0 / dv) + NORM_EPS)
        for i, x in enumerate(halves):
            sl = slice(h * dv + i * LANES, h * dv + (i + 1) * LANES)
            o_ref[:, sl] = (x * rs * ng[:, sl] * jax.nn.sigmoid(mo_ref[:, sl])).astype(BF16)


def _mlstm(proj, small, conv_w, conv_b, norm_g, bsz, seq, qk_col, v_col, o_col, if_col, tc=512):
    heads, dk, dv = M_HEADS, M_QK_DIM, M_V_DIM
    qk2 = 2 * heads * dk
    mw = heads * dv
    nt = seq // tc
    n = bsz * seq

    def rows(b, t):
        return b * nt + t

    def prev_rows(b, t):
        return jnp.maximum((b * nt + t) * (tc // SUBLANES) - 1, 0)

    return pl.pallas_call(
        functools.partial(_mlstm_kernel, tc=tc),
        grid=(bsz, nt),
        in_specs=[pl.BlockSpec((tc, qk2), lambda b, t: (rows(b, t), qk_col // qk2)),
                  pl.BlockSpec((SUBLANES, qk2), lambda b, t: (prev_rows(b, t), qk_col // qk2)),
                  pl.BlockSpec((tc, mw), lambda b, t: (rows(b, t), v_col // mw)),
                  pl.BlockSpec((tc, mw), lambda b, t: (rows(b, t), o_col // mw)),
                  pl.BlockSpec((tc, LANES), lambda b, t: (rows(b, t), if_col // LANES)),
                  pl.BlockSpec((CONV_WIDTH, qk2), lambda b, t: (0, 0)),
                  pl.BlockSpec((1, qk2), lambda b, t: (0, 0)),
                  pl.BlockSpec((1, mw), lambda b, t: (0, 0))],
        out_specs=pl.BlockSpec((tc, mw), lambda b, t: (rows(b, t), 0)),
        out_shape=jax.ShapeDtypeStruct((n, mw), BF16),
        scratch_shapes=[pltpu.VMEM((SUBLANES + tc, qk2), F32),
                        pltpu.VMEM((heads, dk, dv + LANES), F32),
                        pltpu.VMEM((SUBLANES, LANES), F32)],
        compiler_params=_params(("arbitrary", "arbitrary")),
        name="mlstm",
    )(proj, proj, proj, proj, small, conv_w, conv_b.reshape(1, qk2), norm_g.reshape(1, mw))


def _merge_kernel(oa_ref, hm_ref, u_ref, h_ref, gt_ref, wa_ref, wm_ref, wga_ref, wgm_ref, wo_ref, o_ref, *, rows):
    j = pl.program_id(1)
    nj = pl.num_programs(1)
    tm = h_ref.shape[0]

    def step(first, last):
        for r in range(tm // rows):
            sl = slice(r * rows, (r + 1) * rows)
            u = u_ref[sl, :]
            y_att = _dot(oa_ref[sl, :], wa_ref[...])
            y_m = _dot(hm_ref[sl, :], wm_ref[...])
            g_att = jax.nn.sigmoid(_dot(u, wga_ref[...]))
            g_m = jax.nn.sigmoid(_dot(u, wgm_ref[...]))
            acc = _dot((g_att * y_att + g_m * y_m).astype(BF16), wo_ref[...])
            if not first:
                acc = o_ref[sl, :] + acc
            if last:
                acc = h_ref[sl, :] + gt_ref[0] * acc
            o_ref[sl, :] = acc

    pl.when(j == 0)(lambda: step(True, False))
    pl.when((j > 0) & (j < nj - 1))(lambda: step(False, False))
    pl.when(j == nj - 1)(lambda: step(False, True))


def _merge(o_att, hm, u, h, seq, mod3, mod_idx, w_att_up, w_mlstm_up, w_gates, w_out, tm=512, tn=512, rows=256):
    n, d = h.shape
    aw = o_att.shape[1]
    mw = hm.shape[1]
    nj = d // tn
    assert nj >= 2
    tiles_per_seq = seq // tm
    return pl.pallas_call(
        functools.partial(_merge_kernel, rows=rows),
        grid=(n // tm, nj),
        in_specs=[pl.BlockSpec((tm, aw), lambda i, j: (i, 0)),
                  pl.BlockSpec((tm, mw), lambda i, j: (i, 0)),
                  pl.BlockSpec((tm, d), lambda i, j: (i, 0)),
                  pl.BlockSpec((tm, d), lambda i, j: (i, 0)),
                  pl.BlockSpec((1, 1, d), lambda i, j: ((i // tiles_per_seq) * N_MOD + mod_idx, 0, 0)),
                  pl.BlockSpec((aw, tn), lambda i, j: (0, j)),
                  pl.BlockSpec((mw, tn), lambda i, j: (0, j)),
                  pl.BlockSpec((d, tn), lambda i, j: (0, j)),
                  pl.BlockSpec((d, tn), lambda i, j: (0, nj + j)),
                  pl.BlockSpec((tn, d), lambda i, j: (j, 0))],
        out_specs=pl.BlockSpec((tm, d), lambda i, j: (i, 0)),
        out_shape=jax.ShapeDtypeStruct((n, d), F32),
        compiler_params=_params(("arbitrary", "arbitrary")),
        name="merge",
    )(o_att, hm, u, h, mod3, w_att_up, w_mlstm_up, w_gates, w_gates, w_out)


def _repack_kernel(w_ref, *out_refs, plans):
    for o_ref, plan in zip(out_refs, plans):
        dst = 0
        for src, width, valid in plan:
            piece = w_ref[:, src:src + width]
            if valid < width:
                piece = jnp.where(lax.broadcasted_iota(jnp.int32, (1, width), 1) < valid, piece, 0.0)
            o_ref[:, dst:dst + width] = piece.astype(BF16)
            dst += width


def _repack(w, plans, tr=256):
    r, c = w.shape
    widths = [sum(width for _, width, _ in plan) for plan in plans]
    return pl.pallas_call(
        functools.partial(_repack_kernel, plans=plans),
        grid=(r // tr,),
        in_specs=[pl.BlockSpec((tr, c), lambda i: (i, 0))],
        out_specs=[pl.BlockSpec((tr, wd), lambda i: (i, 0)) for wd in widths],
        out_shape=[jax.ShapeDtypeStruct((r, wd), BF16) for wd in widths],
        compiler_params=_params(("arbitrary",)),
        name="repack",
    )(w)


def kernel(x, c, positions, w_ada, b_ada, norm1_g, ffn1_w_in, ffn1_w_out, norm2_g, w_in, conv_w, conv_b,
           b_igate, b_fgate, sinks, mlstm_norm_g, w_att_up, w_mlstm_up, w_out, norm3_g, ffn2_w_in,
           ffn2_w_out, final_g):
    bsz, seq, d = x.shape
    depth = w_ada.shape[0]
    n = bsz * seq
    aw = ATT_HEADS * ATT_HEAD_DIM
    kvw = ATT_KV_HEADS * ATT_HEAD_DIM
    qk2 = 2 * M_HEADS * M_QK_DIM
    mw = M_HEADS * M_V_DIM

    sizes = (aw, kvw, kvw, qk2, mw, M_HEADS, M_HEADS, mw, 2 * d)
    offs = [0]
    for s in sizes:
        offs.append(offs[-1] + s)
    (o_q, o_k, o_v, o_qk, o_mv, o_i, o_f, o_mo, o_g) = offs[:-1]
    q_col, qk_col, mv_col, mo_col = 0, aw, aw + qk2, aw + qk2 + mw
    k_col, v_col, if_col = 0, kvw, 2 * kvw

    half = ROPE_DIM // 2
    inv_freq = jnp.power(ROPE_THETA, -jnp.arange(half, dtype=F32) * 2.0 / ROPE_DIM)
    inv_freq = jnp.broadcast_to(inv_freq[:, None], (half, WINDOW))
    pos3 = positions.reshape(n // WINDOW, 1, WINDOW)

    h = x.reshape(n, d)
    for l in range(depth):
        mod3 = _ada(c, w_ada[l], b_ada[l]).reshape(bsz * N_MOD, 1, d)

        h, (ffn2_in_b, ffn2_out_b) = _ffn(
            h, seq, norm1_g[l], mod3, 0, ffn1_w_in[l].astype(BF16), ffn1_w_out[l].astype(BF16),
            casts=(ffn2_w_in[l], ffn2_w_out[l]))

        w_main, w_small, w_gates = _repack(w_in[l], (
            ((o_q, aw, aw), (o_qk, qk2, qk2), (o_mv, mw, mw), (o_mo, mw, mw)),
            ((o_k, 2 * kvw, 2 * kvw), (o_i, LANES, 2 * M_HEADS)),
            ((o_g, 2 * d, 2 * d),)))
        b_small = jnp.concatenate(
            [jnp.zeros((2 * kvw,), F32), b_igate[l], b_fgate[l], jnp.zeros((LANES - 2 * M_HEADS,), F32)])[None, :]
        proj, small, u = _proj(h, seq, norm2_g[l], mod3, 3, w_main, w_small, b_small)

        o_att = _attn(proj, small, pos3, inv_freq, sinks[l], bsz, seq, q_col, k_col, v_col)
        hm = _mlstm(proj, small, conv_w[l], conv_b[l], mlstm_norm_g[l], bsz, seq, qk_col, mv_col, mo_col, if_col)
        h = _merge(o_att, hm, u, h, seq, mod3, 5, w_att_up[l].astype(BF16), w_mlstm_up[l].astype(BF16), w_gates,
                   w_out[l].astype(BF16))

        last = l == depth - 1
        h, _ = _ffn(h, seq, norm3_g[l], mod3, 6, ffn2_in_b, ffn2_out_b, final_g=final_g if last else None)
    return h.reshape(bsz, seq, d)
```

```python
import functools

import jax
import jax.numpy as jnp
from jax import lax
from jax.experimental import pallas as pl
from jax.experimental.pallas import tpu as pltpu

F32 = jnp.float32
BF16 = jnp.bfloat16

ATT_HEADS = 16
ATT_KV_HEADS = 2
ATT_HEAD_DIM = 64
WINDOW = 128
ROPE_DIM = ATT_HEAD_DIM // 4
ROPE_THETA = 500000.0
M_HEADS = 4
M_QK_DIM = 128
M_V_DIM = 256
M_CHUNK = 64
CONV_WIDTH = 4
NORM_EPS = 1e-6
NEG_INF = -1e30
N_MOD = 9

LANES = 128
SUBLANES = 8
VMEM_LIMIT = 60 * 1024 * 1024


def _dot(a, b):
    return jnp.dot(a, b, preferred_element_type=F32)


def _dot_nt(a, b):
    return lax.dot_general(a, b, (((1,), (1,)), ((), ())), preferred_element_type=F32)


def _dot_tn(a, b):
    return lax.dot_general(a, b, (((0,), (0,)), ((), ())), preferred_element_type=F32)


def _split3(x):
    hi = x.astype(BF16)
    r1 = x - hi.astype(F32)
    mid = r1.astype(BF16)
    lo = (r1 - mid.astype(F32)).astype(BF16)
    return hi, mid, lo


def _rms(x):
    return x * lax.rsqrt(jnp.mean(x * x, axis=-1, keepdims=True) + NORM_EPS)


def _params(sem):
    return pltpu.CompilerParams(dimension_semantics=sem, vmem_limit_bytes=VMEM_LIMIT)


def _ada_kernel(c_ref, b_ref, *refs):
    w_refs, o_ref = refs[:-1], refs[-1]
    c = c_ref[...]
    hi, mid, lo = _split3(c * jax.nn.sigmoid(c))
    tn = w_refs[0].shape[1]
    for s, w_ref in enumerate(w_refs):
        w = w_ref[...].astype(BF16)
        sl = slice(s * tn, (s + 1) * tn)
        o_ref[:, sl] = _dot(hi, w) + _dot(mid, w) + _dot(lo, w) + b_ref[:, sl]


def _ada(c, w_ada, b_ada, tn=1024, streams=3):
    bsz, d = c.shape
    n = w_ada.shape[1]
    cp = jnp.zeros((SUBLANES, d), F32).at[:bsz].set(c)
    wide = tn * streams
    out = pl.pallas_call(
        _ada_kernel,
        grid=(n // wide,),
        in_specs=[pl.BlockSpec((SUBLANES, d), lambda j: (0, 0)),
                  pl.BlockSpec((1, wide), lambda j: (0, j))]
        + [pl.BlockSpec((d, tn), lambda j, s=s: (0, streams * j + s)) for s in range(streams)],
        out_specs=pl.BlockSpec((SUBLANES, wide), lambda j: (0, j)),
        out_shape=jax.ShapeDtypeStruct((SUBLANES, n), F32),
        compiler_params=_params(("arbitrary",)),
        name="ada",
    )(cp, b_ada.reshape(1, n), *([w_ada] * streams))
    return out[:bsz]


def _ffn_kernel(x_ref, g_ref, sh_ref, sc_ref, gt_ref, wa_ref, wb_ref, wo_ref, *rest, final_norm, rows, n_cast):
    rest = list(rest)
    fg_ref = rest.pop(0) if final_norm else None
    cast_in, o_ref, cast_out, u_ref = rest[:n_cast], rest[n_cast], rest[n_cast + 1:2 * n_cast + 1], rest[-1]
    j = pl.program_id(1)
    nj = pl.num_programs(1)
    tm = x_ref.shape[0]

    for src, dst in zip(cast_in, cast_out):
        dst[...] = src[...].astype(BF16)

    def step(first, last):
        chunk = rows if (first or last) else tm
        for r in range(tm // chunk):
            sl = slice(r * chunk, (r + 1) * chunk)
            if first:
                y = _rms(x_ref[sl, :]) * g_ref[...]
                u = (y * (1.0 + sc_ref[0]) + sh_ref[0]).astype(BF16)
                u_ref[sl, :] = u
            else:
                u = u_ref[sl, :]
            a = _dot(u, wa_ref[...])
            b = _dot(u, wb_ref[...])
            acc = _dot((a * jax.nn.sigmoid(a) * b).astype(BF16), wo_ref[...])
            if not first:
                acc = o_ref[sl, :] + acc
            if last:
                acc = x_ref[sl, :] + (0.5 * gt_ref[0]) * acc
                if final_norm:
                    acc = _rms(acc) * fg_ref[...]
            o_ref[sl, :] = acc

    pl.when(j == 0)(lambda: step(True, False))
    pl.when((j > 0) & (j < nj - 1))(lambda: step(False, False))
    pl.when(j == nj - 1)(lambda: step(False, True))


def _cast_spec(shape, n_i, n_j):
    r, c = shape
    sub, lanes = 2 * SUBLANES, LANES
    if r % n_i == 0 and c % n_j == 0 and (r // n_i) % sub == 0 and (c // n_j) % lanes == 0:
        return pl.BlockSpec((r // n_i, c // n_j), lambda i, j: (i, j))
    if r % n_j == 0 and c % n_i == 0 and (r // n_j) % sub == 0 and (c // n_i) % lanes == 0:
        return pl.BlockSpec((r // n_j, c // n_i), lambda i, j: (j, i))
    assert r % n_i == 0 and (r // n_i) % sub == 0, shape
    return pl.BlockSpec((r // n_i, c), lambda i, j: (i, 0))


def _ffn(h, seq, norm_g, mod3, mod_base, w_in, w_out, final_g=None, casts=(), tm=1024, tf=512, rows=256):
    n, d = h.shape
    dff = w_out.shape[0]
    nj = dff // tf
    assert nj >= 2
    tiles_per_seq = seq // tm

    def mod_spec(k):
        return pl.BlockSpec((1, 1, d), lambda i, j: ((i // tiles_per_seq) * N_MOD + mod_base + k, 0, 0))

    in_specs = [pl.BlockSpec((tm, d), lambda i, j: (i, 0)),
                pl.BlockSpec((1, d), lambda i, j: (0, 0)),
                mod_spec(0), mod_spec(1), mod_spec(2),
                pl.BlockSpec((d, tf), lambda i, j: (0, j)),
                pl.BlockSpec((d, tf), lambda i, j: (0, j + nj)),
                pl.BlockSpec((tf, d), lambda i, j: (j, 0))]
    args = [h, norm_g.reshape(1, d), mod3, mod3, mod3, w_in, w_in, w_out]
    if final_g is not None:
        in_specs.append(pl.BlockSpec((1, d), lambda i, j: (0, 0)))
        args.append(final_g.reshape(1, d))
    cast_specs = [_cast_spec(w.shape, n // tm, nj) for w in casts]
    outs = pl.pallas_call(
        functools.partial(_ffn_kernel, final_norm=final_g is not None, rows=rows, n_cast=len(casts)),
        grid=(n // tm, nj),
        in_specs=in_specs + cast_specs,
        out_specs=[pl.BlockSpec((tm, d), lambda i, j: (i, 0))] + [_cast_spec(w.shape, n // tm, nj) for w in casts],
        out_shape=[jax.ShapeDtypeStruct((n, d), F32)] + [jax.ShapeDtypeStruct(w.shape, BF16) for w in casts],
        scratch_shapes=[pltpu.VMEM((tm, d), BF16)],
        compiler_params=_params(("arbitrary", "arbitrary")),
        name="ffn_final" if final_g is not None else "ffn",
    )(*args, *casts)
    return outs[0], tuple(outs[1:])


def _proj_kernel(x_ref, g_ref, sh_ref, sc_ref, w_ref, ws_ref, bs_ref, p_ref, s_ref, u_ref, *, rows):
    j = pl.program_id(1)
    tm = x_ref.shape[0]

    @pl.when(j == 0)
    def _():
        for r in range(tm // rows):
            sl = slice(r * rows, (r + 1) * rows)
            y = _rms(x_ref[sl, :]) * g_ref[...]
            u = (y * (1.0 + sc_ref[0]) + sh_ref[0]).astype(BF16)
            u_ref[sl, :] = u
            s_ref[sl, :] = _dot(u, ws_ref[...]) + bs_ref[...]
            p_ref[sl, :] = _dot(u, w_ref[...])

    @pl.when(j > 0)
    def _():
        p_ref[...] = _dot(u_ref[...], w_ref[...])


def _proj(h, seq, norm_g, mod3, mod_base, w, w_small, b_small, tm=1024, tn=1024, rows=256):
    n, d = h.shape
    ncol = w.shape[1]
    nsmall = w_small.shape[1]
    tiles_per_seq = seq // tm

    def mod_spec(k):
        return pl.BlockSpec((1, 1, d), lambda i, j: ((i // tiles_per_seq) * N_MOD + mod_base + k, 0, 0))

    return pl.pallas_call(
        functools.partial(_proj_kernel, rows=rows),
        grid=(n // tm, ncol // tn),
        in_specs=[pl.BlockSpec((tm, d), lambda i, j: (i, 0)),
                  pl.BlockSpec((1, d), lambda i, j: (0, 0)),
                  mod_spec(0), mod_spec(1),
                  pl.BlockSpec((d, tn), lambda i, j: (0, j)),
                  pl.BlockSpec((d, nsmall), lambda i, j: (0, 0)),
                  pl.BlockSpec((1, nsmall), lambda i, j: (0, 0))],
        out_specs=[pl.BlockSpec((tm, tn), lambda i, j: (i, j)),
                   pl.BlockSpec((tm, nsmall), lambda i, j: (i, 0)),
                   pl.BlockSpec((tm, d), lambda i, j: (i, 0))],
        out_shape=[jax.ShapeDtypeStruct((n, ncol), F32),
                   jax.ShapeDtypeStruct((n, nsmall), F32),
                   jax.ShapeDtypeStruct((n, d), BF16)],
        compiler_params=_params(("arbitrary", "arbitrary")),
        name="proj",
    )(h, norm_g.reshape(1, d), mod3, mod3, w, w_small, b_small)


def _attn_kernel(q_ref, kc_ref, vc_ref, kp_ref, vp_ref, posc_ref, posp_ref, invf_ref, sink_ref, o_ref, *, nsub):
    blk = WINDOW
    lane = lax.broadcasted_iota(jnp.int32, (1, LANES), 1)
    hl = lane % ATT_HEAD_DIM
    half = ROPE_DIM // 2
    lo_half = lane < ATT_HEAD_DIM

    def rope_tables(pos):
        pos = pos.astype(F32)
        ang = invf_ref[...] * pos
        reps = LANES // half
        ct = jnp.tile(jnp.cos(ang), (reps, 1)).T
        st = jnp.tile(jnp.sin(ang), (reps, 1)).T
        c = jnp.where(hl < ROPE_DIM, ct, 1.0)
        s = jnp.where(hl < ROPE_DIM, st, 0.0)
        return c, s

    def swap(x):
        return jnp.where(hl < half, pltpu.roll(x, LANES - half, 1),
                         jnp.where(hl < ROPE_DIM, -pltpu.roll(x, half, 1), 0.0))

    def rope(x, c, s):
        return x * c - swap(x) * s

    tabs = [rope_tables(posp_ref[0])] + [rope_tables(posc_ref[t]) for t in range(nsub)]
    k = jnp.concatenate([rope(kp_ref[...], *tabs[0])]
                        + [rope(kc_ref[t * blk:(t + 1) * blk, :], *tabs[t + 1]) for t in range(nsub)], axis=0)
    v = jnp.concatenate([vp_ref[...], vc_ref[...]], axis=0)
    ks = swap(k)
    k_sw, ks_sw = pltpu.roll(k, ATT_HEAD_DIM, 1), pltpu.roll(ks, ATT_HEAD_DIM, 1)
    v_sw = pltpu.roll(v, ATT_HEAD_DIM, 1)

    qi = lax.broadcasted_iota(jnp.int32, (blk, 2 * blk), 0) + blk
    kj = lax.broadcasted_iota(jnp.int32, (blk, 2 * blk), 1)
    band = (kj <= qi) & (kj > qi - WINDOW)
    k_min = jnp.where(pl.program_id(1) == 0, blk, 0)
    band_first = band & (kj >= k_min)

    scale = ATT_HEAD_DIM ** -0.5
    pairs_per_group = ATT_HEADS // ATT_KV_HEADS // 2
    ones = jnp.ones((2 * blk, LANES), F32)
    one_rhs = jnp.concatenate([jnp.where(lo_half, ones, 0.0), jnp.where(lo_half, 0.0, ones)], axis=0)
    for g in range(ATT_KV_HEADS):
        k_g, k_o = (k, k_sw) if g == 0 else (k_sw, k)
        ks_g, ks_o = (ks, ks_sw) if g == 0 else (ks_sw, ks)
        v_g, v_o = (v, v_sw) if g == 0 else (v_sw, v)
        k_lo = jnp.concatenate([jnp.where(lo_half, k_g, 0.0), jnp.where(lo_half, ks_g, 0.0)], axis=1).astype(BF16)
        k_hi = jnp.concatenate([jnp.where(lo_half, 0.0, k_o), jnp.where(lo_half, 0.0, ks_o)], axis=1).astype(BF16)
        v_lo, v_hi = jnp.where(lo_half, v_g, 0.0), jnp.where(lo_half, 0.0, v_o)
        for t in range(nsub):
            rows = slice(t * blk, (t + 1) * blk)
            keys = slice(t * blk, (t + 2) * blk)
            mask = band_first if t == 0 else band
            k_rhs = jnp.concatenate([k_lo[keys], k_hi[keys]], axis=0)
            c_q, s_q = tabs[t + 1][0] * scale, tabs[t + 1][1] * scale
            v_rhs = jnp.concatenate([jnp.concatenate([v_lo[keys], v_hi[keys]], axis=0), one_rhs],
                                    axis=1).astype(BF16)
            q_rows = []
            for pp in range(pairs_per_group):
                p = g * pairs_per_group + pp
                q_p = q_ref[rows, p * LANES:(p + 1) * LANES]
                q_rows.append(jnp.concatenate([q_p * c_q, q_p * s_q], axis=1).astype(BF16))
            s_all = _dot_nt(jnp.concatenate(q_rows, axis=0), k_rhs)
            p_rows, sink_rows = [], []
            for pp in range(pairs_per_group):
                p = g * pairs_per_group + pp
                halves, m_pair = [], []
                for e in range(2):
                    s = s_all[pp * blk:(pp + 1) * blk, e * 2 * blk:(e + 1) * 2 * blk]
                    s = jnp.where(mask, s, NEG_INF)
                    m = jnp.maximum(jnp.max(s, axis=-1, keepdims=True), sink_ref[2 * p + e])
                    halves.append(jnp.exp(s - m).astype(BF16))
                    m_pair.append(m)
                p_rows.append(jnp.concatenate(halves, axis=1))
                sink_rows.append(
                    jnp.exp(jnp.where(lo_half, sink_ref[2 * p] - m_pair[0], sink_ref[2 * p + 1] - m_pair[1])))
            o_all = _dot(jnp.concatenate(p_rows, axis=0), v_rhs)
            denom = o_all[:, LANES:] + jnp.concatenate(sink_rows, axis=0)
            o_all = o_all[:, :LANES] * (1.0 / denom)
            for pp in range(pairs_per_group):
                p = g * pairs_per_group + pp
                o_ref[rows, p * LANES:(p + 1) * LANES] = o_all[pp * blk:(pp + 1) * blk].astype(BF16)


def _attn(q_src, kv_src, pos3, inv_freq, sinks, bsz, seq, q_col, k_col, v_col, nsub=1):
    blk = WINDOW
    aw = ATT_HEADS * ATT_HEAD_DIM
    ns = seq // (nsub * blk)
    n = bsz * seq
    half = ROPE_DIM // 2

    def cur(b, i):
        return b * ns + i

    def prv(b, i):
        return b * ns * nsub + jnp.maximum(i * nsub - 1, 0)

    return pl.pallas_call(
        functools.partial(_attn_kernel, nsub=nsub),
        grid=(bsz, ns),
        in_specs=[pl.BlockSpec((nsub * blk, aw), lambda b, i: (cur(b, i), q_col // aw)),
                  pl.BlockSpec((nsub * blk, LANES), lambda b, i: (cur(b, i), k_col // LANES)),
                  pl.BlockSpec((nsub * blk, LANES), lambda b, i: (cur(b, i), v_col // LANES)),
                  pl.BlockSpec((blk, LANES), lambda b, i: (prv(b, i), k_col // LANES)),
                  pl.BlockSpec((blk, LANES), lambda b, i: (prv(b, i), v_col // LANES)),
                  pl.BlockSpec((nsub, 1, blk), lambda b, i: (cur(b, i), 0, 0)),
                  pl.BlockSpec((1, 1, blk), lambda b, i: (prv(b, i), 0, 0)),
                  pl.BlockSpec((half, blk), lambda b, i: (0, 0)),
                  pl.BlockSpec(memory_space=pltpu.SMEM)],
        out_specs=pl.BlockSpec((nsub * blk, aw), lambda b, i: (cur(b, i), 0)),
        out_shape=jax.ShapeDtypeStruct((n, aw), BF16),
        compiler_params=_params(("arbitrary", "arbitrary")),
        name="attn",
    )(q_src, kv_src, kv_src, kv_src, kv_src, pos3, pos3, inv_freq, sinks)


def _log_sigmoid(x):
    return jnp.minimum(x, 0.0) - jnp.log1p(jnp.exp(-jnp.abs(x)))


def _mlstm_kernel(xqk_ref, xprev_ref, v_ref, mo_ref, if_ref, ifn_ref, cw_ref, cb_ref, ng_ref, o_ref,
                  xbuf_ref, c_ref, m_ref, cols_ref, rt_ref, dec_ref, *, tc):
    heads, dk, dv, ch = M_HEADS, M_QK_DIM, M_V_DIM, M_CHUNK
    qkw = heads * dk
    nc = tc // ch
    assert ch * 2 == LANES and nc % 2 == 0 and 2 * heads <= SUBLANES
    t_idx = pl.program_id(1)

    lane = lax.broadcasted_iota(jnp.int32, (1, LANES), 1)
    nblk = tc // LANES
    slot = t_idx % 2

    def gate_prep(g_ref, dst):
        gates = g_ref[...]
        logf = _log_sigmoid(gates)
        row = lax.broadcasted_iota(jnp.int32, (tc, tc), 0)
        col = lax.broadcasted_iota(jnp.int32, (tc, tc), 1)
        tri = jnp.where((row // ch == col // ch) & (col <= row), 1.0, 0.0).astype(BF16)
        hi, mid, lo = _split3(logf)
        bcum = _dot(tri, hi) + _dot(tri, mid) + _dot(tri, lo)
        packed_t = jnp.where(lane < heads, gates, bcum).T

        gt8 = packed_t[0:SUBLANES, :]
        bc_t = pltpu.roll(gt8, heads, 0)
        r_t = gt8 - bc_t
        pos = lax.broadcasted_iota(jnp.int32, (1, tc), 1) % ch

        def lane_blocks(a, f):
            return jnp.concatenate([f(a[:, j * LANES:(j + 1) * LANES]) for j in range(nblk)], axis=1)

        rmax_t = r_t
        for sh in (1, 2, 4, 8, 16, 32):
            back = lane_blocks(rmax_t, lambda blk, sh=sh: pltpu.roll(blk, sh, 1))
            rmax_t = jnp.maximum(rmax_t, jnp.where(pos >= sh, back, NEG_INF))

        last = [c * ch + ch - 1 for c in range(nc)]
        b_last = [bc_t[:, i:i + 1] for i in last]
        g_max = [bc_t[:, i:i + 1] + rmax_t[:, i:i + 1] for i in last]
        m_state = m_ref[:, 0:1]
        m_in, m_out = [], []
        for c in range(nc):
            m_in.append(m_state)
            m_state = jnp.maximum(b_last[c] + m_state, g_max[c])
            m_out.append(m_state)
        m_ref[...] = jnp.broadcast_to(m_state, m_ref.shape)
        decay = [jnp.exp(b_last[c] + m_in[c] - m_out[c]) for c in range(nc)]

        def per_time(cols):
            return jnp.concatenate(
                [jnp.where(lane < ch, cols[2 * j], cols[2 * j + 1]) for j in range(nblk)], axis=1)

        m_in_t, m_out_t, b_last_t = per_time(m_in), per_time(m_out), per_time(b_last)
        w_t = jnp.exp(b_last_t + r_t - m_out_t)
        a_log_t = bc_t + m_in_t
        m_t_t = jnp.maximum(a_log_t, bc_t + rmax_t)
        inter_t = jnp.exp(a_log_t - m_t_t)
        eneg_t = jnp.exp(-m_t_t)
        sub = lax.broadcasted_iota(jnp.int32, (SUBLANES, 1), 0)
        cols_t = jnp.concatenate(
            [jnp.where(sub < heads, w_t, pltpu.roll(inter_t, heads, 0)),
             jnp.where(sub < heads, eneg_t, pltpu.roll(m_t_t, heads, 0)),
             bc_t,
             jnp.zeros((LANES - 3 * SUBLANES, tc), F32)], axis=0)
        cols_ref[dst] = cols_t.T
        rt_ref[dst] = r_t
        dec_ref[dst] = per_time(decay)

    @pl.when(t_idx == 0)
    def _():
        c_ref[...] = jnp.zeros_like(c_ref)
        m_ref[...] = jnp.full_like(m_ref, NEG_INF)
        gate_prep(if_ref, 0)

    cols = cols_ref[slot]
    r_t = rt_ref[slot]
    dec_t = dec_ref[slot]
    gate_prep(ifn_ref, 1 - slot)

    pad = SUBLANES
    xbuf_ref[0:pad, :] = jnp.where(t_idx > 0, xprev_ref[...], 0.0)
    xbuf_ref[pad:pad + tc, :] = xqk_ref[...]
    y = cb_ref[...]
    for j in range(CONV_WIDTH):
        off = pad - (CONV_WIDTH - 1) + j
        y = y + cw_ref[j:j + 1, :] * xbuf_ref[off:off + tc, :]
    y = y * jax.nn.sigmoid(y)
    q_all = y[:, :qkw].astype(BF16)
    k_all = y[:, qkw:] * (dk ** -0.5)
    k_all_b = k_all.astype(BF16)

    causal = (lax.broadcasted_iota(jnp.int32, (ch, ch), 1) <= lax.broadcasted_iota(jnp.int32, (ch, ch), 0))
    ng = ng_ref[...]
    ones_v = jnp.ones((tc, LANES), BF16)
    ones_sq = jnp.ones((dv, LANES), BF16)

    def chunks(a):
        return a.reshape(nc, ch, a.shape[-1])

    for h in range(heads):
        w_col = cols[:, h:h + 1]
        inter = cols[:, heads + h:heads + h + 1]
        eneg = cols[:, 2 * heads + h:2 * heads + h + 1]
        m_t = cols[:, 3 * heads + h:3 * heads + h + 1]
        bc = cols[:, 4 * heads + h:4 * heads + h + 1]
        r_row = jnp.stack([r_t[h:h + 1, c * ch:(c + 1) * ch] for c in range(nc)])
        q = chunks(q_all[:, h * dk:(h + 1) * dk])
        kb = chunks(k_all_b[:, h * dk:(h + 1) * dk])
        v = chunks(jnp.concatenate([v_ref[:, h * dv:(h + 1) * dv].astype(BF16), ones_v], axis=1))
        wkb = chunks((w_col * k_all[:, h * dk:(h + 1) * dk]).astype(BF16))

        c_state = c_ref[h]
        c_in = []
        for c in range(nc):
            c_in.append(c_state.astype(BF16))
            c_state = dec_t[h:h + 1, c * ch:c * ch + 1] * c_state + _dot_tn(wkb[c], v[c])
        c_ref[h] = c_state

        d = jnp.where(causal, chunks(bc) + r_row, NEG_INF)
        qk = jnp.stack([_dot_nt(q[c], kb[c]) for c in range(nc)]) * jnp.exp(d - chunks(m_t))
        qkb = qk.astype(BF16)
        num = chunks(inter) * jnp.stack([_dot(q[c], c_in[c]) for c in range(nc)]) \
            + jnp.stack([_dot(qkb[c], v[c]) for c in range(nc)])
        num = num.reshape(tc, dv + LANES)
        inv = 1.0 / jnp.maximum(jnp.abs(num[:, dv:]), eneg)
        halves = [num[:, i * LANES:(i + 1) * LANES] * inv for i in range(dv // LANES)]
        sq = jnp.concatenate([x * x for x in halves], axis=1).astype(BF16)
        rs = lax.rsqrt(_dot(sq, ones_sq) * (1.0 / dv) + NORM_EPS)
        for i, x in enumerate(halves):
            sl = slice(h * dv + i * LANES, h * dv + (i + 1) * LANES)
            o_ref[:, sl] = (x * rs * ng[:, sl] * jax.nn.sigmoid(mo_ref[:, sl])).astype(BF16)


def _mlstm(proj, small, conv_w, conv_b, norm_g, bsz, seq, qk_col, v_col, o_col, if_col, tc=512):
    heads, dk, dv = M_HEADS, M_QK_DIM, M_V_DIM
    qk2 = 2 * heads * dk
    mw = heads * dv
    nt = seq // tc
    n = bsz * seq

    def rows(b, t):
        return b * nt + t

    def prev_rows(b, t):
        return jnp.maximum((b * nt + t) * (tc // SUBLANES) - 1, 0)

    return pl.pallas_call(
        functools.partial(_mlstm_kernel, tc=tc),
        grid=(bsz, nt),
        in_specs=[pl.BlockSpec((tc, qk2), lambda b, t: (rows(b, t), qk_col // qk2)),
                  pl.BlockSpec((SUBLANES, qk2), lambda b, t: (prev_rows(b, t), qk_col // qk2)),
                  pl.BlockSpec((tc, mw), lambda b, t: (rows(b, t), v_col // mw)),
                  pl.BlockSpec((tc, mw), lambda b, t: (rows(b, t), o_col // mw)),
                  pl.BlockSpec((tc, LANES), lambda b, t: (rows(b, t), if_col // LANES)),
                  pl.BlockSpec((tc, LANES), lambda b, t: (rows(b, jnp.minimum(t + 1, nt - 1)), if_col // LANES)),
                  pl.BlockSpec((CONV_WIDTH, qk2), lambda b, t: (0, 0)),
                  pl.BlockSpec((1, qk2), lambda b, t: (0, 0)),
                  pl.BlockSpec((1, mw), lambda b, t: (0, 0))],
        out_specs=pl.BlockSpec((tc, mw), lambda b, t: (rows(b, t), 0)),
        out_shape=jax.ShapeDtypeStruct((n, mw), BF16),
        scratch_shapes=[pltpu.VMEM((SUBLANES + tc, qk2), F32),
                        pltpu.VMEM((heads, dk, dv + LANES), F32),
                        pltpu.VMEM((SUBLANES, LANES), F32),
                        pltpu.VMEM((2, tc, LANES), F32),
                        pltpu.VMEM((2, SUBLANES, tc), F32),
                        pltpu.VMEM((2, SUBLANES, tc), F32)],
        compiler_params=_params(("arbitrary", "arbitrary")),
        name="mlstm",
    )(proj, proj, proj, proj, small, small, conv_w, conv_b.reshape(1, qk2), norm_g.reshape(1, mw))


def _merge_kernel(oa_ref, hm_ref, u_ref, h_ref, gt_ref, wa_ref, wm_ref, wga_ref, wgm_ref, wo_ref, o_ref, *, rows):
    j = pl.program_id(1)
    nj = pl.num_programs(1)
    tm = h_ref.shape[0]

    def step(first, last):
        for r in range(tm // rows):
            sl = slice(r * rows, (r + 1) * rows)
            u = u_ref[sl, :]
            y_att = _dot(oa_ref[sl, :], wa_ref[...])
            y_m = _dot(hm_ref[sl, :], wm_ref[...])
            g_att = jax.nn.sigmoid(_dot(u, wga_ref[...]))
            g_m = jax.nn.sigmoid(_dot(u, wgm_ref[...]))
            acc = _dot((g_att * y_att + g_m * y_m).astype(BF16), wo_ref[...])
            if not first:
                acc = o_ref[sl, :] + acc
            if last:
                acc = h_ref[sl, :] + gt_ref[0] * acc
            o_ref[sl, :] = acc

    pl.when(j == 0)(lambda: step(True, False))
    pl.when((j > 0) & (j < nj - 1))(lambda: step(False, False))
    pl.when(j == nj - 1)(lambda: step(False, True))


def _merge(o_att, hm, u, h, seq, mod3, mod_idx, w_att_up, w_mlstm_up, w_gates, w_out, tm=512, tn=512, rows=256):
    n, d = h.shape
    aw = o_att.shape[1]
    mw = hm.shape[1]
    nj = d // tn
    assert nj >= 2
    tiles_per_seq = seq // tm
    return pl.pallas_call(
        functools.partial(_merge_kernel, rows=rows),
        grid=(n // tm, nj),
        in_specs=[pl.BlockSpec((tm, aw), lambda i, j: (i, 0)),
                  pl.BlockSpec((tm, mw), lambda i, j: (i, 0)),
                  pl.BlockSpec((tm, d), lambda i, j: (i, 0)),
                  pl.BlockSpec((tm, d), lambda i, j: (i, 0)),
                  pl.BlockSpec((1, 1, d), lambda i, j: ((i // tiles_per_seq) * N_MOD + mod_idx, 0, 0)),
                  pl.BlockSpec((aw, tn), lambda i, j: (0, j)),
                  pl.BlockSpec((mw, tn), lambda i, j: (0, j)),
                  pl.BlockSpec((d, tn), lambda i, j: (0, j)),
                  pl.BlockSpec((d, tn), lambda i, j: (0, nj + j)),
                  pl.BlockSpec((tn, d), lambda i, j: (j, 0))],
        out_specs=pl.BlockSpec((tm, d), lambda i, j: (i, 0)),
        out_shape=jax.ShapeDtypeStruct((n, d), F32),
        compiler_params=_params(("arbitrary", "arbitrary")),
        name="merge",
    )(o_att, hm, u, h, mod3, w_att_up, w_mlstm_up, w_gates, w_gates, w_out)


def _repack_kernel(w_ref, *out_refs, plans):
    for o_ref, plan in zip(out_refs, plans):
        dst = 0
        for src, width, valid in plan:
            piece = w_ref[:, src:src + width]
            if valid < width:
                piece = jnp.where(lax.broadcasted_iota(jnp.int32, (1, width), 1) < valid, piece, 0.0)
            o_ref[:, dst:dst + width] = piece.astype(BF16)
            dst += width


def _repack(w, plans, tr=256):
    r, c = w.shape
    widths = [sum(width for _, width, _ in plan) for plan in plans]
    return pl.pallas_call(
        functools.partial(_repack_kernel, plans=plans),
        grid=(r // tr,),
        in_specs=[pl.BlockSpec((tr, c), lambda i: (i, 0))],
        out_specs=[pl.BlockSpec((tr, wd), lambda i: (i, 0)) for wd in widths],
        out_shape=[jax.ShapeDtypeStruct((r, wd), BF16) for wd in widths],
        compiler_params=_params(("arbitrary",)),
        name="repack",
    )(w)


def kernel(x, c, positions, w_ada, b_ada, norm1_g, ffn1_w_in, ffn1_w_out, norm2_g, w_in, conv_w, conv_b,
           b_igate, b_fgate, sinks, mlstm_norm_g, w_att_up, w_mlstm_up, w_out, norm3_g, ffn2_w_in,
           ffn2_w_out, final_g):
    bsz, seq, d = x.shape
    depth = w_ada.shape[0]
    n = bsz * seq
    aw = ATT_HEADS * ATT_HEAD_DIM
    kvw = ATT_KV_HEADS * ATT_HEAD_DIM
    qk2 = 2 * M_HEADS * M_QK_DIM
    mw = M_HEADS * M_V_DIM

    sizes = (aw, kvw, kvw, qk2, mw, M_HEADS, M_HEADS, mw, 2 * d)
    offs = [0]
    for s in sizes:
        offs.append(offs[-1] + s)
    (o_q, o_k, o_v, o_qk, o_mv, o_i, o_f, o_mo, o_g) = offs[:-1]
    q_col, qk_col, mv_col, mo_col = 0, aw, aw + qk2, aw + qk2 + mw
    k_col, v_col, if_col = 0, kvw, 2 * kvw

    half = ROPE_DIM // 2
    inv_freq = jnp.power(ROPE_THETA, -jnp.arange(half, dtype=F32) * 2.0 / ROPE_DIM)
    inv_freq = jnp.broadcast_to(inv_freq[:, None], (half, WINDOW))
    pos3 = positions.reshape(n // WINDOW, 1, WINDOW)

    h = x.reshape(n, d)
    for l in range(depth):
        mod3 = _ada(c, w_ada[l], b_ada[l]).reshape(bsz * N_MOD, 1, d)

        h, (ffn2_in_b, ffn2_out_b) = _ffn(
            h, seq, norm1_g[l], mod3, 0, ffn1_w_in[l].astype(BF16), ffn1_w_out[l].astype(BF16),
            casts=(ffn2_w_in[l], ffn2_w_out[l]))

        w_main, w_small, w_gates = _repack(w_in[l], (
            ((o_q, aw, aw), (o_qk, qk2, qk2), (o_mv, mw, mw), (o_mo, mw, mw)),
            ((o_k, 2 * kvw, 2 * kvw), (o_i, LANES, 2 * M_HEADS)),
            ((o_g, 2 * d, 2 * d),)))
        b_small = jnp.concatenate(
            [jnp.zeros((2 * kvw,), F32), b_igate[l], b_fgate[l], jnp.zeros((LANES - 2 * M_HEADS,), F32)])[None, :]
        proj, small, u = _proj(h, seq, norm2_g[l], mod3, 3, w_main, w_small, b_small)

        o_att = _attn(proj, small, pos3, inv_freq, sinks[l], bsz, seq, q_col, k_col, v_col)
        hm = _mlstm(proj, small, conv_w[l], conv_b[l], mlstm_norm_g[l], bsz, seq, qk_col, mv_col, mo_col, if_col)
        h = _merge(o_att, hm, u, h, seq, mod3, 5, w_att_up[l].astype(BF16), w_mlstm_up[l].astype(BF16), w_gates,
                   w_out[l].astype(BF16))

        last = l == depth - 1
        h, _ = _ffn(h, seq, norm3_g[l], mod3, 6, ffn2_in_b, ffn2_out_b, final_g=final_g if last else None)
    return h.reshape(bsz, seq, d)
```

```python
import functools

import jax
import jax.numpy as jnp
from jax import lax
from jax.experimental import pallas as pl
from jax.experimental.pallas import tpu as pltpu

F32 = jnp.float32
BF16 = jnp.bfloat16

ATT_HEADS = 16
ATT_KV_HEADS = 2
ATT_HEAD_DIM = 64
WINDOW = 128
ROPE_DIM = ATT_HEAD_DIM // 4
ROPE_THETA = 500000.0
M_HEADS = 4
M_QK_DIM = 128
M_V_DIM = 256
M_CHUNK = 64
CONV_WIDTH = 4
NORM_EPS = 1e-6
NEG_INF = -1e30
N_MOD = 9

LANES = 128
SUBLANES = 8
VMEM_LIMIT = 60 * 1024 * 1024


def _dot(a, b):
    return jnp.dot(a, b, preferred_element_type=F32)


def _dot_nt(a, b):
    return lax.dot_general(a, b, (((1,), (1,)), ((), ())), preferred_element_type=F32)


def _dot_tn(a, b):
    return lax.dot_general(a, b, (((0,), (0,)), ((), ())), preferred_element_type=F32)


def _split3(x):
    hi = x.astype(BF16)
    r1 = x - hi.astype(F32)
    mid = r1.astype(BF16)
    lo = (r1 - mid.astype(F32)).astype(BF16)
    return hi, mid, lo


def _rms(x):
    return x * lax.rsqrt(jnp.mean(x * x, axis=-1, keepdims=True) + NORM_EPS)


def _params(sem):
    return pltpu.CompilerParams(dimension_semantics=sem, vmem_limit_bytes=VMEM_LIMIT)


def _ada_kernel(c_ref, b_ref, w_ref, o_ref):
    @pl.when(pl.program_id(0) == 0)
    def _():
        o_ref[...] = jnp.broadcast_to(b_ref[...], o_ref.shape)

    c = c_ref[...]
    hi, mid, lo = _split3(c * jax.nn.sigmoid(c))
    w = w_ref[...].astype(BF16)
    o_ref[...] += _dot(hi, w) + _dot(mid, w) + _dot(lo, w)


def _ada(c, w_ada, b_ada, tk=256):
    bsz, d = c.shape
    n = w_ada.shape[1]
    cp = jnp.zeros((SUBLANES, d), F32).at[:bsz].set(c)
    out = pl.pallas_call(
        _ada_kernel,
        grid=(d // tk,),
        in_specs=[pl.BlockSpec((SUBLANES, tk), lambda k: (0, k)),
                  pl.BlockSpec((1, n), lambda k: (0, 0)),
                  pl.BlockSpec((tk, n), lambda k: (k, 0))],
        out_specs=pl.BlockSpec((SUBLANES, n), lambda k: (0, 0)),
        out_shape=jax.ShapeDtypeStruct((SUBLANES, n), F32),
        compiler_params=_params(("arbitrary",)),
        name="ada",
    )(cp, b_ada.reshape(1, n), w_ada)
    return out[:bsz]


def _ffn_kernel(x_ref, g_ref, sh_ref, sc_ref, gt_ref, wa_ref, wb_ref, wo_ref, *rest, final_norm, rows, n_cast):
    rest = list(rest)
    fg_ref = rest.pop(0) if final_norm else None
    cast_in, o_ref, cast_out, u_ref = rest[:n_cast], rest[n_cast], rest[n_cast + 1:2 * n_cast + 1], rest[-1]
    j = pl.program_id(1)
    nj = pl.num_programs(1)
    tm = x_ref.shape[0]

    for src, dst in zip(cast_in, cast_out):
        dst[...] = src[...].astype(BF16)

    def step(first, last):
        chunk = rows if (first or last) else tm
        for r in range(tm // chunk):
            sl = slice(r * chunk, (r + 1) * chunk)
            if first:
                y = _rms(x_ref[sl, :]) * g_ref[...]
                u = (y * (1.0 + sc_ref[0]) + sh_ref[0]).astype(BF16)
                u_ref[sl, :] = u
            else:
                u = u_ref[sl, :]
            a = _dot(u, wa_ref[...])
            b = _dot(u, wb_ref[...])
            acc = _dot((a * jax.nn.sigmoid(a) * b).astype(BF16), wo_ref[...])
            if not first:
                acc = o_ref[sl, :] + acc
            if last:
                acc = x_ref[sl, :] + (0.5 * gt_ref[0]) * acc
                if final_norm:
                    acc = _rms(acc) * fg_ref[...]
            o_ref[sl, :] = acc

    pl.when(j == 0)(lambda: step(True, False))
    pl.when((j > 0) & (j < nj - 1))(lambda: step(False, False))
    pl.when(j == nj - 1)(lambda: step(False, True))


def _cast_spec(shape, n_i, n_j):
    r, c = shape
    sub, lanes = 2 * SUBLANES, LANES
    if r % n_i == 0 and c % n_j == 0 and (r // n_i) % sub == 0 and (c // n_j) % lanes == 0:
        return pl.BlockSpec((r // n_i, c // n_j), lambda i, j: (i, j))
    if r % n_j == 0 and c % n_i == 0 and (r // n_j) % sub == 0 and (c // n_i) % lanes == 0:
        return pl.BlockSpec((r // n_j, c // n_i), lambda i, j: (j, i))
    assert r % n_i == 0 and (r // n_i) % sub == 0, shape
    return pl.BlockSpec((r // n_i, c), lambda i, j: (i, 0))


def _ffn(h, seq, norm_g, mod3, mod_base, w_in, w_out, final_g=None, casts=(), tm=1024, tf=512, rows=256):
    n, d = h.shape
    dff = w_out.shape[0]
    nj = dff // tf
    assert nj >= 2
    tiles_per_seq = seq // tm

    def mod_spec(k):
        return pl.BlockSpec((1, 1, d), lambda i, j: ((i // tiles_per_seq) * N_MOD + mod_base + k, 0, 0))

    in_specs = [pl.BlockSpec((tm, d), lambda i, j: (i, 0)),
                pl.BlockSpec((1, d), lambda i, j: (0, 0)),
                mod_spec(0), mod_spec(1), mod_spec(2),
                pl.BlockSpec((d, tf), lambda i, j: (0, j)),
                pl.BlockSpec((d, tf), lambda i, j: (0, j + nj)),
                pl.BlockSpec((tf, d), lambda i, j: (j, 0))]
    args = [h, norm_g.reshape(1, d), mod3, mod3, mod3, w_in, w_in, w_out]
    if final_g is not None:
        in_specs.append(pl.BlockSpec((1, d), lambda i, j: (0, 0)))
        args.append(final_g.reshape(1, d))
    cast_specs = [_cast_spec(w.shape, n // tm, nj) for w in casts]
    outs = pl.pallas_call(
        functools.partial(_ffn_kernel, final_norm=final_g is not None, rows=rows, n_cast=len(casts)),
        grid=(n // tm, nj),
        in_specs=in_specs + cast_specs,
        out_specs=[pl.BlockSpec((tm, d), lambda i, j: (i, 0))] + [_cast_spec(w.shape, n // tm, nj) for w in casts],
        out_shape=[jax.ShapeDtypeStruct((n, d), F32)] + [jax.ShapeDtypeStruct(w.shape, BF16) for w in casts],
        scratch_shapes=[pltpu.VMEM((tm, d), BF16)],
        compiler_params=_params(("arbitrary", "arbitrary")),
        name="ffn_final" if final_g is not None else "ffn",
    )(*args, *casts)
    return outs[0], tuple(outs[1:])


def _proj_kernel(x_ref, g_ref, sh_ref, sc_ref, w_ref, ws_ref, bs_ref, p_ref, s_ref, u_ref, *, rows):
    j = pl.program_id(1)
    tm = x_ref.shape[0]

    @pl.when(j == 0)
    def _():
        for r in range(tm // rows):
            sl = slice(r * rows, (r + 1) * rows)
            y = _rms(x_ref[sl, :]) * g_ref[...]
            u = (y * (1.0 + sc_ref[0]) + sh_ref[0]).astype(BF16)
            u_ref[sl, :] = u
            s_ref[sl, :] = _dot(u, ws_ref[...]) + bs_ref[...]
            p_ref[sl, :] = _dot(u, w_ref[...])

    @pl.when(j > 0)
    def _():
        p_ref[...] = _dot(u_ref[...], w_ref[...])


def _proj(h, seq, norm_g, mod3, mod_base, w_all, ncol, small_col, b_small, tm=1024, tn=1024, rows=256):
    n, d = h.shape
    nsmall = b_small.shape[1]
    assert small_col % nsmall == 0 and ncol % tn == 0
    tiles_per_seq = seq // tm

    def mod_spec(k):
        return pl.BlockSpec((1, 1, d), lambda i, j: ((i // tiles_per_seq) * N_MOD + mod_base + k, 0, 0))

    return pl.pallas_call(
        functools.partial(_proj_kernel, rows=rows),
        grid=(n // tm, ncol // tn),
        in_specs=[pl.BlockSpec((tm, d), lambda i, j: (i, 0)),
                  pl.BlockSpec((1, d), lambda i, j: (0, 0)),
                  mod_spec(0), mod_spec(1),
                  pl.BlockSpec((d, tn), lambda i, j: (0, j)),
                  pl.BlockSpec((d, nsmall), lambda i, j: (0, small_col // nsmall)),
                  pl.BlockSpec((1, nsmall), lambda i, j: (0, 0))],
        out_specs=[pl.BlockSpec((tm, tn), lambda i, j: (i, j)),
                   pl.BlockSpec((tm, nsmall), lambda i, j: (i, 0)),
                   pl.BlockSpec((tm, d), lambda i, j: (i, 0))],
        out_shape=[jax.ShapeDtypeStruct((n, ncol), F32),
                   jax.ShapeDtypeStruct((n, nsmall), F32),
                   jax.ShapeDtypeStruct((n, d), BF16)],
        compiler_params=_params(("arbitrary", "arbitrary")),
        name="proj",
    )(h, norm_g.reshape(1, d), mod3, mod3, w_all, w_all, b_small)


def _attn_kernel(q_ref, kc_ref, vc_ref, kp_ref, vp_ref, posc_ref, posp_ref, invf_ref, sink_ref, o_ref, *, nsub):
    blk = WINDOW
    lane = lax.broadcasted_iota(jnp.int32, (1, LANES), 1)
    hl = lane % ATT_HEAD_DIM
    half = ROPE_DIM // 2
    lo_half = lane < ATT_HEAD_DIM

    def rope_tables(pos):
        pos = pos.astype(F32)
        ang = invf_ref[...] * pos
        reps = LANES // half
        ct = jnp.tile(jnp.cos(ang), (reps, 1)).T
        st = jnp.tile(jnp.sin(ang), (reps, 1)).T
        c = jnp.where(hl < ROPE_DIM, ct, 1.0)
        s = jnp.where(hl < ROPE_DIM, st, 0.0)
        return c, s

    def swap(x):
        return jnp.where(hl < half, pltpu.roll(x, LANES - half, 1),
                         jnp.where(hl < ROPE_DIM, -pltpu.roll(x, half, 1), 0.0))

    def rope(x, c, s):
        return x * c - swap(x) * s

    tabs = [rope_tables(posp_ref[0])] + [rope_tables(posc_ref[t]) for t in range(nsub)]
    k = jnp.concatenate([rope(kp_ref[...], *tabs[0])]
                        + [rope(kc_ref[t * blk:(t + 1) * blk, :], *tabs[t + 1]) for t in range(nsub)], axis=0)
    v = jnp.concatenate([vp_ref[...], vc_ref[...]], axis=0)
    ks = swap(k)
    k_sw, ks_sw = pltpu.roll(k, ATT_HEAD_DIM, 1), pltpu.roll(ks, ATT_HEAD_DIM, 1)
    v_sw = pltpu.roll(v, ATT_HEAD_DIM, 1)

    qi = lax.broadcasted_iota(jnp.int32, (blk, 2 * blk), 0) + blk
    kj = lax.broadcasted_iota(jnp.int32, (blk, 2 * blk), 1)
    band = (kj <= qi) & (kj > qi - WINDOW)
    k_min = jnp.where(pl.program_id(1) == 0, blk, 0)
    band_first = band & (kj >= k_min)

    scale = ATT_HEAD_DIM ** -0.5
    pairs_per_group = ATT_HEADS // ATT_KV_HEADS // 2
    ones = jnp.ones((2 * blk, LANES), F32)
    one_rhs = jnp.concatenate([jnp.where(lo_half, ones, 0.0), jnp.where(lo_half, 0.0, ones)], axis=0)
    for g in range(ATT_KV_HEADS):
        k_g, k_o = (k, k_sw) if g == 0 else (k_sw, k)
        ks_g, ks_o = (ks, ks_sw) if g == 0 else (ks_sw, ks)
        v_g, v_o = (v, v_sw) if g == 0 else (v_sw, v)
        k_lo = jnp.concatenate([jnp.where(lo_half, k_g, 0.0), jnp.where(lo_half, ks_g, 0.0)], axis=1).astype(BF16)
        k_hi = jnp.concatenate([jnp.where(lo_half, 0.0, k_o), jnp.where(lo_half, 0.0, ks_o)], axis=1).astype(BF16)
        v_lo, v_hi = jnp.where(lo_half, v_g, 0.0), jnp.where(lo_half, 0.0, v_o)
        for t in range(nsub):
            rows = slice(t * blk, (t + 1) * blk)
            keys = slice(t * blk, (t + 2) * blk)
            mask = band_first if t == 0 else band
            k_rhs = jnp.concatenate([k_lo[keys], k_hi[keys]], axis=0)
            c_q, s_q = tabs[t + 1][0] * scale, tabs[t + 1][1] * scale
            v_rhs = jnp.concatenate([jnp.concatenate([v_lo[keys], v_hi[keys]], axis=0), one_rhs],
                                    axis=1).astype(BF16)
            q_rows = []
            for pp in range(pairs_per_group):
                p = g * pairs_per_group + pp
                q_p = q_ref[rows, p * LANES:(p + 1) * LANES]
                q_rows.append(jnp.concatenate([q_p * c_q, q_p * s_q], axis=1).astype(BF16))
            s_all = _dot_nt(jnp.concatenate(q_rows, axis=0), k_rhs)
            p_rows, sink_rows = [], []
            for pp in range(pairs_per_group):
                p = g * pairs_per_group + pp
                halves, m_pair = [], []
                for e in range(2):
                    s = s_all[pp * blk:(pp + 1) * blk, e * 2 * blk:(e + 1) * 2 * blk]
                    s = jnp.where(mask, s, NEG_INF)
                    m = jnp.maximum(jnp.max(s, axis=-1, keepdims=True), sink_ref[2 * p + e])
                    halves.append(jnp.exp(s - m).astype(BF16))
                    m_pair.append(m)
                p_rows.append(jnp.concatenate(halves, axis=1))
                sink_rows.append(
                    jnp.exp(jnp.where(lo_half, sink_ref[2 * p] - m_pair[0], sink_ref[2 * p + 1] - m_pair[1])))
            o_all = _dot(jnp.concatenate(p_rows, axis=0), v_rhs)
            denom = o_all[:, LANES:] + jnp.concatenate(sink_rows, axis=0)
            o_all = o_all[:, :LANES] * (1.0 / denom)
            for pp in range(pairs_per_group):
                p = g * pairs_per_group + pp
                o_ref[rows, p * LANES:(p + 1) * LANES] = o_all[pp * blk:(pp + 1) * blk].astype(BF16)


def _attn(q_src, kv_src, pos3, inv_freq, sinks, bsz, seq, q_col, k_col, v_col, nsub=1):
    blk = WINDOW
    aw = ATT_HEADS * ATT_HEAD_DIM
    ns = seq // (nsub * blk)
    n = bsz * seq
    half = ROPE_DIM // 2

    def cur(b, i):
        return b * ns + i

    def prv(b, i):
        return b * ns * nsub + jnp.maximum(i * nsub - 1, 0)

    return pl.pallas_call(
        functools.partial(_attn_kernel, nsub=nsub),
        grid=(bsz, ns),
        in_specs=[pl.BlockSpec((nsub * blk, aw), lambda b, i: (cur(b, i), q_col // aw)),
                  pl.BlockSpec((nsub * blk, LANES), lambda b, i: (cur(b, i), k_col // LANES)),
                  pl.BlockSpec((nsub * blk, LANES), lambda b, i: (cur(b, i), v_col // LANES)),
                  pl.BlockSpec((blk, LANES), lambda b, i: (prv(b, i), k_col // LANES)),
                  pl.BlockSpec((blk, LANES), lambda b, i: (prv(b, i), v_col // LANES)),
                  pl.BlockSpec((nsub, 1, blk), lambda b, i: (cur(b, i), 0, 0)),
                  pl.BlockSpec((1, 1, blk), lambda b, i: (prv(b, i), 0, 0)),
                  pl.BlockSpec((half, blk), lambda b, i: (0, 0)),
                  pl.BlockSpec(memory_space=pltpu.SMEM)],
        out_specs=pl.BlockSpec((nsub * blk, aw), lambda b, i: (cur(b, i), 0)),
        out_shape=jax.ShapeDtypeStruct((n, aw), BF16),
        compiler_params=_params(("arbitrary", "arbitrary")),
        name="attn",
    )(q_src, kv_src, kv_src, kv_src, kv_src, pos3, pos3, inv_freq, sinks)


def _log_sigmoid(x):
    return jnp.minimum(x, 0.0) - jnp.log1p(jnp.exp(-jnp.abs(x)))


def _mlstm_kernel(xqk_ref, xprev_ref, v_ref, mo_ref, if_ref, ifn_ref, cw_ref, cb_ref, ng_ref, o_ref,
                  xbuf_ref, c_ref, m_ref, cols_ref, rt_ref, dec_ref, *, tc):
    heads, dk, dv, ch = M_HEADS, M_QK_DIM, M_V_DIM, M_CHUNK
    qkw = heads * dk
    nc = tc // ch
    assert ch * 2 == LANES and nc % 2 == 0 and 2 * heads <= SUBLANES
    t_idx = pl.program_id(1)

    lane = lax.broadcasted_iota(jnp.int32, (1, LANES), 1)
    nblk = tc // LANES
    slot = t_idx % 2

    def gate_prep(g_ref, dst):
        gates = g_ref[...]
        logf = _log_sigmoid(gates)
        row = lax.broadcasted_iota(jnp.int32, (tc, tc), 0)
        col = lax.broadcasted_iota(jnp.int32, (tc, tc), 1)
        tri = jnp.where((row // ch == col // ch) & (col <= row), 1.0, 0.0).astype(BF16)
        hi, mid, lo = _split3(logf)
        bcum = _dot(tri, hi) + _dot(tri, mid) + _dot(tri, lo)
        packed_t = jnp.where(lane < heads, gates, bcum).T

        gt8 = packed_t[0:SUBLANES, :]
        bc_t = pltpu.roll(gt8, heads, 0)
        r_t = gt8 - bc_t
        pos = lax.broadcasted_iota(jnp.int32, (1, tc), 1) % ch

        def lane_blocks(a, f):
            return jnp.concatenate([f(a[:, j * LANES:(j + 1) * LANES]) for j in range(nblk)], axis=1)

        rmax_t = r_t
        for sh in (1, 2, 4, 8, 16, 32):
            back = lane_blocks(rmax_t, lambda blk, sh=sh: pltpu.roll(blk, sh, 1))
            rmax_t = jnp.maximum(rmax_t, jnp.where(pos >= sh, back, NEG_INF))

        last = [c * ch + ch - 1 for c in range(nc)]
        b_last = [bc_t[:, i:i + 1] for i in last]
        g_max = [bc_t[:, i:i + 1] + rmax_t[:, i:i + 1] for i in last]
        m_state = m_ref[:, 0:1]
        m_in, m_out = [], []
        for c in range(nc):
            m_in.append(m_state)
            m_state = jnp.maximum(b_last[c] + m_state, g_max[c])
            m_out.append(m_state)
        m_ref[...] = jnp.broadcast_to(m_state, m_ref.shape)
        decay = [jnp.exp(b_last[c] + m_in[c] - m_out[c]) for c in range(nc)]

        def per_time(cols):
            return jnp.concatenate(
                [jnp.where(lane < ch, cols[2 * j], cols[2 * j + 1]) for j in range(nblk)], axis=1)

        m_in_t, m_out_t, b_last_t = per_time(m_in), per_time(m_out), per_time(b_last)
        w_t = jnp.exp(b_last_t + r_t - m_out_t)
        a_log_t = bc_t + m_in_t
        m_t_t = jnp.maximum(a_log_t, bc_t + rmax_t)
        inter_t = jnp.exp(a_log_t - m_t_t)
        eneg_t = jnp.exp(-m_t_t)
        sub = lax.broadcasted_iota(jnp.int32, (SUBLANES, 1), 0)
        cols_t = jnp.concatenate(
            [jnp.where(sub < heads, w_t, pltpu.roll(inter_t, heads, 0)),
             jnp.where(sub < heads, eneg_t, pltpu.roll(m_t_t, heads, 0)),
             bc_t,
             jnp.zeros((LANES - 3 * SUBLANES, tc), F32)], axis=0)
        cols_ref[dst] = cols_t.T
        rt_ref[dst] = r_t
        dec_ref[dst] = per_time(decay)

    @pl.when(t_idx == 0)
    def _():
        c_ref[...] = jnp.zeros_like(c_ref)
        m_ref[...] = jnp.full_like(m_ref, NEG_INF)
        gate_prep(if_ref, 0)

    cols = cols_ref[slot]
    r_t = rt_ref[slot]
    dec_t = dec_ref[slot]
    gate_prep(ifn_ref, 1 - slot)

    pad = SUBLANES
    xbuf_ref[0:pad, :] = jnp.where(t_idx > 0, xprev_ref[...], 0.0)
    xbuf_ref[pad:pad + tc, :] = xqk_ref[...]
    y = cb_ref[...]
    for j in range(CONV_WIDTH):
        off = pad - (CONV_WIDTH - 1) + j
        y = y + cw_ref[j:j + 1, :] * xbuf_ref[off:off + tc, :]
    y = y * jax.nn.sigmoid(y)
    q_all = y[:, :qkw].astype(BF16)
    k_all = y[:, qkw:] * (dk ** -0.5)
    k_all_b = k_all.astype(BF16)

    causal = (lax.broadcasted_iota(jnp.int32, (ch, ch), 1) <= lax.broadcasted_iota(jnp.int32, (ch, ch), 0))
    ng = ng_ref[...]
    ones_v = jnp.ones((tc, LANES), BF16)
    ones_sq = jnp.ones((dv, LANES), BF16)

    def chunks(a):
        return a.reshape(nc, ch, a.shape[-1])

    for h in range(heads):
        w_col = cols[:, h:h + 1]
        inter = cols[:, heads + h:heads + h + 1]
        eneg = cols[:, 2 * heads + h:2 * heads + h + 1]
        m_t = cols[:, 3 * heads + h:3 * heads + h + 1]
        bc = cols[:, 4 * heads + h:4 * heads + h + 1]
        r_row = jnp.stack([r_t[h:h + 1, c * ch:(c + 1) * ch] for c in range(nc)])
        q = chunks(q_all[:, h * dk:(h + 1) * dk])
        kb = chunks(k_all_b[:, h * dk:(h + 1) * dk])
        v = chunks(jnp.concatenate([v_ref[:, h * dv:(h + 1) * dv].astype(BF16), ones_v], axis=1))
        wkb = chunks((w_col * k_all[:, h * dk:(h + 1) * dk]).astype(BF16))

        c_state = c_ref[h]
        c_in = []
        for c in range(nc):
            c_in.append(c_state.astype(BF16))
            c_state = dec_t[h:h + 1, c * ch:c * ch + 1] * c_state + _dot_tn(wkb[c], v[c])
        c_ref[h] = c_state

        d = jnp.where(causal, chunks(bc) + r_row, NEG_INF)
        qk = jnp.stack([_dot_nt(q[c], kb[c]) for c in range(nc)]) * jnp.exp(d - chunks(m_t))
        qkb = qk.astype(BF16)
        num = chunks(inter) * jnp.stack([_dot(q[c], c_in[c]) for c in range(nc)]) \
            + jnp.stack([_dot(qkb[c], v[c]) for c in range(nc)])
        num = num.reshape(tc, dv + LANES)
        inv = 1.0 / jnp.maximum(jnp.abs(num[:, dv:]), eneg)
        halves = [num[:, i * LANES:(i + 1) * LANES] * inv for i in range(dv // LANES)]
        sq = jnp.concatenate([x * x for x in halves], axis=1).astype(BF16)
        rs = lax.rsqrt(_dot(sq, ones_sq) * (1.0 / dv) + NORM_EPS)
        for i, x in enumerate(halves):
            sl = slice(h * dv + i * LANES, h * dv + (i + 1) * LANES)
            o_ref[:, sl] = (x * rs * ng[:, sl] * jax.nn.sigmoid(mo_ref[:, sl])).astype(BF16)


def _mlstm(proj, small, conv_w, conv_b, norm_g, bsz, seq, qk_col, v_col, o_col, if_col, tc=512):
    heads, dk, dv = M_HEADS, M_QK_DIM, M_V_DIM
    qk2 = 2 * heads * dk
    mw = heads * dv
    nt = seq // tc
    n = bsz * seq

    def rows(b, t):
        return b * nt + t

    def prev_rows(b, t):
        return jnp.maximum((b * nt + t) * (tc // SUBLANES) - 1, 0)

    return pl.pallas_call(
        functools.partial(_mlstm_kernel, tc=tc),
        grid=(bsz, nt),
        in_specs=[pl.BlockSpec((tc, qk2), lambda b, t: (rows(b, t), qk_col // qk2)),
                  pl.BlockSpec((SUBLANES, qk2), lambda b, t: (prev_rows(b, t), qk_col // qk2)),
                  pl.BlockSpec((tc, mw), lambda b, t: (rows(b, t), v_col // mw)),
                  pl.BlockSpec((tc, mw), lambda b, t: (rows(b, t), o_col // mw)),
                  pl.BlockSpec((tc, LANES), lambda b, t: (rows(b, t), if_col // LANES)),
                  pl.BlockSpec((tc, LANES), lambda b, t: (rows(b, jnp.minimum(t + 1, nt - 1)), if_col // LANES)),
                  pl.BlockSpec((CONV_WIDTH, qk2), lambda b, t: (0, 0)),
                  pl.BlockSpec((1, qk2), lambda b, t: (0, 0)),
                  pl.BlockSpec((1, mw), lambda b, t: (0, 0))],
        out_specs=pl.BlockSpec((tc, mw), lambda b, t: (rows(b, t), 0)),
        out_shape=jax.ShapeDtypeStruct((n, mw), BF16),
        scratch_shapes=[pltpu.VMEM((SUBLANES + tc, qk2), F32),
                        pltpu.VMEM((heads, dk, dv + LANES), F32),
                        pltpu.VMEM((SUBLANES, LANES), F32),
                        pltpu.VMEM((2, tc, LANES), F32),
                        pltpu.VMEM((2, SUBLANES, tc), F32),
                        pltpu.VMEM((2, SUBLANES, tc), F32)],
        compiler_params=_params(("arbitrary", "arbitrary")),
        name="mlstm",
    )(proj, proj, proj, proj, small, small, conv_w, conv_b.reshape(1, qk2), norm_g.reshape(1, mw))


def _merge_kernel(oa_ref, hm_ref, u_ref, h_ref, gt_ref, wa_ref, wm_ref, wga_ref, wgm_ref, wo_ref, o_ref, *, rows):
    j = pl.program_id(1)
    nj = pl.num_programs(1)
    tm = h_ref.shape[0]

    def step(first, last):
        for r in range(tm // rows):
            sl = slice(r * rows, (r + 1) * rows)
            u = u_ref[sl, :]
            y_att = _dot(oa_ref[sl, :], wa_ref[...])
            y_m = _dot(hm_ref[sl, :], wm_ref[...])
            g_att = jax.nn.sigmoid(_dot(u, wga_ref[...]))
            g_m = jax.nn.sigmoid(_dot(u, wgm_ref[...]))
            acc = _dot((g_att * y_att + g_m * y_m).astype(BF16), wo_ref[...])
            if not first:
                acc = o_ref[sl, :] + acc
            if last:
                acc = h_ref[sl, :] + gt_ref[0] * acc
            o_ref[sl, :] = acc

    pl.when(j == 0)(lambda: step(True, False))
    pl.when((j > 0) & (j < nj - 1))(lambda: step(False, False))
    pl.when(j == nj - 1)(lambda: step(False, True))


def _merge(o_att, hm, u, h, seq, mod3, mod_idx, w_att_up, w_mlstm_up, w_gates, gates_col, w_out, tm=512, tn=512,
           rows=256):
    n, d = h.shape
    aw = o_att.shape[1]
    mw = hm.shape[1]
    nj = d // tn
    assert nj >= 2 and gates_col % tn == 0
    g0 = gates_col // tn
    tiles_per_seq = seq // tm
    return pl.pallas_call(
        functools.partial(_merge_kernel, rows=rows),
        grid=(n // tm, nj),
        in_specs=[pl.BlockSpec((tm, aw), lambda i, j: (i, 0)),
                  pl.BlockSpec((tm, mw), lambda i, j: (i, 0)),
                  pl.BlockSpec((tm, d), lambda i, j: (i, 0)),
                  pl.BlockSpec((tm, d), lambda i, j: (i, 0)),
                  pl.BlockSpec((1, 1, d), lambda i, j: ((i // tiles_per_seq) * N_MOD + mod_idx, 0, 0)),
                  pl.BlockSpec((aw, tn), lambda i, j: (0, j)),
                  pl.BlockSpec((mw, tn), lambda i, j: (0, j)),
                  pl.BlockSpec((d, tn), lambda i, j: (0, g0 + j)),
                  pl.BlockSpec((d, tn), lambda i, j: (0, g0 + nj + j)),
                  pl.BlockSpec((tn, d), lambda i, j: (j, 0))],
        out_specs=pl.BlockSpec((tm, d), lambda i, j: (i, 0)),
        out_shape=jax.ShapeDtypeStruct((n, d), F32),
        compiler_params=_params(("arbitrary", "arbitrary")),
        name="merge",
    )(o_att, hm, u, h, mod3, w_att_up, w_mlstm_up, w_gates, w_gates, w_out)


def _repack_kernel(wt_ref, o_ref, *, valid_of_step):
    x = wt_ref[...].T
    valid = valid_of_step(pl.program_id(0))
    lane = lax.broadcasted_iota(jnp.int32, (1, x.shape[1]), 1)
    o_ref[...] = jnp.where(lane < valid, x, 0.0).astype(BF16)


def _repack(w_t, pieces, tb=256):
    f, d = w_t.shape
    starts, valids = [], []
    for src, width, valid in pieces:
        assert width % tb == 0 and src % SUBLANES == 0 and src + width <= f
        for b in range(width // tb):
            starts.append(src + b * tb)
            valids.append(min(max(valid - b * tb, 0), tb))

    def lookup(table, i):
        out = jnp.int32(table[0])
        for s, v in enumerate(table[1:], 1):
            out = jnp.where(i == s, jnp.int32(v), out)
        return out

    return pl.pallas_call(
        functools.partial(_repack_kernel, valid_of_step=functools.partial(lookup, valids)),
        grid=(len(starts),),
        in_specs=[pl.BlockSpec((pl.Element(tb), pl.Element(d)),
                               lambda i: (pl.multiple_of(lookup(starts, i), SUBLANES), 0))],
        out_specs=pl.BlockSpec((d, tb), lambda i: (0, i)),
        out_shape=jax.ShapeDtypeStruct((d, len(starts) * tb), BF16),
        compiler_params=_params(("arbitrary",)),
        name="repack",
    )(w_t)


def kernel(x, c, positions, w_ada, b_ada, norm1_g, ffn1_w_in, ffn1_w_out, norm2_g, w_in, conv_w, conv_b,
           b_igate, b_fgate, sinks, mlstm_norm_g, w_att_up, w_mlstm_up, w_out, norm3_g, ffn2_w_in,
           ffn2_w_out, final_g):
    bsz, seq, d = x.shape
    depth = w_ada.shape[0]
    n = bsz * seq
    aw = ATT_HEADS * ATT_HEAD_DIM
    kvw = ATT_KV_HEADS * ATT_HEAD_DIM
    qk2 = 2 * M_HEADS * M_QK_DIM
    mw = M_HEADS * M_V_DIM

    sizes = (aw, kvw, kvw, qk2, mw, M_HEADS, M_HEADS, mw, 2 * d)
    offs = [0]
    for s in sizes:
        offs.append(offs[-1] + s)
    (o_q, o_k, o_v, o_qk, o_mv, o_i, o_f, o_mo, o_g) = offs[:-1]
    q_col, qk_col, mv_col, mo_col = 0, aw, aw + qk2, aw + qk2 + mw
    k_col, v_col, if_col = 0, kvw, 2 * kvw

    half = ROPE_DIM // 2
    inv_freq = jnp.power(ROPE_THETA, -jnp.arange(half, dtype=F32) * 2.0 / ROPE_DIM)
    inv_freq = jnp.broadcast_to(inv_freq[:, None], (half, WINDOW))
    pos3 = positions.reshape(n // WINDOW, 1, WINDOW)

    h = x.reshape(n, d)
    for l in range(depth):
        mod3 = _ada(c, w_ada[l], b_ada[l]).reshape(bsz * N_MOD, 1, d)

        h, (ffn2_in_b, ffn2_out_b) = _ffn(
            h, seq, norm1_g[l], mod3, 0, ffn1_w_in[l].astype(BF16), ffn1_w_out[l].astype(BF16),
            casts=(ffn2_w_in[l], ffn2_w_out[l]))

        n_main, n_small = aw + qk2 + 2 * mw, 2 * (2 * kvw)
        w_all = _repack(jnp.swapaxes(w_in[l], 0, 1), (
            (o_q, aw, aw), (o_qk, qk2, qk2), (o_mv, mw, mw), (o_mo, mw, mw),
            (o_g, 2 * d, 2 * d),
            (o_k, 2 * kvw, 2 * kvw), (o_i, 2 * kvw, 2 * M_HEADS)))
        g_col, s_col = n_main, n_main + 2 * d
        b_small = jnp.concatenate(
            [jnp.zeros((2 * kvw,), F32), b_igate[l], b_fgate[l], jnp.zeros((2 * kvw - 2 * M_HEADS,), F32)])[None, :]
        proj, small, u = _proj(h, seq, norm2_g[l], mod3, 3, w_all, n_main, s_col, b_small)

        o_att = _attn(proj, small, pos3, inv_freq, sinks[l], bsz, seq, q_col, k_col, v_col)
        hm = _mlstm(proj, small, conv_w[l], conv_b[l], mlstm_norm_g[l], bsz, seq, qk_col, mv_col, mo_col, if_col)
        h = _merge(o_att, hm, u, h, seq, mod3, 5, w_att_up[l].astype(BF16), w_mlstm_up[l].astype(BF16), w_all,
                   g_col, w_out[l].astype(BF16))

        last = l == depth - 1
        h, _ = _ffn(h, seq, norm3_g[l], mod3, 6, ffn2_in_b, ffn2_out_b, final_g=final_g if last else None)
    return h.reshape(bsz, seq, d)
```

```python
import functools

import jax
import jax.numpy as jnp
from jax import lax
from jax.experimental import pallas as pl
from jax.experimental.pallas import tpu as pltpu

F32 = jnp.float32
BF16 = jnp.bfloat16

ATT_HEADS = 16
ATT_KV_HEADS = 2
ATT_HEAD_DIM = 64
WINDOW = 128
ROPE_DIM = ATT_HEAD_DIM // 4
ROPE_THETA = 500000.0
M_HEADS = 4
M_QK_DIM = 128
M_V_DIM = 256
M_CHUNK = 64
CONV_WIDTH = 4
NORM_EPS = 1e-6
NEG_INF = -1e30
N_MOD = 9

LANES = 128
SUBLANES = 8
VMEM_LIMIT = 60 * 1024 * 1024


def _dot(a, b):
    return jnp.dot(a, b, preferred_element_type=F32)


def _dot_nt(a, b):
    return lax.dot_general(a, b, (((1,), (1,)), ((), ())), preferred_element_type=F32)


def _dot_tn(a, b):
    return lax.dot_general(a, b, (((0,), (0,)), ((), ())), preferred_element_type=F32)


def _split3(x):
    hi = x.astype(BF16)
    r1 = x - hi.astype(F32)
    mid = r1.astype(BF16)
    lo = (r1 - mid.astype(F32)).astype(BF16)
    return hi, mid, lo


def _rms(x):
    return x * lax.rsqrt(jnp.mean(x * x, axis=-1, keepdims=True) + NORM_EPS)


def _params(sem):
    return pltpu.CompilerParams(dimension_semantics=sem, vmem_limit_bytes=VMEM_LIMIT)


def _ada_kernel(c_ref, b_ref, w_ref, o_ref):
    @pl.when(pl.program_id(0) == 0)
    def _():
        o_ref[...] = jnp.broadcast_to(b_ref[...], o_ref.shape)

    c = c_ref[...]
    hi, mid, lo = _split3(c * jax.nn.sigmoid(c))
    w = w_ref[...].astype(BF16)
    o_ref[...] += _dot(hi, w) + _dot(mid, w) + _dot(lo, w)


def _ada(c, w_ada, b_ada, tk=256):
    bsz, d = c.shape
    n = w_ada.shape[1]
    cp = jnp.zeros((SUBLANES, d), F32).at[:bsz].set(c)
    out = pl.pallas_call(
        _ada_kernel,
        grid=(d // tk,),
        in_specs=[pl.BlockSpec((SUBLANES, tk), lambda k: (0, k)),
                  pl.BlockSpec((1, n), lambda k: (0, 0)),
                  pl.BlockSpec((tk, n), lambda k: (k, 0))],
        out_specs=pl.BlockSpec((SUBLANES, n), lambda k: (0, 0)),
        out_shape=jax.ShapeDtypeStruct((SUBLANES, n), F32),
        compiler_params=_params(("arbitrary",)),
        name="ada",
    )(cp, b_ada.reshape(1, n), w_ada)
    return out[:bsz]


def _ffn_kernel(x_ref, g_ref, sh_ref, sc_ref, gt_ref, wa_ref, wb_ref, wo_ref, *rest, final_norm, rows, n_cast):
    rest = list(rest)
    fg_ref = rest.pop(0) if final_norm else None
    cast_in, o_ref, cast_out, u_ref = rest[:n_cast], rest[n_cast], rest[n_cast + 1:2 * n_cast + 1], rest[-1]
    j = pl.program_id(1)
    nj = pl.num_programs(1)
    tm = x_ref.shape[0]

    for src, dst in zip(cast_in, cast_out):
        dst[...] = src[...].astype(BF16)

    def step(first, last):
        chunk = rows if (first or last) else tm
        for r in range(tm // chunk):
            sl = slice(r * chunk, (r + 1) * chunk)
            if first:
                y = _rms(x_ref[sl, :]) * g_ref[...]
                u = (y * (1.0 + sc_ref[0]) + sh_ref[0]).astype(BF16)
                u_ref[sl, :] = u
            else:
                u = u_ref[sl, :]
            a = _dot(u, wa_ref[...])
            b = _dot(u, wb_ref[...])
            acc = _dot((a * jax.nn.sigmoid(a) * b).astype(BF16), wo_ref[...])
            if not first:
                acc = o_ref[sl, :] + acc
            if last:
                acc = x_ref[sl, :] + (0.5 * gt_ref[0]) * acc
                if final_norm:
                    acc = _rms(acc) * fg_ref[...]
            o_ref[sl, :] = acc

    pl.when(j == 0)(lambda: step(True, False))
    pl.when((j > 0) & (j < nj - 1))(lambda: step(False, False))
    pl.when(j == nj - 1)(lambda: step(False, True))


def _cast_spec(shape, n_i, n_j):
    r, c = shape
    sub, lanes = 2 * SUBLANES, LANES
    if r % n_i == 0 and c % n_j == 0 and (r // n_i) % sub == 0 and (c // n_j) % lanes == 0:
        return pl.BlockSpec((r // n_i, c // n_j), lambda i, j: (i, j))
    if r % n_j == 0 and c % n_i == 0 and (r // n_j) % sub == 0 and (c // n_i) % lanes == 0:
        return pl.BlockSpec((r // n_j, c // n_i), lambda i, j: (j, i))
    assert r % n_i == 0 and (r // n_i) % sub == 0, shape
    return pl.BlockSpec((r // n_i, c), lambda i, j: (i, 0))


def _ffn(h, seq, norm_g, mod3, mod_base, w_in, w_out, final_g=None, casts=(), tm=1024, tf=512, rows=256):
    n, d = h.shape
    dff = w_out.shape[0]
    nj = dff // tf
    assert nj >= 2
    tiles_per_seq = seq // tm

    def mod_spec(k):
        return pl.BlockSpec((1, 1, d), lambda i, j: ((i // tiles_per_seq) * N_MOD + mod_base + k, 0, 0))

    in_specs = [pl.BlockSpec((tm, d), lambda i, j: (i, 0)),
                pl.BlockSpec((1, d), lambda i, j: (0, 0)),
                mod_spec(0), mod_spec(1), mod_spec(2),
                pl.BlockSpec((d, tf), lambda i, j: (0, j)),
                pl.BlockSpec((d, tf), lambda i, j: (0, j + nj)),
                pl.BlockSpec((tf, d), lambda i, j: (j, 0))]
    args = [h, norm_g.reshape(1, d), mod3, mod3, mod3, w_in, w_in, w_out]
    if final_g is not None:
        in_specs.append(pl.BlockSpec((1, d), lambda i, j: (0, 0)))
        args.append(final_g.reshape(1, d))
    cast_specs = [_cast_spec(w.shape, n // tm, nj) for w in casts]
    outs = pl.pallas_call(
        functools.partial(_ffn_kernel, final_norm=final_g is not None, rows=rows, n_cast=len(casts)),
        grid=(n // tm, nj),
        in_specs=in_specs + cast_specs,
        out_specs=[pl.BlockSpec((tm, d), lambda i, j: (i, 0))] + [_cast_spec(w.shape, n // tm, nj) for w in casts],
        out_shape=[jax.ShapeDtypeStruct((n, d), F32)] + [jax.ShapeDtypeStruct(w.shape, BF16) for w in casts],
        scratch_shapes=[pltpu.VMEM((tm, d), BF16)],
        compiler_params=_params(("arbitrary", "arbitrary")),
        name="ffn_final" if final_g is not None else "ffn",
    )(*args, *casts)
    return outs[0], tuple(outs[1:])


def _proj_kernel(x_ref, g_ref, sh_ref, sc_ref, w_ref, ws_ref, bs_ref, *rest, rows, n_cast):
    cast_in, (p_ref, s_ref, u_ref), cast_out = rest[:n_cast], rest[n_cast:n_cast + 3], rest[n_cast + 3:]
    j = pl.program_id(1)
    tm = x_ref.shape[0]

    for src, dst in zip(cast_in, cast_out):
        dst[...] = src[...].astype(BF16)

    @pl.when(j == 0)
    def _():
        for r in range(tm // rows):
            sl = slice(r * rows, (r + 1) * rows)
            y = _rms(x_ref[sl, :]) * g_ref[...]
            u = (y * (1.0 + sc_ref[0]) + sh_ref[0]).astype(BF16)
            u_ref[sl, :] = u
            s_ref[sl, :] = _dot(u, ws_ref[...]) + bs_ref[...]
            p_ref[sl, :] = _dot(u, w_ref[...])

    @pl.when(j > 0)
    def _():
        p_ref[...] = _dot(u_ref[...], w_ref[...])


def _proj(h, seq, norm_g, mod3, mod_base, w_all, ncol, small_col, b_small, casts=(), tm=1024, tn=1024, rows=256):
    n, d = h.shape
    nsmall = b_small.shape[1]
    assert small_col % nsmall == 0 and ncol % tn == 0
    tiles_per_seq = seq // tm
    n_i, n_j = n // tm, ncol // tn

    def mod_spec(k):
        return pl.BlockSpec((1, 1, d), lambda i, j: ((i // tiles_per_seq) * N_MOD + mod_base + k, 0, 0))

    outs = pl.pallas_call(
        functools.partial(_proj_kernel, rows=rows, n_cast=len(casts)),
        grid=(n_i, n_j),
        in_specs=[pl.BlockSpec((tm, d), lambda i, j: (i, 0)),
                  pl.BlockSpec((1, d), lambda i, j: (0, 0)),
                  mod_spec(0), mod_spec(1),
                  pl.BlockSpec((d, tn), lambda i, j: (0, j)),
                  pl.BlockSpec((d, nsmall), lambda i, j: (0, small_col // nsmall)),
                  pl.BlockSpec((1, nsmall), lambda i, j: (0, 0))]
        + [_cast_spec(w.shape, n_i, n_j) for w in casts],
        out_specs=[pl.BlockSpec((tm, tn), lambda i, j: (i, j)),
                   pl.BlockSpec((tm, nsmall), lambda i, j: (i, 0)),
                   pl.BlockSpec((tm, d), lambda i, j: (i, 0))]
        + [_cast_spec(w.shape, n_i, n_j) for w in casts],
        out_shape=[jax.ShapeDtypeStruct((n, ncol), F32),
                   jax.ShapeDtypeStruct((n, nsmall), F32),
                   jax.ShapeDtypeStruct((n, d), BF16)]
        + [jax.ShapeDtypeStruct(w.shape, BF16) for w in casts],
        compiler_params=_params(("arbitrary", "arbitrary")),
        name="proj",
    )(h, norm_g.reshape(1, d), mod3, mod3, w_all, w_all, b_small, *casts)
    return outs[0], outs[1], outs[2], tuple(outs[3:])


def _attn_kernel(q_ref, kc_ref, vc_ref, kp_ref, vp_ref, posc_ref, posp_ref, invf_ref, sink_ref, o_ref, *, nsub):
    blk = WINDOW
    lane = lax.broadcasted_iota(jnp.int32, (1, LANES), 1)
    hl = lane % ATT_HEAD_DIM
    half = ROPE_DIM // 2
    lo_half = lane < ATT_HEAD_DIM

    def rope_tables(pos):
        pos = pos.astype(F32)
        ang = invf_ref[...] * pos
        reps = LANES // half
        ct = jnp.tile(jnp.cos(ang), (reps, 1)).T
        st = jnp.tile(jnp.sin(ang), (reps, 1)).T
        c = jnp.where(hl < ROPE_DIM, ct, 1.0)
        s = jnp.where(hl < ROPE_DIM, st, 0.0)
        return c, s

    def swap(x):
        return jnp.where(hl < half, pltpu.roll(x, LANES - half, 1),
                         jnp.where(hl < ROPE_DIM, -pltpu.roll(x, half, 1), 0.0))

    def rope(x, c, s):
        return x * c - swap(x) * s

    tabs = [rope_tables(posp_ref[0])] + [rope_tables(posc_ref[t]) for t in range(nsub)]
    k = jnp.concatenate([rope(kp_ref[...], *tabs[0])]
                        + [rope(kc_ref[t * blk:(t + 1) * blk, :], *tabs[t + 1]) for t in range(nsub)], axis=0)
    v = jnp.concatenate([vp_ref[...], vc_ref[...]], axis=0)
    ks = swap(k)
    k_sw, ks_sw = pltpu.roll(k, ATT_HEAD_DIM, 1), pltpu.roll(ks, ATT_HEAD_DIM, 1)
    v_sw = pltpu.roll(v, ATT_HEAD_DIM, 1)

    qi = lax.broadcasted_iota(jnp.int32, (blk, 2 * blk), 0) + blk
    kj = lax.broadcasted_iota(jnp.int32, (blk, 2 * blk), 1)
    band = (kj <= qi) & (kj > qi - WINDOW)
    slot0 = kj[0:1, :] == 0
    not_slot0 = lax.broadcasted_iota(jnp.int32, (2 * blk, 1), 0) > 0
    k_min = jnp.where(pl.program_id(1) == 0, blk, 0)
    band_first = band & (kj >= k_min)

    scale = ATT_HEAD_DIM ** -0.5
    pairs_per_group = ATT_HEADS // ATT_KV_HEADS // 2
    ones = jnp.ones((2 * blk, LANES), F32)
    one_rhs = jnp.concatenate([jnp.where(lo_half, ones, 0.0), jnp.where(lo_half, 0.0, ones)], axis=0)
    for g in range(ATT_KV_HEADS):
        k_g, k_o = (k, k_sw) if g == 0 else (k_sw, k)
        ks_g, ks_o = (ks, ks_sw) if g == 0 else (ks_sw, ks)
        v_g, v_o = (v, v_sw) if g == 0 else (v_sw, v)
        k_lo = jnp.concatenate([jnp.where(lo_half, k_g, 0.0), jnp.where(lo_half, ks_g, 0.0)], axis=1).astype(BF16)
        k_hi = jnp.concatenate([jnp.where(lo_half, 0.0, k_o), jnp.where(lo_half, 0.0, ks_o)], axis=1).astype(BF16)
        v_lo, v_hi = jnp.where(lo_half, v_g, 0.0), jnp.where(lo_half, 0.0, v_o)
        for t in range(nsub):
            rows = slice(t * blk, (t + 1) * blk)
            keys = slice(t * blk, (t + 2) * blk)
            mask = band_first if t == 0 else band
            k_rhs = jnp.concatenate([k_lo[keys], k_hi[keys]], axis=0)
            c_q, s_q = tabs[t + 1][0] * scale, tabs[t + 1][1] * scale
            v_rhs = jnp.concatenate(
                [jnp.concatenate([jnp.where(not_slot0, v_lo[keys], 0.0), jnp.where(not_slot0, v_hi[keys], 0.0)], axis=0),
                 one_rhs], axis=1).astype(BF16)
            q_rows = []
            for pp in range(pairs_per_group):
                p = g * pairs_per_group + pp
                q_p = q_ref[rows, p * LANES:(p + 1) * LANES]
                q_rows.append(jnp.concatenate([q_p * c_q, q_p * s_q], axis=1).astype(BF16))
            s_all = _dot_nt(jnp.concatenate(q_rows, axis=0), k_rhs)
            p_rows = []
            for pp in range(pairs_per_group):
                p = g * pairs_per_group + pp
                halves = []
                for e in range(2):
                    s = s_all[pp * blk:(pp + 1) * blk, e * 2 * blk:(e + 1) * 2 * blk]
                    s = jnp.where(mask, s, jnp.where(slot0, sink_ref[2 * p + e], NEG_INF))
                    m = jnp.max(s, axis=-1, keepdims=True)
                    halves.append(jnp.exp(s - m).astype(BF16))
                p_rows.append(jnp.concatenate(halves, axis=1))
            o_all = _dot(jnp.concatenate(p_rows, axis=0), v_rhs)
            o_all = o_all[:, :LANES] * (1.0 / o_all[:, LANES:])
            for pp in range(pairs_per_group):
                p = g * pairs_per_group + pp
                o_ref[rows, p * LANES:(p + 1) * LANES] = o_all[pp * blk:(pp + 1) * blk].astype(BF16)


def _attn(q_src, kv_src, pos3, inv_freq, sinks, bsz, seq, q_col, k_col, v_col, nsub=1):
    blk = WINDOW
    aw = ATT_HEADS * ATT_HEAD_DIM
    ns = seq // (nsub * blk)
    n = bsz * seq
    half = ROPE_DIM // 2

    def cur(b, i):
        return b * ns + i

    def prv(b, i):
        return b * ns * nsub + jnp.maximum(i * nsub - 1, 0)

    return pl.pallas_call(
        functools.partial(_attn_kernel, nsub=nsub),
        grid=(bsz, ns),
        in_specs=[pl.BlockSpec((nsub * blk, aw), lambda b, i: (cur(b, i), q_col // aw)),
                  pl.BlockSpec((nsub * blk, LANES), lambda b, i: (cur(b, i), k_col // LANES)),
                  pl.BlockSpec((nsub * blk, LANES), lambda b, i: (cur(b, i), v_col // LANES)),
                  pl.BlockSpec((blk, LANES), lambda b, i: (prv(b, i), k_col // LANES)),
                  pl.BlockSpec((blk, LANES), lambda b, i: (prv(b, i), v_col // LANES)),
                  pl.BlockSpec((nsub, 1, blk), lambda b, i: (cur(b, i), 0, 0)),
                  pl.BlockSpec((1, 1, blk), lambda b, i: (prv(b, i), 0, 0)),
                  pl.BlockSpec((half, blk), lambda b, i: (0, 0)),
                  pl.BlockSpec(memory_space=pltpu.SMEM)],
        out_specs=pl.BlockSpec((nsub * blk, aw), lambda b, i: (cur(b, i), 0)),
        out_shape=jax.ShapeDtypeStruct((n, aw), BF16),
        compiler_params=_params(("arbitrary", "arbitrary")),
        name="attn",
    )(q_src, kv_src, kv_src, kv_src, kv_src, pos3, pos3, inv_freq, sinks)


def _log_sigmoid(x):
    return jnp.minimum(x, 0.0) - jnp.log1p(jnp.exp(-jnp.abs(x)))


def _mlstm_kernel(xqk_ref, xprev_ref, v_ref, mo_ref, if_ref, ifn_ref, cw_ref, cb_ref, ng_ref, o_ref,
                  xbuf_ref, c_ref, m_ref, cols_ref, rt_ref, dec_ref, *, tc):
    heads, dk, dv, ch = M_HEADS, M_QK_DIM, M_V_DIM, M_CHUNK
    qkw = heads * dk
    nc = tc // ch
    assert ch * 2 == LANES and nc % 2 == 0 and 2 * heads <= SUBLANES
    t_idx = pl.program_id(1)

    lane = lax.broadcasted_iota(jnp.int32, (1, LANES), 1)
    nblk = tc // LANES
    slot = t_idx % 2

    def gate_prep(g_ref, dst):
        gates = g_ref[...]
        logf = _log_sigmoid(gates)
        row = lax.broadcasted_iota(jnp.int32, (tc, tc), 0)
        col = lax.broadcasted_iota(jnp.int32, (tc, tc), 1)
        tri = jnp.where((row // ch == col // ch) & (col <= row), 1.0, 0.0).astype(BF16)
        hi, mid, lo = _split3(logf)
        bcum = _dot(tri, hi) + _dot(tri, mid) + _dot(tri, lo)
        packed_t = jnp.where(lane < heads, gates, bcum).T

        gt8 = packed_t[0:SUBLANES, :]
        bc_t = pltpu.roll(gt8, heads, 0)
        r_t = gt8 - bc_t
        pos = lax.broadcasted_iota(jnp.int32, (1, tc), 1) % ch

        def lane_blocks(a, f):
            return jnp.concatenate([f(a[:, j * LANES:(j + 1) * LANES]) for j in range(nblk)], axis=1)

        rmax_t = r_t
        for sh in (1, 2, 4, 8, 16, 32):
            back = lane_blocks(rmax_t, lambda blk, sh=sh: pltpu.roll(blk, sh, 1))
            rmax_t = jnp.maximum(rmax_t, jnp.where(pos >= sh, back, NEG_INF))

        last = [c * ch + ch - 1 for c in range(nc)]
        b_last = [bc_t[:, i:i + 1] for i in last]
        g_max = [bc_t[:, i:i + 1] + rmax_t[:, i:i + 1] for i in last]
        m_state = m_ref[:, 0:1]
        m_in, m_out = [], []
        for c in range(nc):
            m_in.append(m_state)
            m_state = jnp.maximum(b_last[c] + m_state, g_max[c])
            m_out.append(m_state)
        m_ref[...] = jnp.broadcast_to(m_state, m_ref.shape)
        decay = [jnp.exp(b_last[c] + m_in[c] - m_out[c]) for c in range(nc)]

        def per_time(cols):
            return jnp.concatenate(
                [jnp.where(lane < ch, cols[2 * j], cols[2 * j + 1]) for j in range(nblk)], axis=1)

        m_in_t, m_out_t, b_last_t = per_time(m_in), per_time(m_out), per_time(b_last)
        w_t = jnp.exp(b_last_t + r_t - m_out_t)
        a_log_t = bc_t + m_in_t
        m_t_t = jnp.maximum(a_log_t, bc_t + rmax_t)
        inter_t = jnp.exp(a_log_t - m_t_t)
        eneg_t = jnp.exp(-m_t_t)
        sub = lax.broadcasted_iota(jnp.int32, (SUBLANES, 1), 0)
        cols_t = jnp.concatenate(
            [jnp.where(sub < heads, w_t, pltpu.roll(inter_t, heads, 0)),
             jnp.where(sub < heads, eneg_t, pltpu.roll(m_t_t, heads, 0)),
             bc_t,
             jnp.zeros((LANES - 3 * SUBLANES, tc), F32)], axis=0)
        cols_ref[dst] = cols_t.T
        rt_ref[dst] = r_t
        dec_ref[dst] = per_time(decay)

    @pl.when(t_idx == 0)
    def _():
        c_ref[...] = jnp.zeros_like(c_ref)
        m_ref[...] = jnp.full_like(m_ref, NEG_INF)
        gate_prep(if_ref, 0)

    cols = cols_ref[slot]
    r_t = rt_ref[slot]
    dec_t = dec_ref[slot]
    gate_prep(ifn_ref, 1 - slot)

    pad = SUBLANES
    xbuf_ref[0:pad, :] = jnp.where(t_idx > 0, xprev_ref[...], 0.0)
    xbuf_ref[pad:pad + tc, :] = xqk_ref[...]
    y = cb_ref[...]
    for j in range(CONV_WIDTH):
        off = pad - (CONV_WIDTH - 1) + j
        y = y + cw_ref[j:j + 1, :] * xbuf_ref[off:off + tc, :]
    y = y * jax.nn.sigmoid(y)
    q_all = y[:, :qkw].astype(BF16)
    k_all = y[:, qkw:] * (dk ** -0.5)
    k_all_b = k_all.astype(BF16)

    causal = (lax.broadcasted_iota(jnp.int32, (ch, ch), 1) <= lax.broadcasted_iota(jnp.int32, (ch, ch), 0))
    ng = ng_ref[...]
    ones_v = jnp.ones((tc, LANES), BF16)
    ones_sq = jnp.ones((dv, LANES), BF16)

    def chunks(a):
        return a.reshape(nc, ch, a.shape[-1])

    for h in range(heads):
        w_col = cols[:, h:h + 1]
        inter = cols[:, heads + h:heads + h + 1]
        eneg = cols[:, 2 * heads + h:2 * heads + h + 1]
        m_t = cols[:, 3 * heads + h:3 * heads + h + 1]
        bc = cols[:, 4 * heads + h:4 * heads + h + 1]
        r_row = jnp.stack([r_t[h:h + 1, c * ch:(c + 1) * ch] for c in range(nc)])
        q = chunks(q_all[:, h * dk:(h + 1) * dk])
        kb = chunks(k_all_b[:, h * dk:(h + 1) * dk])
        v = chunks(jnp.concatenate([v_ref[:, h * dv:(h + 1) * dv].astype(BF16), ones_v], axis=1))
        wkb = chunks((w_col * k_all[:, h * dk:(h + 1) * dk]).astype(BF16))

        c_state = c_ref[h]
        c_in = []
        for c in range(nc):
            c_in.append(c_state.astype(BF16))
            c_state = dec_t[h:h + 1, c * ch:c * ch + 1] * c_state + _dot_tn(wkb[c], v[c])
        c_ref[h] = c_state

        d = jnp.where(causal, chunks(bc) + r_row, NEG_INF)
        qk = jnp.stack([_dot_nt(q[c], kb[c]) for c in range(nc)]) * jnp.exp(d - chunks(m_t))
        qkb = qk.astype(BF16)
        num = chunks(inter) * jnp.stack([_dot(q[c], c_in[c]) for c in range(nc)]) \
            + jnp.stack([_dot(qkb[c], v[c]) for c in range(nc)])
        num = num.reshape(tc, dv + LANES)
        inv = 1.0 / jnp.maximum(jnp.abs(num[:, dv:]), eneg)
        halves = [num[:, i * LANES:(i + 1) * LANES] * inv for i in range(dv // LANES)]
        sq = jnp.concatenate([x * x for x in halves], axis=1).astype(BF16)
        rs = lax.rsqrt(_dot(sq, ones_sq) * (1.0 / dv) + NORM_EPS)
        for i, x in enumerate(halves):
            sl = slice(h * dv + i * LANES, h * dv + (i + 1) * LANES)
            o_ref[:, sl] = (x * rs * ng[:, sl] * jax.nn.sigmoid(mo_ref[:, sl])).astype(BF16)


def _mlstm(proj, small, conv_w, conv_b, norm_g, bsz, seq, qk_col, v_col, o_col, if_col, tc=512):
    heads, dk, dv = M_HEADS, M_QK_DIM, M_V_DIM
    qk2 = 2 * heads * dk
    mw = heads * dv
    nt = seq // tc
    n = bsz * seq

    def rows(b, t):
        return b * nt + t

    def prev_rows(b, t):
        return jnp.maximum((b * nt + t) * (tc // SUBLANES) - 1, 0)

    return pl.pallas_call(
        functools.partial(_mlstm_kernel, tc=tc),
        grid=(bsz, nt),
        in_specs=[pl.BlockSpec((tc, qk2), lambda b, t: (rows(b, t), qk_col // qk2)),
                  pl.BlockSpec((SUBLANES, qk2), lambda b, t: (prev_rows(b, t), qk_col // qk2)),
                  pl.BlockSpec((tc, mw), lambda b, t: (rows(b, t), v_col // mw)),
                  pl.BlockSpec((tc, mw), lambda b, t: (rows(b, t), o_col // mw)),
                  pl.BlockSpec((tc, LANES), lambda b, t: (rows(b, t), if_col // LANES)),
                  pl.BlockSpec((tc, LANES), lambda b, t: (rows(b, jnp.minimum(t + 1, nt - 1)), if_col // LANES)),
                  pl.BlockSpec((CONV_WIDTH, qk2), lambda b, t: (0, 0)),
                  pl.BlockSpec((1, qk2), lambda b, t: (0, 0)),
                  pl.BlockSpec((1, mw), lambda b, t: (0, 0))],
        out_specs=pl.BlockSpec((tc, mw), lambda b, t: (rows(b, t), 0)),
        out_shape=jax.ShapeDtypeStruct((n, mw), BF16),
        scratch_shapes=[pltpu.VMEM((SUBLANES + tc, qk2), F32),
                        pltpu.VMEM((heads, dk, dv + LANES), F32),
                        pltpu.VMEM((SUBLANES, LANES), F32),
                        pltpu.VMEM((2, tc, LANES), F32),
                        pltpu.VMEM((2, SUBLANES, tc), F32),
                        pltpu.VMEM((2, SUBLANES, tc), F32)],
        compiler_params=_params(("arbitrary", "arbitrary")),
        name="mlstm",
    )(proj, proj, proj, proj, small, small, conv_w, conv_b.reshape(1, qk2), norm_g.reshape(1, mw))


def _merge_kernel(oa_ref, hm_ref, u_ref, h_ref, gt_ref, wa_ref, wm_ref, wga_ref, wgm_ref, wo_ref, o_ref, *, rows):
    j = pl.program_id(1)
    nj = pl.num_programs(1)
    tm = h_ref.shape[0]

    def step(first, last):
        for r in range(tm // rows):
            sl = slice(r * rows, (r + 1) * rows)
            u = u_ref[sl, :]
            y_att = _dot(oa_ref[sl, :], wa_ref[...])
            y_m = _dot(hm_ref[sl, :], wm_ref[...])
            g_att = jax.nn.sigmoid(_dot(u, wga_ref[...]))
            g_m = jax.nn.sigmoid(_dot(u, wgm_ref[...]))
            acc = _dot((g_att * y_att + g_m * y_m).astype(BF16), wo_ref[...])
            if not first:
                acc = o_ref[sl, :] + acc
            if last:
                acc = h_ref[sl, :] + gt_ref[0] * acc
            o_ref[sl, :] = acc

    pl.when(j == 0)(lambda: step(True, False))
    pl.when((j > 0) & (j < nj - 1))(lambda: step(False, False))
    pl.when(j == nj - 1)(lambda: step(False, True))


def _merge(o_att, hm, u, h, seq, mod3, mod_idx, w_att_up, w_mlstm_up, w_gates, gates_col, w_out, tm=512, tn=512,
           rows=256):
    n, d = h.shape
    aw = o_att.shape[1]
    mw = hm.shape[1]
    nj = d // tn
    assert nj >= 2 and gates_col % tn == 0
    g0 = gates_col // tn
    tiles_per_seq = seq // tm
    return pl.pallas_call(
        functools.partial(_merge_kernel, rows=rows),
        grid=(n // tm, nj),
        in_specs=[pl.BlockSpec((tm, aw), lambda i, j: (i, 0)),
                  pl.BlockSpec((tm, mw), lambda i, j: (i, 0)),
                  pl.BlockSpec((tm, d), lambda i, j: (i, 0)),
                  pl.BlockSpec((tm, d), lambda i, j: (i, 0)),
                  pl.BlockSpec((1, 1, d), lambda i, j: ((i // tiles_per_seq) * N_MOD + mod_idx, 0, 0)),
                  pl.BlockSpec((aw, tn), lambda i, j: (0, j)),
                  pl.BlockSpec((mw, tn), lambda i, j: (0, j)),
                  pl.BlockSpec((d, tn), lambda i, j: (0, g0 + j)),
                  pl.BlockSpec((d, tn), lambda i, j: (0, g0 + nj + j)),
                  pl.BlockSpec((tn, d), lambda i, j: (j, 0))],
        out_specs=pl.BlockSpec((tm, d), lambda i, j: (i, 0)),
        out_shape=jax.ShapeDtypeStruct((n, d), F32),
        compiler_params=_params(("arbitrary", "arbitrary")),
        name="merge",
    )(o_att, hm, u, h, mod3, w_att_up, w_mlstm_up, w_gates, w_gates, w_out)


def _repack_kernel(wt_ref, o_ref, *, valid_of_step):
    x = wt_ref[...].T
    valid = valid_of_step(pl.program_id(0))
    lane = lax.broadcasted_iota(jnp.int32, (1, x.shape[1]), 1)
    o_ref[...] = jnp.where(lane < valid, x, 0.0).astype(BF16)


def _repack(w_t, pieces, tb=256):
    f, d = w_t.shape
    starts, valids = [], []
    for src, width, valid in pieces:
        assert width % tb == 0 and src % SUBLANES == 0 and src + width <= f
        for b in range(width // tb):
            starts.append(src + b * tb)
            valids.append(min(max(valid - b * tb, 0), tb))

    def lookup(table, i):
        out = jnp.int32(table[0])
        for s, v in enumerate(table[1:], 1):
            out = jnp.where(i == s, jnp.int32(v), out)
        return out

    return pl.pallas_call(
        functools.partial(_repack_kernel, valid_of_step=functools.partial(lookup, valids)),
        grid=(len(starts),),
        in_specs=[pl.BlockSpec((pl.Element(tb), pl.Element(d)),
                               lambda i: (pl.multiple_of(lookup(starts, i), SUBLANES), 0))],
        out_specs=pl.BlockSpec((d, tb), lambda i: (0, i)),
        out_shape=jax.ShapeDtypeStruct((d, len(starts) * tb), BF16),
        compiler_params=_params(("arbitrary",)),
        name="repack",
    )(w_t)


def kernel(x, c, positions, w_ada, b_ada, norm1_g, ffn1_w_in, ffn1_w_out, norm2_g, w_in, conv_w, conv_b,
           b_igate, b_fgate, sinks, mlstm_norm_g, w_att_up, w_mlstm_up, w_out, norm3_g, ffn2_w_in,
           ffn2_w_out, final_g):
    bsz, seq, d = x.shape
    depth = w_ada.shape[0]
    n = bsz * seq
    aw = ATT_HEADS * ATT_HEAD_DIM
    kvw = ATT_KV_HEADS * ATT_HEAD_DIM
    qk2 = 2 * M_HEADS * M_QK_DIM
    mw = M_HEADS * M_V_DIM

    sizes = (aw, kvw, kvw, qk2, mw, M_HEADS, M_HEADS, mw, 2 * d)
    offs = [0]
    for s in sizes:
        offs.append(offs[-1] + s)
    (o_q, o_k, o_v, o_qk, o_mv, o_i, o_f, o_mo, o_g) = offs[:-1]
    q_col, qk_col, mv_col, mo_col = 0, aw, aw + qk2, aw + qk2 + mw
    k_col, v_col, if_col = 0, kvw, 2 * kvw

    half = ROPE_DIM // 2
    inv_freq = jnp.power(ROPE_THETA, -jnp.arange(half, dtype=F32) * 2.0 / ROPE_DIM)
    inv_freq = jnp.broadcast_to(inv_freq[:, None], (half, WINDOW))
    pos3 = positions.reshape(n // WINDOW, 1, WINDOW)

    h = x.reshape(n, d)
    for l in range(depth):
        mod3 = _ada(c, w_ada[l], b_ada[l]).reshape(bsz * N_MOD, 1, d)

        h, (ffn2_in_b, ffn2_out_b) = _ffn(
            h, seq, norm1_g[l], mod3, 0, ffn1_w_in[l].astype(BF16), ffn1_w_out[l].astype(BF16),
            casts=(ffn2_w_in[l], ffn2_w_out[l]))

        n_main, n_small = aw + qk2 + 2 * mw, 2 * (2 * kvw)
        w_all = _repack(jnp.swapaxes(w_in[l], 0, 1), (
            (o_q, aw, aw), (o_qk, qk2, qk2), (o_mv, mw, mw), (o_mo, mw, mw),
            (o_g, 2 * d, 2 * d),
            (o_k, 2 * kvw, 2 * kvw), (o_i, 2 * kvw, 2 * M_HEADS)))
        g_col, s_col = n_main, n_main + 2 * d
        b_small = jnp.concatenate(
            [jnp.zeros((2 * kvw,), F32), b_igate[l], b_fgate[l], jnp.zeros((2 * kvw - 2 * M_HEADS,), F32)])[None, :]
        proj, small, u, (att_up_b, mlstm_up_b, out_b) = _proj(
            h, seq, norm2_g[l], mod3, 3, w_all, n_main, s_col, b_small, casts=(w_att_up[l], w_mlstm_up[l], w_out[l]))

        o_att = _attn(proj, small, pos3, inv_freq, sinks[l], bsz, seq, q_col, k_col, v_col)
        hm = _mlstm(proj, small, conv_w[l], conv_b[l], mlstm_norm_g[l], bsz, seq, qk_col, mv_col, mo_col, if_col)
        h = _merge(o_att, hm, u, h, seq, mod3, 5, att_up_b, mlstm_up_b, w_all, g_col, out_b)

        last = l == depth - 1
        h, _ = _ffn(h, seq, norm3_g[l], mod3, 6, ffn2_in_b, ffn2_out_b, final_g=final_g if last else None)
    return h.reshape(bsz, seq, d)
```

```python
import functools

import jax
import jax.numpy as jnp
from jax import lax
from jax.experimental import pallas as pl
from jax.experimental.pallas import tpu as pltpu

F32 = jnp.float32
BF16 = jnp.bfloat16

ATT_HEADS = 16
ATT_KV_HEADS = 2
ATT_HEAD_DIM = 64
WINDOW = 128
ROPE_DIM = ATT_HEAD_DIM // 4
ROPE_THETA = 500000.0
M_HEADS = 4
M_QK_DIM = 128
M_V_DIM = 256
M_CHUNK = 64
CONV_WIDTH = 4
NORM_EPS = 1e-6
NEG_INF = -1e30
N_MOD = 9

LANES = 128
SUBLANES = 8
VMEM_LIMIT = 60 * 1024 * 1024

ADA_K = 256
FFN_TM, FFN_TF = 1024, 512
PROJ_TM, PROJ_TN = 1024, 1024
MERGE_TM, MERGE_TN = 512, 512
MLSTM_T = 1024
REPACK_TB = 256
FFN_CHUNK, PROJ_CHUNK = 512, 256


def _dot(a, b):
    return jnp.dot(a, b, preferred_element_type=F32)


def _dot_nt(a, b):
    return lax.dot_general(a, b, (((1,), (1,)), ((), ())), preferred_element_type=F32)


def _dot_tn(a, b):
    return lax.dot_general(a, b, (((0,), (0,)), ((), ())), preferred_element_type=F32)


def _split3(x):
    hi = x.astype(BF16)
    r1 = x - hi.astype(F32)
    mid = r1.astype(BF16)
    lo = (r1 - mid.astype(F32)).astype(BF16)
    return hi, mid, lo


def _rms(x):
    return x * lax.rsqrt(jnp.mean(x * x, axis=-1, keepdims=True) + NORM_EPS)


def _params(sem):
    return pltpu.CompilerParams(dimension_semantics=sem, vmem_limit_bytes=VMEM_LIMIT)


def _ada_kernel(c_ref, b_ref, w_ref, o_ref):
    @pl.when(pl.program_id(0) == 0)
    def _():
        o_ref[...] = jnp.broadcast_to(b_ref[...], o_ref.shape)

    c = c_ref[...]
    hi, mid, lo = _split3(c * jax.nn.sigmoid(c))
    w = w_ref[...].astype(BF16)
    o_ref[...] += _dot(hi, w) + _dot(mid, w) + _dot(lo, w)


def _ada(c, w_ada, b_ada, tk=ADA_K):
    bsz, d = c.shape
    n = w_ada.shape[1]
    cp = jnp.zeros((SUBLANES, d), F32).at[:bsz].set(c)
    out = pl.pallas_call(
        _ada_kernel,
        grid=(d // tk,),
        in_specs=[pl.BlockSpec((SUBLANES, tk), lambda k: (0, k)),
                  pl.BlockSpec((1, n), lambda k: (0, 0)),
                  pl.BlockSpec((tk, n), lambda k: (k, 0))],
        out_specs=pl.BlockSpec((SUBLANES, n), lambda k: (0, 0)),
        out_shape=jax.ShapeDtypeStruct((SUBLANES, n), F32),
        compiler_params=_params(("arbitrary",)),
        name="ada",
    )(cp, b_ada.reshape(1, n), w_ada)
    return out[:bsz]


def _ffn_kernel(x_ref, g_ref, sh_ref, sc_ref, gt_ref, wa_ref, wb_ref, wo_ref, *rest, final_norm, rows, n_cast):
    rest = list(rest)
    fg_ref = rest.pop(0) if final_norm else None
    cast_in, o_ref, cast_out, u_ref = rest[:n_cast], rest[n_cast], rest[n_cast + 1:2 * n_cast + 1], rest[-1]
    j = pl.program_id(1)
    nj = pl.num_programs(1)
    tm = x_ref.shape[0]

    for src, dst in zip(cast_in, cast_out):
        dst[...] = src[...].astype(BF16)

    def step(first, last):
        chunk = rows if (first or last) else tm
        for r in range(tm // chunk):
            sl = slice(r * chunk, (r + 1) * chunk)
            if first:
                y = _rms(x_ref[sl, :]) * g_ref[...]
                u = (y * (1.0 + sc_ref[0]) + sh_ref[0]).astype(BF16)
                u_ref[sl, :] = u
            else:
                u = u_ref[sl, :]
            a = _dot(u, wa_ref[...])
            b = _dot(u, wb_ref[...])
            acc = _dot((a * jax.nn.sigmoid(a) * b).astype(BF16), wo_ref[...])
            if not first:
                acc = o_ref[sl, :] + acc
            if last:
                acc = x_ref[sl, :] + (0.5 * gt_ref[0]) * acc
                if final_norm:
                    acc = _rms(acc) * fg_ref[...]
            o_ref[sl, :] = acc

    pl.when(j == 0)(lambda: step(True, False))
    pl.when((j > 0) & (j < nj - 1))(lambda: step(False, False))
    pl.when(j == nj - 1)(lambda: step(False, True))


def _cast_spec(shape, n_i, n_j):
    r, c = shape
    sub, lanes = 2 * SUBLANES, LANES
    if r % n_i == 0 and c % n_j == 0 and (r // n_i) % sub == 0 and (c // n_j) % lanes == 0:
        return pl.BlockSpec((r // n_i, c // n_j), lambda i, j: (i, j))
    if r % n_j == 0 and c % n_i == 0 and (r // n_j) % sub == 0 and (c // n_i) % lanes == 0:
        return pl.BlockSpec((r // n_j, c // n_i), lambda i, j: (j, i))
    assert r % n_i == 0 and (r // n_i) % sub == 0, shape
    return pl.BlockSpec((r // n_i, c), lambda i, j: (i, 0))


def _ffn(h, seq, norm_g, mod3, mod_base, w_in, w_out, final_g=None, casts=(), tm=FFN_TM, tf=FFN_TF,
         rows=FFN_CHUNK):
    n, d = h.shape
    dff = w_out.shape[0]
    nj = dff // tf
    assert nj >= 2
    tiles_per_seq = seq // tm

    def mod_spec(k):
        return pl.BlockSpec((1, 1, d), lambda i, j: ((i // tiles_per_seq) * N_MOD + mod_base + k, 0, 0))

    in_specs = [pl.BlockSpec((tm, d), lambda i, j: (i, 0)),
                pl.BlockSpec((1, d), lambda i, j: (0, 0)),
                mod_spec(0), mod_spec(1), mod_spec(2),
                pl.BlockSpec((d, tf), lambda i, j: (0, j)),
                pl.BlockSpec((d, tf), lambda i, j: (0, j + nj)),
                pl.BlockSpec((tf, d), lambda i, j: (j, 0))]
    args = [h, norm_g.reshape(1, d), mod3, mod3, mod3, w_in, w_in, w_out]
    if final_g is not None:
        in_specs.append(pl.BlockSpec((1, d), lambda i, j: (0, 0)))
        args.append(final_g.reshape(1, d))
    cast_specs = [_cast_spec(w.shape, n // tm, nj) for w in casts]
    outs = pl.pallas_call(
        functools.partial(_ffn_kernel, final_norm=final_g is not None, rows=rows, n_cast=len(casts)),
        grid=(n // tm, nj),
        in_specs=in_specs + cast_specs,
        out_specs=[pl.BlockSpec((tm, d), lambda i, j: (i, 0))] + [_cast_spec(w.shape, n // tm, nj) for w in casts],
        out_shape=[jax.ShapeDtypeStruct((n, d), F32)] + [jax.ShapeDtypeStruct(w.shape, BF16) for w in casts],
        scratch_shapes=[pltpu.VMEM((tm, d), BF16)],
        compiler_params=_params(("arbitrary", "arbitrary")),
        name="ffn_final" if final_g is not None else "ffn",
    )(*args, *casts)
    return outs[0], tuple(outs[1:])


def _proj_kernel(x_ref, g_ref, sh_ref, sc_ref, w_ref, ws_ref, bs_ref, *rest, rows, n_cast):
    cast_in, (p_ref, s_ref, u_ref), cast_out = rest[:n_cast], rest[n_cast:n_cast + 3], rest[n_cast + 3:]
    j = pl.program_id(1)
    tm = x_ref.shape[0]

    for src, dst in zip(cast_in, cast_out):
        dst[...] = src[...].astype(BF16)

    @pl.when(j == 0)
    def _():
        for r in range(tm // rows):
            sl = slice(r * rows, (r + 1) * rows)
            y = _rms(x_ref[sl, :]) * g_ref[...]
            u = (y * (1.0 + sc_ref[0]) + sh_ref[0]).astype(BF16)
            u_ref[sl, :] = u
            s_ref[sl, :] = _dot(u, ws_ref[...]) + bs_ref[...]
            p_ref[sl, :] = _dot(u, w_ref[...])

    @pl.when(j > 0)
    def _():
        p_ref[...] = _dot(u_ref[...], w_ref[...])


def _proj(h, seq, norm_g, mod3, mod_base, w_all, ncol, small_col, b_small, casts=(), tm=PROJ_TM, tn=PROJ_TN,
          rows=PROJ_CHUNK):
    n, d = h.shape
    nsmall = b_small.shape[1]
    assert small_col % nsmall == 0 and ncol % tn == 0
    tiles_per_seq = seq // tm
    n_i, n_j = n // tm, ncol // tn

    def mod_spec(k):
        return pl.BlockSpec((1, 1, d), lambda i, j: ((i // tiles_per_seq) * N_MOD + mod_base + k, 0, 0))

    outs = pl.pallas_call(
        functools.partial(_proj_kernel, rows=rows, n_cast=len(casts)),
        grid=(n_i, n_j),
        in_specs=[pl.BlockSpec((tm, d), lambda i, j: (i, 0)),
                  pl.BlockSpec((1, d), lambda i, j: (0, 0)),
                  mod_spec(0), mod_spec(1),
                  pl.BlockSpec((d, tn), lambda i, j: (0, j)),
                  pl.BlockSpec((d, nsmall), lambda i, j: (0, small_col // nsmall)),
                  pl.BlockSpec((1, nsmall), lambda i, j: (0, 0))]
        + [_cast_spec(w.shape, n_i, n_j) for w in casts],
        out_specs=[pl.BlockSpec((tm, tn), lambda i, j: (i, j)),
                   pl.BlockSpec((tm, nsmall), lambda i, j: (i, 0)),
                   pl.BlockSpec((tm, d), lambda i, j: (i, 0))]
        + [_cast_spec(w.shape, n_i, n_j) for w in casts],
        out_shape=[jax.ShapeDtypeStruct((n, ncol), F32),
                   jax.ShapeDtypeStruct((n, nsmall), F32),
                   jax.ShapeDtypeStruct((n, d), BF16)]
        + [jax.ShapeDtypeStruct(w.shape, BF16) for w in casts],
        compiler_params=_params(("arbitrary", "arbitrary")),
        name="proj",
    )(h, norm_g.reshape(1, d), mod3, mod3, w_all, w_all, b_small, *casts)
    return outs[0], outs[1], outs[2], tuple(outs[3:])


def _attn_kernel(q_ref, kc_ref, vc_ref, kp_ref, vp_ref, posc_ref, posp_ref, invf_ref, sink_ref, o_ref, *, nsub):
    blk = WINDOW
    lane = lax.broadcasted_iota(jnp.int32, (1, LANES), 1)
    hl = lane % ATT_HEAD_DIM
    half = ROPE_DIM // 2
    lo_half = lane < ATT_HEAD_DIM

    def rope_tables(pos):
        pos = pos.astype(F32)
        ang = invf_ref[...] * pos
        reps = LANES // half
        ct = jnp.tile(jnp.cos(ang), (reps, 1)).T
        st = jnp.tile(jnp.sin(ang), (reps, 1)).T
        c = jnp.where(hl < ROPE_DIM, ct, 1.0)
        s = jnp.where(hl < ROPE_DIM, st, 0.0)
        return c, s

    def swap(x):
        return jnp.where(hl < half, pltpu.roll(x, LANES - half, 1),
                         jnp.where(hl < ROPE_DIM, -pltpu.roll(x, half, 1), 0.0))

    def rope(x, c, s):
        return x * c - swap(x) * s

    tabs = [rope_tables(posp_ref[0])] + [rope_tables(posc_ref[t]) for t in range(nsub)]
    k = jnp.concatenate([rope(kp_ref[...], *tabs[0])]
                        + [rope(kc_ref[t * blk:(t + 1) * blk, :], *tabs[t + 1]) for t in range(nsub)], axis=0)
    v = jnp.concatenate([vp_ref[...], vc_ref[...]], axis=0)
    ks = swap(k)
    k_sw, ks_sw = pltpu.roll(k, ATT_HEAD_DIM, 1), pltpu.roll(ks, ATT_HEAD_DIM, 1)
    v_sw = pltpu.roll(v, ATT_HEAD_DIM, 1)

    qi = lax.broadcasted_iota(jnp.int32, (blk, 2 * blk), 0) + blk
    kj = lax.broadcasted_iota(jnp.int32, (blk, 2 * blk), 1)
    band = (kj <= qi) & (kj > qi - WINDOW)
    slot0 = kj[0:1, :] == 0
    not_slot0 = lax.broadcasted_iota(jnp.int32, (2 * blk, 1), 0) > 0
    k_min = jnp.where(pl.program_id(1) == 0, blk, 0)
    band_first = band & (kj >= k_min)

    scale = ATT_HEAD_DIM ** -0.5
    pairs_per_group = ATT_HEADS // ATT_KV_HEADS // 2
    ones = jnp.ones((2 * blk, LANES), F32)
    one_rhs = jnp.concatenate([jnp.where(lo_half, ones, 0.0), jnp.where(lo_half, 0.0, ones)], axis=0)
    for g in range(ATT_KV_HEADS):
        k_g, k_o = (k, k_sw) if g == 0 else (k_sw, k)
        ks_g, ks_o = (ks, ks_sw) if g == 0 else (ks_sw, ks)
        v_g, v_o = (v, v_sw) if g == 0 else (v_sw, v)
        k_lo = jnp.concatenate([jnp.where(lo_half, k_g, 0.0), jnp.where(lo_half, ks_g, 0.0)], axis=1).astype(BF16)
        k_hi = jnp.concatenate([jnp.where(lo_half, 0.0, k_o), jnp.where(lo_half, 0.0, ks_o)], axis=1).astype(BF16)
        v_lo, v_hi = jnp.where(lo_half, v_g, 0.0), jnp.where(lo_half, 0.0, v_o)
        for t in range(nsub):
            rows = slice(t * blk, (t + 1) * blk)
            keys = slice(t * blk, (t + 2) * blk)
            mask = band_first if t == 0 else band
            k_rhs = jnp.concatenate([k_lo[keys], k_hi[keys]], axis=0)
            c_q, s_q = tabs[t + 1][0] * scale, tabs[t + 1][1] * scale
            v_rhs = jnp.concatenate(
                [jnp.concatenate([jnp.where(not_slot0, v_lo[keys], 0.0), jnp.where(not_slot0, v_hi[keys], 0.0)], axis=0),
                 one_rhs], axis=1).astype(BF16)
            q_rows = []
            for pp in range(pairs_per_group):
                p = g * pairs_per_group + pp
                q_p = q_ref[rows, p * LANES:(p + 1) * LANES]
                q_rows.append(jnp.concatenate([q_p * c_q, q_p * s_q], axis=1).astype(BF16))
            s_all = _dot_nt(jnp.concatenate(q_rows, axis=0), k_rhs)
            p_rows = []
            for pp in range(pairs_per_group):
                p = g * pairs_per_group + pp
                halves = []
                for e in range(2):
                    s = s_all[pp * blk:(pp + 1) * blk, e * 2 * blk:(e + 1) * 2 * blk]
                    s = jnp.where(mask, s, jnp.where(slot0, sink_ref[2 * p + e], NEG_INF))
                    m = jnp.max(s, axis=-1, keepdims=True)
                    halves.append(jnp.exp(s - m).astype(BF16))
                p_rows.append(jnp.concatenate(halves, axis=1))
            o_all = _dot(jnp.concatenate(p_rows, axis=0), v_rhs)
            o_all = o_all[:, :LANES] * (1.0 / o_all[:, LANES:])
            for pp in range(pairs_per_group):
                p = g * pairs_per_group + pp
                o_ref[rows, p * LANES:(p + 1) * LANES] = o_all[pp * blk:(pp + 1) * blk].astype(BF16)


def _attn(q_src, kv_src, pos3, inv_freq, sinks, bsz, seq, q_col, k_col, v_col, nsub=1):
    blk = WINDOW
    aw = ATT_HEADS * ATT_HEAD_DIM
    ns = seq // (nsub * blk)
    n = bsz * seq
    half = ROPE_DIM // 2

    def cur(b, i):
        return b * ns + i

    def prv(b, i):
        return b * ns * nsub + jnp.maximum(i * nsub - 1, 0)

    return pl.pallas_call(
        functools.partial(_attn_kernel, nsub=nsub),
        grid=(bsz, ns),
        in_specs=[pl.BlockSpec((nsub * blk, aw), lambda b, i: (cur(b, i), q_col // aw)),
                  pl.BlockSpec((nsub * blk, LANES), lambda b, i: (cur(b, i), k_col // LANES)),
                  pl.BlockSpec((nsub * blk, LANES), lambda b, i: (cur(b, i), v_col // LANES)),
                  pl.BlockSpec((blk, LANES), lambda b, i: (prv(b, i), k_col // LANES)),
                  pl.BlockSpec((blk, LANES), lambda b, i: (prv(b, i), v_col // LANES)),
                  pl.BlockSpec((nsub, 1, blk), lambda b, i: (cur(b, i), 0, 0)),
                  pl.BlockSpec((1, 1, blk), lambda b, i: (prv(b, i), 0, 0)),
                  pl.BlockSpec((half, blk), lambda b, i: (0, 0)),
                  pl.BlockSpec(memory_space=pltpu.SMEM)],
        out_specs=pl.BlockSpec((nsub * blk, aw), lambda b, i: (cur(b, i), 0)),
        out_shape=jax.ShapeDtypeStruct((n, aw), BF16),
        compiler_params=_params(("arbitrary", "arbitrary")),
        name="attn",
    )(q_src, kv_src, kv_src, kv_src, kv_src, pos3, pos3, inv_freq, sinks)


def _log_sigmoid(x):
    return jnp.minimum(x, 0.0) - jnp.log1p(jnp.exp(-jnp.abs(x)))


def _mlstm_kernel(xqk_ref, xprev_ref, v_ref, mo_ref, if_ref, ifn_ref, cw_ref, cb_ref, ng_ref, o_ref,
                  xbuf_ref, c_ref, m_ref, cols_ref, rt_ref, dec_ref, *, tc):
    heads, dk, dv, ch = M_HEADS, M_QK_DIM, M_V_DIM, M_CHUNK
    qkw = heads * dk
    nc = tc // ch
    assert ch * 2 == LANES and nc % 2 == 0 and 2 * heads <= SUBLANES
    t_idx = pl.program_id(1)

    lane = lax.broadcasted_iota(jnp.int32, (1, LANES), 1)
    nblk = tc // LANES
    slot = t_idx % 2

    def gate_prep(g_ref, dst):
        gates = g_ref[...]
        logf = _log_sigmoid(gates)
        row = lax.broadcasted_iota(jnp.int32, (tc, tc), 0)
        col = lax.broadcasted_iota(jnp.int32, (tc, tc), 1)
        tri = jnp.where((row // ch == col // ch) & (col <= row), 1.0, 0.0).astype(BF16)
        hi, mid, lo = _split3(logf)
        bcum = _dot(tri, hi) + _dot(tri, mid) + _dot(tri, lo)
        packed_t = jnp.where(lane < heads, gates, bcum).T

        gt8 = packed_t[0:SUBLANES, :]
        bc_t = pltpu.roll(gt8, heads, 0)
        r_t = gt8 - bc_t
        pos = lax.broadcasted_iota(jnp.int32, (1, tc), 1) % ch

        def lane_blocks(a, f):
            return jnp.concatenate([f(a[:, j * LANES:(j + 1) * LANES]) for j in range(nblk)], axis=1)

        rmax_t = r_t
        for sh in (1, 2, 4, 8, 16, 32):
            back = lane_blocks(rmax_t, lambda blk, sh=sh: pltpu.roll(blk, sh, 1))
            rmax_t = jnp.maximum(rmax_t, jnp.where(pos >= sh, back, NEG_INF))

        last = [c * ch + ch - 1 for c in range(nc)]
        b_last = [bc_t[:, i:i + 1] for i in last]
        g_max = [bc_t[:, i:i + 1] + rmax_t[:, i:i + 1] for i in last]
        m_state = m_ref[:, 0:1]
        m_in, m_out = [], []
        for c in range(nc):
            m_in.append(m_state)
            m_state = jnp.maximum(b_last[c] + m_state, g_max[c])
            m_out.append(m_state)
        m_ref[...] = jnp.broadcast_to(m_state, m_ref.shape)
        decay = [jnp.exp(b_last[c] + m_in[c] - m_out[c]) for c in range(nc)]

        def per_time(cols):
            return jnp.concatenate(
                [jnp.where(lane < ch, cols[2 * j], cols[2 * j + 1]) for j in range(nblk)], axis=1)

        m_in_t, m_out_t, b_last_t = per_time(m_in), per_time(m_out), per_time(b_last)
        w_t = jnp.exp(b_last_t + r_t - m_out_t)
        a_log_t = bc_t + m_in_t
        m_t_t = jnp.maximum(a_log_t, bc_t + rmax_t)
        inter_t = jnp.exp(a_log_t - m_t_t)
        eneg_t = jnp.exp(-m_t_t)
        sub = lax.broadcasted_iota(jnp.int32, (SUBLANES, 1), 0)
        cols_t = jnp.concatenate(
            [jnp.where(sub < heads, w_t, pltpu.roll(inter_t, heads, 0)),
             jnp.where(sub < heads, eneg_t, pltpu.roll(m_t_t, heads, 0)),
             bc_t,
             jnp.zeros((LANES - 3 * SUBLANES, tc), F32)], axis=0)
        cols_ref[dst] = cols_t.T
        rt_ref[dst] = r_t
        dec_ref[dst] = per_time(decay)

    @pl.when(t_idx == 0)
    def _():
        c_ref[...] = jnp.zeros_like(c_ref)
        m_ref[...] = jnp.full_like(m_ref, NEG_INF)
        gate_prep(if_ref, 0)

    cols = cols_ref[slot]
    r_t = rt_ref[slot]
    dec_t = dec_ref[slot]
    gate_prep(ifn_ref, 1 - slot)

    pad = SUBLANES
    xbuf_ref[0:pad, :] = jnp.where(t_idx > 0, xprev_ref[...], 0.0)
    xbuf_ref[pad:pad + tc, :] = xqk_ref[...]
    y = cb_ref[...]
    for j in range(CONV_WIDTH):
        off = pad - (CONV_WIDTH - 1) + j
        y = y + cw_ref[j:j + 1, :] * xbuf_ref[off:off + tc, :]
    y = y * jax.nn.sigmoid(y)
    q_all = y[:, :qkw].astype(BF16)
    k_all = y[:, qkw:] * (dk ** -0.5)
    k_all_b = k_all.astype(BF16)

    causal = (lax.broadcasted_iota(jnp.int32, (ch, ch), 1) <= lax.broadcasted_iota(jnp.int32, (ch, ch), 0))
    ng = ng_ref[...]
    ones_v = jnp.ones((tc, LANES), BF16)
    ones_sq = jnp.ones((dv, LANES), BF16)

    def chunks(a):
        return a.reshape(nc, ch, a.shape[-1])

    for h in range(heads):
        w_col = cols[:, h:h + 1]
        inter = cols[:, heads + h:heads + h + 1]
        eneg = cols[:, 2 * heads + h:2 * heads + h + 1]
        m_t = cols[:, 3 * heads + h:3 * heads + h + 1]
        bc = cols[:, 4 * heads + h:4 * heads + h + 1]
        r_row = jnp.stack([r_t[h:h + 1, c * ch:(c + 1) * ch] for c in range(nc)])
        q = chunks(q_all[:, h * dk:(h + 1) * dk])
        kb = chunks(k_all_b[:, h * dk:(h + 1) * dk])
        v = chunks(jnp.concatenate([v_ref[:, h * dv:(h + 1) * dv].astype(BF16), ones_v], axis=1))
        wkb = chunks((w_col * k_all[:, h * dk:(h + 1) * dk]).astype(BF16))

        c_state = c_ref[h]
        c_in = []
        for c in range(nc):
            c_in.append(c_state.astype(BF16))
            c_state = dec_t[h:h + 1, c * ch:c * ch + 1] * c_state + _dot_tn(wkb[c], v[c])
        c_ref[h] = c_state

        d = jnp.where(causal, chunks(bc) + r_row, NEG_INF)
        qk = jnp.stack([_dot_nt(q[c], kb[c]) for c in range(nc)]) * jnp.exp(d - chunks(m_t))
        qkb = qk.astype(BF16)
        num = chunks(inter) * jnp.stack([_dot(q[c], c_in[c]) for c in range(nc)]) \
            + jnp.stack([_dot(qkb[c], v[c]) for c in range(nc)])
        num = num.reshape(tc, dv + LANES)
        inv = 1.0 / jnp.maximum(jnp.abs(num[:, dv:]), eneg)
        halves = [num[:, i * LANES:(i + 1) * LANES] * inv for i in range(dv // LANES)]
        sq = jnp.concatenate([x * x for x in halves], axis=1).astype(BF16)
        rs = lax.rsqrt(_dot(sq, ones_sq) * (1.0 / dv) + NORM_EPS)
        for i, x in enumerate(halves):
            sl = slice(h * dv + i * LANES, h * dv + (i + 1) * LANES)
            o_ref[:, sl] = (x * rs * ng[:, sl] * jax.nn.sigmoid(mo_ref[:, sl])).astype(BF16)


def _mlstm(proj, small, conv_w, conv_b, norm_g, bsz, seq, qk_col, v_col, o_col, if_col, tc=MLSTM_T):
    heads, dk, dv = M_HEADS, M_QK_DIM, M_V_DIM
    qk2 = 2 * heads * dk
    mw = heads * dv
    nt = seq // tc
    n = bsz * seq

    def rows(b, t):
        return b * nt + t

    def prev_rows(b, t):
        return jnp.maximum((b * nt + t) * (tc // SUBLANES) - 1, 0)

    return pl.pallas_call(
        functools.partial(_mlstm_kernel, tc=tc),
        grid=(bsz, nt),
        in_specs=[pl.BlockSpec((tc, qk2), lambda b, t: (rows(b, t), qk_col // qk2)),
                  pl.BlockSpec((SUBLANES, qk2), lambda b, t: (prev_rows(b, t), qk_col // qk2)),
                  pl.BlockSpec((tc, mw), lambda b, t: (rows(b, t), v_col // mw)),
                  pl.BlockSpec((tc, mw), lambda b, t: (rows(b, t), o_col // mw)),
                  pl.BlockSpec((tc, LANES), lambda b, t: (rows(b, t), if_col // LANES)),
                  pl.BlockSpec((tc, LANES), lambda b, t: (rows(b, jnp.minimum(t + 1, nt - 1)), if_col // LANES)),
                  pl.BlockSpec((CONV_WIDTH, qk2), lambda b, t: (0, 0)),
                  pl.BlockSpec((1, qk2), lambda b, t: (0, 0)),
                  pl.BlockSpec((1, mw), lambda b, t: (0, 0))],
        out_specs=pl.BlockSpec((tc, mw), lambda b, t: (rows(b, t), 0)),
        out_shape=jax.ShapeDtypeStruct((n, mw), BF16),
        scratch_shapes=[pltpu.VMEM((SUBLANES + tc, qk2), F32),
                        pltpu.VMEM((heads, dk, dv + LANES), F32),
                        pltpu.VMEM((SUBLANES, LANES), F32),
                        pltpu.VMEM((2, tc, LANES), F32),
                        pltpu.VMEM((2, SUBLANES, tc), F32),
                        pltpu.VMEM((2, SUBLANES, tc), F32)],
        compiler_params=_params(("arbitrary", "arbitrary")),
        name="mlstm",
    )(proj, proj, proj, proj, small, small, conv_w, conv_b.reshape(1, qk2), norm_g.reshape(1, mw))


def _merge_kernel(oa_ref, hm_ref, u_ref, h_ref, gt_ref, wa_ref, wm_ref, wga_ref, wgm_ref, wo_ref, o_ref, *, rows):
    j = pl.program_id(1)
    nj = pl.num_programs(1)
    tm = h_ref.shape[0]

    def step(first, last):
        for r in range(tm // rows):
            sl = slice(r * rows, (r + 1) * rows)
            u = u_ref[sl, :]
            y_att = _dot(oa_ref[sl, :], wa_ref[...])
            y_m = _dot(hm_ref[sl, :], wm_ref[...])
            g_att = jax.nn.sigmoid(_dot(u, wga_ref[...]))
            g_m = jax.nn.sigmoid(_dot(u, wgm_ref[...]))
            acc = _dot((g_att * y_att + g_m * y_m).astype(BF16), wo_ref[...])
            if not first:
                acc = o_ref[sl, :] + acc
            if last:
                acc = h_ref[sl, :] + gt_ref[0] * acc
            o_ref[sl, :] = acc

    pl.when(j == 0)(lambda: step(True, False))
    pl.when((j > 0) & (j < nj - 1))(lambda: step(False, False))
    pl.when(j == nj - 1)(lambda: step(False, True))


def _merge(o_att, hm, u, h, seq, mod3, mod_idx, w_att_up, w_mlstm_up, w_gates, gates_col, w_out, tm=MERGE_TM,
           tn=MERGE_TN, rows=MERGE_TM):
    n, d = h.shape
    aw = o_att.shape[1]
    mw = hm.shape[1]
    nj = d // tn
    assert nj >= 2 and gates_col % tn == 0
    g0 = gates_col // tn
    tiles_per_seq = seq // tm
    return pl.pallas_call(
        functools.partial(_merge_kernel, rows=rows),
        grid=(n // tm, nj),
        in_specs=[pl.BlockSpec((tm, aw), lambda i, j: (i, 0)),
                  pl.BlockSpec((tm, mw), lambda i, j: (i, 0)),
                  pl.BlockSpec((tm, d), lambda i, j: (i, 0)),
                  pl.BlockSpec((tm, d), lambda i, j: (i, 0)),
                  pl.BlockSpec((1, 1, d), lambda i, j: ((i // tiles_per_seq) * N_MOD + mod_idx, 0, 0)),
                  pl.BlockSpec((aw, tn), lambda i, j: (0, j)),
                  pl.BlockSpec((mw, tn), lambda i, j: (0, j)),
                  pl.BlockSpec((d, tn), lambda i, j: (0, g0 + j)),
                  pl.BlockSpec((d, tn), lambda i, j: (0, g0 + nj + j)),
                  pl.BlockSpec((tn, d), lambda i, j: (j, 0))],
        out_specs=pl.BlockSpec((tm, d), lambda i, j: (i, 0)),
        out_shape=jax.ShapeDtypeStruct((n, d), F32),
        compiler_params=_params(("arbitrary", "arbitrary")),
        name="merge",
    )(o_att, hm, u, h, mod3, w_att_up, w_mlstm_up, w_gates, w_gates, w_out)


def _repack_kernel(wt_ref, o_ref, *, valid_of_step):
    x = wt_ref[...].T
    valid = valid_of_step(pl.program_id(0))
    lane = lax.broadcasted_iota(jnp.int32, (1, x.shape[1]), 1)
    o_ref[...] = jnp.where(lane < valid, x, 0.0).astype(BF16)


def _repack(w_t, pieces, tb=REPACK_TB):
    f, d = w_t.shape
    starts, valids = [], []
    for src, width, valid in pieces:
        assert width % tb == 0 and src % SUBLANES == 0 and src + width <= f
        for b in range(width // tb):
            starts.append(src + b * tb)
            valids.append(min(max(valid - b * tb, 0), tb))

    def lookup(table, i):
        out = jnp.int32(table[0])
        for s, v in enumerate(table[1:], 1):
            out = jnp.where(i == s, jnp.int32(v), out)
        return out

    return pl.pallas_call(
        functools.partial(_repack_kernel, valid_of_step=functools.partial(lookup, valids)),
        grid=(len(starts),),
        in_specs=[pl.BlockSpec((pl.Element(tb), pl.Element(d)),
                               lambda i: (pl.multiple_of(lookup(starts, i), SUBLANES), 0))],
        out_specs=pl.BlockSpec((d, tb), lambda i: (0, i)),
        out_shape=jax.ShapeDtypeStruct((d, len(starts) * tb), BF16),
        compiler_params=_params(("arbitrary",)),
        name="repack",
    )(w_t)


def kernel(x, c, positions, w_ada, b_ada, norm1_g, ffn1_w_in, ffn1_w_out, norm2_g, w_in, conv_w, conv_b,
           b_igate, b_fgate, sinks, mlstm_norm_g, w_att_up, w_mlstm_up, w_out, norm3_g, ffn2_w_in,
           ffn2_w_out, final_g):
    bsz, seq, d = x.shape
    depth = w_ada.shape[0]
    n = bsz * seq
    aw = ATT_HEADS * ATT_HEAD_DIM
    kvw = ATT_KV_HEADS * ATT_HEAD_DIM
    qk2 = 2 * M_HEADS * M_QK_DIM
    mw = M_HEADS * M_V_DIM

    sizes = (aw, kvw, kvw, qk2, mw, M_HEADS, M_HEADS, mw, 2 * d)
    offs = [0]
    for s in sizes:
        offs.append(offs[-1] + s)
    (o_q, o_k, o_v, o_qk, o_mv, o_i, o_f, o_mo, o_g) = offs[:-1]
    q_col, qk_col, mv_col, mo_col = 0, aw, aw + qk2, aw + qk2 + mw
    k_col, v_col, if_col = 0, kvw, 2 * kvw

    half = ROPE_DIM // 2
    inv_freq = jnp.power(ROPE_THETA, -jnp.arange(half, dtype=F32) * 2.0 / ROPE_DIM)
    inv_freq = jnp.broadcast_to(inv_freq[:, None], (half, WINDOW))
    pos3 = positions.reshape(n // WINDOW, 1, WINDOW)

    h = x.reshape(n, d)
    for l in range(depth):
        mod3 = _ada(c, w_ada[l], b_ada[l]).reshape(bsz * N_MOD, 1, d)

        h, (ffn2_in_b, ffn2_out_b) = _ffn(
            h, seq, norm1_g[l], mod3, 0, ffn1_w_in[l].astype(BF16), ffn1_w_out[l].astype(BF16),
            casts=(ffn2_w_in[l], ffn2_w_out[l]))

        n_main, n_small = aw + qk2 + 2 * mw, 2 * (2 * kvw)
        w_all = _repack(jnp.swapaxes(w_in[l], 0, 1), (
            (o_q, aw, aw), (o_qk, qk2, qk2), (o_mv, mw, mw), (o_mo, mw, mw),
            (o_g, 2 * d, 2 * d),
            (o_k, 2 * kvw, 2 * kvw), (o_i, 2 * kvw, 2 * M_HEADS)))
        g_col, s_col = n_main, n_main + 2 * d
        b_small = jnp.concatenate(
            [jnp.zeros((2 * kvw,), F32), b_igate[l], b_fgate[l], jnp.zeros((2 * kvw - 2 * M_HEADS,), F32)])[None, :]
        proj, small, u, (att_up_b, mlstm_up_b, out_b) = _proj(
            h, seq, norm2_g[l], mod3, 3, w_all, n_main, s_col, b_small, casts=(w_att_up[l], w_mlstm_up[l], w_out[l]))

        o_att = _attn(proj, small, pos3, inv_freq, sinks[l], bsz, seq, q_col, k_col, v_col)
        hm = _mlstm(proj, small, conv_w[l], conv_b[l], mlstm_norm_g[l], bsz, seq, qk_col, mv_col, mo_col, if_col)
        h = _merge(o_att, hm, u, h, seq, mod3, 5, att_up_b, mlstm_up_b, w_all, g_col, out_b)

        last = l == depth - 1
        h, _ = _ffn(h, seq, norm3_g[l], mod3, 6, ffn2_in_b, ffn2_out_b, final_g=final_g if last else None)
    return h.reshape(bsz, seq, d)
```

```python
import functools

import jax
import jax.numpy as jnp
from jax import lax
from jax.experimental import pallas as pl
from jax.experimental.pallas import tpu as pltpu

F32 = jnp.float32
BF16 = jnp.bfloat16

ATT_HEADS = 16
ATT_KV_HEADS = 2
ATT_HEAD_DIM = 64
WINDOW = 128
ROPE_DIM = ATT_HEAD_DIM // 4
ROPE_THETA = 500000.0
M_HEADS = 4
M_QK_DIM = 128
M_V_DIM = 256
M_CHUNK = 64
CONV_WIDTH = 4
NORM_EPS = 1e-6
NEG_INF = -1e30
N_MOD = 9

LANES = 128
SUBLANES = 8
VMEM_LIMIT = 60 * 1024 * 1024

ADA_K = 256
FFN_TM, FFN_TF = 1024, 512
PROJ_TM, PROJ_TN = 1024, 1024
MERGE_TM, MERGE_TN = 512, 512
MLSTM_T = 512
REPACK_TB = 128
FFN_CHUNK, PROJ_CHUNK = 512, 256


def _dot(a, b):
    return jnp.dot(a, b, preferred_element_type=F32)


def _dot_nt(a, b):
    return lax.dot_general(a, b, (((1,), (1,)), ((), ())), preferred_element_type=F32)


def _dot_tn(a, b):
    return lax.dot_general(a, b, (((0,), (0,)), ((), ())), preferred_element_type=F32)


def _split3(x):
    hi = x.astype(BF16)
    r1 = x - hi.astype(F32)
    mid = r1.astype(BF16)
    lo = (r1 - mid.astype(F32)).astype(BF16)
    return hi, mid, lo


def _rms(x):
    return x * lax.rsqrt(jnp.mean(x * x, axis=-1, keepdims=True) + NORM_EPS)


def _params(sem):
    return pltpu.CompilerParams(dimension_semantics=sem, vmem_limit_bytes=VMEM_LIMIT)


def _ada_kernel(c_ref, b_ref, w_ref, o_ref):
    @pl.when(pl.program_id(0) == 0)
    def _():
        o_ref[...] = jnp.broadcast_to(b_ref[...], o_ref.shape)

    c = c_ref[...]
    hi, mid, lo = _split3(c * jax.nn.sigmoid(c))
    w = w_ref[...].astype(BF16)
    o_ref[...] += _dot(hi, w) + _dot(mid, w) + _dot(lo, w)


def _ada(c, w_ada, b_ada, tk=ADA_K):
    bsz, d = c.shape
    n = w_ada.shape[1]
    cp = jnp.zeros((SUBLANES, d), F32).at[:bsz].set(c)
    out = pl.pallas_call(
        _ada_kernel,
        grid=(d // tk,),
        in_specs=[pl.BlockSpec((SUBLANES, tk), lambda k: (0, k)),
                  pl.BlockSpec((1, n), lambda k: (0, 0)),
                  pl.BlockSpec((tk, n), lambda k: (k, 0))],
        out_specs=pl.BlockSpec((SUBLANES, n), lambda k: (0, 0)),
        out_shape=jax.ShapeDtypeStruct((SUBLANES, n), F32),
        compiler_params=_params(("arbitrary",)),
        name="ada",
    )(cp, b_ada.reshape(1, n), w_ada)
    return out[:bsz]


def _lookup(table, i):
    runs = []
    for k, v in enumerate(table):
        if runs and len(runs[-1]) == 2 and k == runs[-1][0] + 1:
            runs[-1] = (runs[-1][0], runs[-1][1], v - runs[-1][1])
        elif runs and len(runs[-1]) == 3 and v == runs[-1][1] + (k - runs[-1][0]) * runs[-1][2]:
            pass
        else:
            runs.append((k, v))
    runs = [r if len(r) == 3 else (r[0], r[1], 0) for r in runs]
    out = jnp.int32(0)
    for first, base, step in runs:
        out = jnp.where(i >= first, base + (i - first) * step, out)
    return out


def _repack_tables(pieces, n_features, tb):
    starts, valids = [], []
    for src, width, valid in pieces:
        assert width % tb == 0 and src % SUBLANES == 0 and src + width <= n_features
        for b in range(width // tb):
            starts.append(src + b * tb)
            valids.append(min(max(valid - b * tb, 0), tb))
    return starts, valids


def _ffn_kernel(x_ref, g_ref, sh_ref, sc_ref, gt_ref, wa_ref, wb_ref, wo_ref, *rest, final_norm, rows, n_cast,
                repack_valids):
    rest = list(rest)
    fg_ref = rest.pop(0) if final_norm else None
    n_rp = 1 if repack_valids else 0
    n_in = n_cast + n_rp
    side_in, o_ref, side_out, u_ref = rest[:n_in], rest[n_in], rest[n_in + 1:2 * n_in + 1], rest[-1]
    j = pl.program_id(1)
    nj = pl.num_programs(1)
    tm = x_ref.shape[0]

    def side_jobs():
        for src, dst in zip(side_in[:n_cast], side_out[:n_cast]):
            dst[...] = src[...].astype(BF16)
        if repack_valids:
            blk = jnp.minimum(pl.program_id(0) * nj + j, len(repack_valids) - 1)
            x_t = side_in[n_cast][...].T
            lane = lax.broadcasted_iota(jnp.int32, (1, x_t.shape[1]), 1)
            side_out[n_cast][...] = jnp.where(lane < _lookup(repack_valids, blk), x_t, 0.0).astype(BF16)

    def step(first, last):
        side_jobs()
        chunk = rows if (first or last) else tm
        for r in range(tm // chunk):
            sl = slice(r * chunk, (r + 1) * chunk)
            if first:
                y = _rms(x_ref[sl, :]) * g_ref[...]
                u = (y * (1.0 + sc_ref[0]) + sh_ref[0]).astype(BF16)
                u_ref[sl, :] = u
            else:
                u = u_ref[sl, :]
            a = _dot(u, wa_ref[...])
            b = _dot(u, wb_ref[...])
            acc = _dot((a * jax.nn.sigmoid(a) * b).astype(BF16), wo_ref[...])
            if not first:
                acc = o_ref[sl, :] + acc
            if last:
                acc = x_ref[sl, :] + (0.5 * gt_ref[0]) * acc
                if final_norm:
                    acc = _rms(acc) * fg_ref[...]
            o_ref[sl, :] = acc

    pl.when(j == 0)(lambda: step(True, False))
    pl.when((j > 0) & (j < nj - 1))(lambda: step(False, False))
    pl.when(j == nj - 1)(lambda: step(False, True))


def _cast_spec(shape, n_i, n_j):
    r, c = shape
    sub, lanes = 2 * SUBLANES, LANES
    if r % n_i == 0 and c % n_j == 0 and (r // n_i) % sub == 0 and (c // n_j) % lanes == 0:
        return pl.BlockSpec((r // n_i, c // n_j), lambda i, j: (i, j))
    if r % n_j == 0 and c % n_i == 0 and (r // n_j) % sub == 0 and (c // n_i) % lanes == 0:
        return pl.BlockSpec((r // n_j, c // n_i), lambda i, j: (j, i))
    assert r % n_i == 0 and (r // n_i) % sub == 0, shape
    return pl.BlockSpec((r // n_i, c), lambda i, j: (i, 0))


def _ffn(h, seq, norm_g, mod3, mod_base, w_in, w_out, final_g=None, casts=(), repack=None, tm=FFN_TM, tf=FFN_TF,
         rows=FFN_CHUNK, tb=REPACK_TB):
    n, d = h.shape
    dff = w_out.shape[0]
    nj = dff // tf
    assert nj >= 2
    tiles_per_seq = seq // tm
    n_i = n // tm

    def mod_spec(k):
        return pl.BlockSpec((1, 1, d), lambda i, j: ((i // tiles_per_seq) * N_MOD + mod_base + k, 0, 0))

    in_specs = [pl.BlockSpec((tm, d), lambda i, j: (i, 0)),
                pl.BlockSpec((1, d), lambda i, j: (0, 0)),
                mod_spec(0), mod_spec(1), mod_spec(2),
                pl.BlockSpec((d, tf), lambda i, j: (0, j)),
                pl.BlockSpec((d, tf), lambda i, j: (0, j + nj)),
                pl.BlockSpec((tf, d), lambda i, j: (j, 0))]
    args = [h, norm_g.reshape(1, d), mod3, mod3, mod3, w_in, w_in, w_out]
    if final_g is not None:
        in_specs.append(pl.BlockSpec((1, d), lambda i, j: (0, 0)))
        args.append(final_g.reshape(1, d))
    side_in = [_cast_spec(w.shape, n_i, nj) for w in casts]
    side_out = [_cast_spec(w.shape, n_i, nj) for w in casts]
    side_shapes = [jax.ShapeDtypeStruct(w.shape, BF16) for w in casts]
    side_args = list(casts)
    valids = None
    if repack is not None:
        w_t, pieces = repack
        starts, valids = _repack_tables(pieces, w_t.shape[0], tb)
        nb = len(starts)
        assert nb <= n_i * nj and w_t.shape[1] == d

        def blk(i, j):
            return jnp.minimum(i * nj + j, nb - 1)

        side_in.append(pl.BlockSpec((pl.Element(tb), pl.Element(d)),
                                    lambda i, j: (pl.multiple_of(_lookup(starts, blk(i, j)), SUBLANES), 0)))
        side_out.append(pl.BlockSpec((d, tb), lambda i, j: (0, blk(i, j))))
        side_shapes.append(jax.ShapeDtypeStruct((d, nb * tb), BF16))
        side_args.append(w_t)
    outs = pl.pallas_call(
        functools.partial(_ffn_kernel, final_norm=final_g is not None, rows=rows, n_cast=len(casts),
                          repack_valids=valids),
        grid=(n_i, nj),
        in_specs=in_specs + side_in,
        out_specs=[pl.BlockSpec((tm, d), lambda i, j: (i, 0))] + side_out,
        out_shape=[jax.ShapeDtypeStruct((n, d), F32)] + side_shapes,
        scratch_shapes=[pltpu.VMEM((tm, d), BF16)],
        compiler_params=_params(("arbitrary", "arbitrary")),
        name="ffn_final" if final_g is not None else "ffn",
    )(*args, *side_args)
    return outs[0], tuple(outs[1:])


def _proj_kernel(x_ref, g_ref, sh_ref, sc_ref, w_ref, ws_ref, bs_ref, *rest, rows, n_cast):
    cast_in, (p_ref, s_ref, u_ref), cast_out = rest[:n_cast], rest[n_cast:n_cast + 3], rest[n_cast + 3:]
    j = pl.program_id(1)
    tm = x_ref.shape[0]

    def side_jobs():
        for src, dst in zip(cast_in, cast_out):
            dst[...] = src[...].astype(BF16)

    @pl.when(j == 0)
    def _():
        side_jobs()
        for r in range(tm // rows):
            sl = slice(r * rows, (r + 1) * rows)
            y = _rms(x_ref[sl, :]) * g_ref[...]
            u = (y * (1.0 + sc_ref[0]) + sh_ref[0]).astype(BF16)
            u_ref[sl, :] = u
            s_ref[sl, :] = _dot(u, ws_ref[...]) + bs_ref[...]
            p_ref[sl, :] = _dot(u, w_ref[...])

    @pl.when(j > 0)
    def _():
        side_jobs()
        p_ref[...] = _dot(u_ref[...], w_ref[...])


def _proj(h, seq, norm_g, mod3, mod_base, w_all, ncol, small_col, b_small, casts=(), tm=PROJ_TM, tn=PROJ_TN,
          rows=PROJ_CHUNK):
    n, d = h.shape
    nsmall = b_small.shape[1]
    assert small_col % nsmall == 0 and ncol % tn == 0
    tiles_per_seq = seq // tm
    n_i, n_j = n // tm, ncol // tn

    def mod_spec(k):
        return pl.BlockSpec((1, 1, d), lambda i, j: ((i // tiles_per_seq) * N_MOD + mod_base + k, 0, 0))

    outs = pl.pallas_call(
        functools.partial(_proj_kernel, rows=rows, n_cast=len(casts)),
        grid=(n_i, n_j),
        in_specs=[pl.BlockSpec((tm, d), lambda i, j: (i, 0)),
                  pl.BlockSpec((1, d), lambda i, j: (0, 0)),
                  mod_spec(0), mod_spec(1),
                  pl.BlockSpec((d, tn), lambda i, j: (0, j)),
                  pl.BlockSpec((d, nsmall), lambda i, j: (0, small_col // nsmall)),
                  pl.BlockSpec((1, nsmall), lambda i, j: (0, 0))]
        + [_cast_spec(w.shape, n_i, n_j) for w in casts],
        out_specs=[pl.BlockSpec((tm, tn), lambda i, j: (i, j)),
                   pl.BlockSpec((tm, nsmall), lambda i, j: (i, 0)),
                   pl.BlockSpec((tm, d), lambda i, j: (i, 0))]
        + [_cast_spec(w.shape, n_i, n_j) for w in casts],
        out_shape=[jax.ShapeDtypeStruct((n, ncol), F32),
                   jax.ShapeDtypeStruct((n, nsmall), F32),
                   jax.ShapeDtypeStruct((n, d), BF16)]
        + [jax.ShapeDtypeStruct(w.shape, BF16) for w in casts],
        compiler_params=_params(("arbitrary", "arbitrary")),
        name="proj",
    )(h, norm_g.reshape(1, d), mod3, mod3, w_all, w_all, b_small, *casts)
    return outs[0], outs[1], outs[2], tuple(outs[3:])


def _attn_kernel(q_ref, kc_ref, vc_ref, kp_ref, vp_ref, posc_ref, posp_ref, invf_ref, sink_ref, o_ref, *, nsub):
    blk = WINDOW
    lane = lax.broadcasted_iota(jnp.int32, (1, LANES), 1)
    hl = lane % ATT_HEAD_DIM
    half = ROPE_DIM // 2
    lo_half = lane < ATT_HEAD_DIM

    def rope_tables(pos):
        pos = pos.astype(F32)
        ang = invf_ref[...] * pos
        reps = LANES // half
        ct = jnp.tile(jnp.cos(ang), (reps, 1)).T
        st = jnp.tile(jnp.sin(ang), (reps, 1)).T
        c = jnp.where(hl < ROPE_DIM, ct, 1.0)
        s = jnp.where(hl < ROPE_DIM, st, 0.0)
        return c, s

    def swap(x):
        return jnp.where(hl < half, pltpu.roll(x, LANES - half, 1),
                         jnp.where(hl < ROPE_DIM, -pltpu.roll(x, half, 1), 0.0))

    def rope(x, c, s):
        return x * c - swap(x) * s

    tabs = [rope_tables(posp_ref[0])] + [rope_tables(posc_ref[t]) for t in range(nsub)]
    k = jnp.concatenate([rope(kp_ref[...], *tabs[0])]
                        + [rope(kc_ref[t * blk:(t + 1) * blk, :], *tabs[t + 1]) for t in range(nsub)], axis=0)
    v = jnp.concatenate([vp_ref[...], vc_ref[...]], axis=0)
    ks = swap(k)
    k_sw, ks_sw = pltpu.roll(k, ATT_HEAD_DIM, 1), pltpu.roll(ks, ATT_HEAD_DIM, 1)
    v_sw = pltpu.roll(v, ATT_HEAD_DIM, 1)

    qi = lax.broadcasted_iota(jnp.int32, (blk, 2 * blk), 0) + blk
    kj = lax.broadcasted_iota(jnp.int32, (blk, 2 * blk), 1)
    band = (kj <= qi) & (kj > qi - WINDOW)
    slot0 = kj[0:1, :] == 0
    not_slot0 = lax.broadcasted_iota(jnp.int32, (2 * blk, 1), 0) > 0
    k_min = jnp.where(pl.program_id(1) == 0, blk, 0)
    band_first = band & (kj >= k_min)

    scale = ATT_HEAD_DIM ** -0.5
    pairs_per_group = ATT_HEADS // ATT_KV_HEADS // 2
    ones = jnp.ones((2 * blk, LANES), F32)
    one_rhs = jnp.concatenate([jnp.where(lo_half, ones, 0.0), jnp.where(lo_half, 0.0, ones)], axis=0)
    for g in range(ATT_KV_HEADS):
        k_g, k_o = (k, k_sw) if g == 0 else (k_sw, k)
        ks_g, ks_o = (ks, ks_sw) if g == 0 else (ks_sw, ks)
        v_g, v_o = (v, v_sw) if g == 0 else (v_sw, v)
        k_lo = jnp.concatenate([jnp.where(lo_half, k_g, 0.0), jnp.where(lo_half, ks_g, 0.0)], axis=1).astype(BF16)
        k_hi = jnp.concatenate([jnp.where(lo_half, 0.0, k_o), jnp.where(lo_half, 0.0, ks_o)], axis=1).astype(BF16)
        v_lo, v_hi = jnp.where(lo_half, v_g, 0.0), jnp.where(lo_half, 0.0, v_o)
        for t in range(nsub):
            rows = slice(t * blk, (t + 1) * blk)
            keys = slice(t * blk, (t + 2) * blk)
            mask = band_first if t == 0 else band
            k_rhs = jnp.concatenate([k_lo[keys], k_hi[keys]], axis=0)
            c_q, s_q = tabs[t + 1][0] * scale, tabs[t + 1][1] * scale
            v_rhs = jnp.concatenate(
                [jnp.concatenate([jnp.where(not_slot0, v_lo[keys], 0.0), jnp.where(not_slot0, v_hi[keys], 0.0)], axis=0),
                 one_rhs], axis=1).astype(BF16)
            q_rows = []
            for pp in range(pairs_per_group):
                p = g * pairs_per_group + pp
                q_p = q_ref[rows, p * LANES:(p + 1) * LANES]
                q_rows.append(jnp.concatenate([q_p * c_q, q_p * s_q], axis=1).astype(BF16))
            s_all = _dot_nt(jnp.concatenate(q_rows, axis=0), k_rhs)
            p_rows = []
            for pp in range(pairs_per_group):
                p = g * pairs_per_group + pp
                halves = []
                for e in range(2):
                    s = s_all[pp * blk:(pp + 1) * blk, e * 2 * blk:(e + 1) * 2 * blk]
                    s = jnp.where(mask, s, jnp.where(slot0, sink_ref[2 * p + e], NEG_INF))
                    m = jnp.max(s, axis=-1, keepdims=True)
                    halves.append(jnp.exp(s - m).astype(BF16))
                p_rows.append(jnp.concatenate(halves, axis=1))
            o_all = _dot(jnp.concatenate(p_rows, axis=0), v_rhs)
            o_all = o_all[:, :LANES] * (1.0 / o_all[:, LANES:])
            for pp in range(pairs_per_group):
                p = g * pairs_per_group + pp
                o_ref[rows, p * LANES:(p + 1) * LANES] = o_all[pp * blk:(pp + 1) * blk].astype(BF16)


def _attn(q_src, kv_src, pos3, inv_freq, sinks, bsz, seq, q_col, k_col, v_col, nsub=1):
    blk = WINDOW
    aw = ATT_HEADS * ATT_HEAD_DIM
    ns = seq // (nsub * blk)
    n = bsz * seq
    half = ROPE_DIM // 2

    def cur(b, i):
        return b * ns + i

    def prv(b, i):
        return b * ns * nsub + jnp.maximum(i * nsub - 1, 0)

    return pl.pallas_call(
        functools.partial(_attn_kernel, nsub=nsub),
        grid=(bsz, ns),
        in_specs=[pl.BlockSpec((nsub * blk, aw), lambda b, i: (cur(b, i), q_col // aw)),
                  pl.BlockSpec((nsub * blk, LANES), lambda b, i: (cur(b, i), k_col // LANES)),
                  pl.BlockSpec((nsub * blk, LANES), lambda b, i: (cur(b, i), v_col // LANES)),
                  pl.BlockSpec((blk, LANES), lambda b, i: (prv(b, i), k_col // LANES)),
                  pl.BlockSpec((blk, LANES), lambda b, i: (prv(b, i), v_col // LANES)),
                  pl.BlockSpec((nsub, 1, blk), lambda b, i: (cur(b, i), 0, 0)),
                  pl.BlockSpec((1, 1, blk), lambda b, i: (prv(b, i), 0, 0)),
                  pl.BlockSpec((half, blk), lambda b, i: (0, 0)),
                  pl.BlockSpec(memory_space=pltpu.SMEM)],
        out_specs=pl.BlockSpec((nsub * blk, aw), lambda b, i: (cur(b, i), 0)),
        out_shape=jax.ShapeDtypeStruct((n, aw), BF16),
        compiler_params=_params(("arbitrary", "arbitrary")),
        name="attn",
    )(q_src, kv_src, kv_src, kv_src, kv_src, pos3, pos3, inv_freq, sinks)


def _log_sigmoid(x):
    return jnp.minimum(x, 0.0) - jnp.log1p(jnp.exp(-jnp.abs(x)))


def _mlstm_kernel(xqk_ref, xprev_ref, v_ref, mo_ref, if_ref, ifn_ref, cw_ref, cb_ref, ng_ref, o_ref,
                  xbuf_ref, c_ref, m_ref, cols_ref, rt_ref, dec_ref, *, tc):
    heads, dk, dv, ch = M_HEADS, M_QK_DIM, M_V_DIM, M_CHUNK
    qkw = heads * dk
    nc = tc // ch
    assert ch * 2 == LANES and nc % 2 == 0 and 2 * heads <= SUBLANES
    t_idx = pl.program_id(1)

    lane = lax.broadcasted_iota(jnp.int32, (1, LANES), 1)
    nblk = tc // LANES
    slot = t_idx % 2

    def gate_prep(g_ref, dst):
        gates = g_ref[...]
        logf = _log_sigmoid(gates)
        row = lax.broadcasted_iota(jnp.int32, (tc, tc), 0)
        col = lax.broadcasted_iota(jnp.int32, (tc, tc), 1)
        tri = jnp.where((row // ch == col // ch) & (col <= row), 1.0, 0.0).astype(BF16)
        hi, mid, lo = _split3(logf)
        bcum = _dot(tri, hi) + _dot(tri, mid) + _dot(tri, lo)
        packed_t = jnp.where(lane < heads, gates, bcum).T

        gt8 = packed_t[0:SUBLANES, :]
        bc_t = pltpu.roll(gt8, heads, 0)
        r_t = gt8 - bc_t
        pos = lax.broadcasted_iota(jnp.int32, (1, tc), 1) % ch

        def lane_blocks(a, f):
            return jnp.concatenate([f(a[:, j * LANES:(j + 1) * LANES]) for j in range(nblk)], axis=1)

        rmax_t = r_t
        for sh in (1, 2, 4, 8, 16, 32):
            back = lane_blocks(rmax_t, lambda blk, sh=sh: pltpu.roll(blk, sh, 1))
            rmax_t = jnp.maximum(rmax_t, jnp.where(pos >= sh, back, NEG_INF))

        last = [c * ch + ch - 1 for c in range(nc)]
        b_last = [bc_t[:, i:i + 1] for i in last]
        g_max = [bc_t[:, i:i + 1] + rmax_t[:, i:i + 1] for i in last]
        m_state = m_ref[:, 0:1]
        m_in, m_out = [], []
        for c in range(nc):
            m_in.append(m_state)
            m_state = jnp.maximum(b_last[c] + m_state, g_max[c])
            m_out.append(m_state)
        m_ref[...] = jnp.broadcast_to(m_state, m_ref.shape)
        decay = [jnp.exp(b_last[c] + m_in[c] - m_out[c]) for c in range(nc)]

        def per_time(cols):
            return jnp.concatenate(
                [jnp.where(lane < ch, cols[2 * j], cols[2 * j + 1]) for j in range(nblk)], axis=1)

        m_in_t, m_out_t, b_last_t = per_time(m_in), per_time(m_out), per_time(b_last)
        w_t = jnp.exp(b_last_t + r_t - m_out_t)
        a_log_t = bc_t + m_in_t
        m_t_t = jnp.maximum(a_log_t, bc_t + rmax_t)
        inter_t = jnp.exp(a_log_t - m_t_t)
        eneg_t = jnp.exp(-m_t_t)
        sub = lax.broadcasted_iota(jnp.int32, (SUBLANES, 1), 0)
        cols_t = jnp.concatenate(
            [jnp.where(sub < heads, w_t, pltpu.roll(inter_t, heads, 0)),
             jnp.where(sub < heads, eneg_t, pltpu.roll(m_t_t, heads, 0)),
             bc_t,
             jnp.zeros((LANES - 3 * SUBLANES, tc), F32)], axis=0)
        cols_ref[dst] = cols_t.T
        rt_ref[dst] = r_t
        dec_ref[dst] = per_time(decay)

    @pl.when(t_idx == 0)
    def _():
        c_ref[...] = jnp.zeros_like(c_ref)
        m_ref[...] = jnp.full_like(m_ref, NEG_INF)
        gate_prep(if_ref, 0)

    cols = cols_ref[slot]
    r_t = rt_ref[slot]
    dec_t = dec_ref[slot]
    gate_prep(ifn_ref, 1 - slot)

    pad = SUBLANES
    xbuf_ref[0:pad, :] = jnp.where(t_idx > 0, xprev_ref[...], 0.0)
    xbuf_ref[pad:pad + tc, :] = xqk_ref[...]
    y = cb_ref[...]
    for j in range(CONV_WIDTH):
        off = pad - (CONV_WIDTH - 1) + j
        y = y + cw_ref[j:j + 1, :] * xbuf_ref[off:off + tc, :]
    y = y * jax.nn.sigmoid(y)
    q_all = y[:, :qkw].astype(BF16)
    k_all = y[:, qkw:] * (dk ** -0.5)
    k_all_b = k_all.astype(BF16)

    causal = (lax.broadcasted_iota(jnp.int32, (ch, ch), 1) <= lax.broadcasted_iota(jnp.int32, (ch, ch), 0))
    ng = ng_ref[...]
    ones_v = jnp.ones((tc, LANES), BF16)
    ones_sq = jnp.ones((dv, LANES), BF16)

    def chunks(a):
        return a.reshape(nc, ch, a.shape[-1])

    for h in range(heads):
        w_col = cols[:, h:h + 1]
        inter = cols[:, heads + h:heads + h + 1]
        eneg = cols[:, 2 * heads + h:2 * heads + h + 1]
        m_t = cols[:, 3 * heads + h:3 * heads + h + 1]
        bc = cols[:, 4 * heads + h:4 * heads + h + 1]
        r_row = jnp.stack([r_t[h:h + 1, c * ch:(c + 1) * ch] for c in range(nc)])
        q = chunks(q_all[:, h * dk:(h + 1) * dk])
        kb = chunks(k_all_b[:, h * dk:(h + 1) * dk])
        v = chunks(jnp.concatenate([v_ref[:, h * dv:(h + 1) * dv].astype(BF16), ones_v], axis=1))
        wkb = chunks((w_col * k_all[:, h * dk:(h + 1) * dk]).astype(BF16))

        c_state = c_ref[h]
        c_in = []
        for c in range(nc):
            c_in.append(c_state.astype(BF16))
            c_state = dec_t[h:h + 1, c * ch:c * ch + 1] * c_state + _dot_tn(wkb[c], v[c])
        c_ref[h] = c_state

        d = jnp.where(causal, chunks(bc) + r_row, NEG_INF)
        qk = jnp.stack([_dot_nt(q[c], kb[c]) for c in range(nc)]) * jnp.exp(d - chunks(m_t))
        qkb = qk.astype(BF16)
        num = chunks(inter) * jnp.stack([_dot(q[c], c_in[c]) for c in range(nc)]) \
            + jnp.stack([_dot(qkb[c], v[c]) for c in range(nc)])
        num = num.reshape(tc, dv + LANES)
        inv = 1.0 / jnp.maximum(jnp.abs(num[:, dv:]), eneg)
        halves = [num[:, i * LANES:(i + 1) * LANES] * inv for i in range(dv // LANES)]
        sq = jnp.concatenate([x * x for x in halves], axis=1).astype(BF16)
        rs = lax.rsqrt(_dot(sq, ones_sq) * (1.0 / dv) + NORM_EPS)
        for i, x in enumerate(halves):
            sl = slice(h * dv + i * LANES, h * dv + (i + 1) * LANES)
            o_ref[:, sl] = (x * rs * ng[:, sl] * jax.nn.sigmoid(mo_ref[:, sl])).astype(BF16)


def _mlstm(proj, small, conv_w, conv_b, norm_g, bsz, seq, qk_col, v_col, o_col, if_col, tc=MLSTM_T):
    heads, dk, dv = M_HEADS, M_QK_DIM, M_V_DIM
    qk2 = 2 * heads * dk
    mw = heads * dv
    nt = seq // tc
    n = bsz * seq

    def rows(b, t):
        return b * nt + t

    def prev_rows(b, t):
        return jnp.maximum((b * nt + t) * (tc // SUBLANES) - 1, 0)

    return pl.pallas_call(
        functools.partial(_mlstm_kernel, tc=tc),
        grid=(bsz, nt),
        in_specs=[pl.BlockSpec((tc, qk2), lambda b, t: (rows(b, t), qk_col // qk2)),
                  pl.BlockSpec((SUBLANES, qk2), lambda b, t: (prev_rows(b, t), qk_col // qk2)),
                  pl.BlockSpec((tc, mw), lambda b, t: (rows(b, t), v_col // mw)),
                  pl.BlockSpec((tc, mw), lambda b, t: (rows(b, t), o_col // mw)),
                  pl.BlockSpec((tc, LANES), lambda b, t: (rows(b, t), if_col // LANES)),
                  pl.BlockSpec((tc, LANES), lambda b, t: (rows(b, jnp.minimum(t + 1, nt - 1)), if_col // LANES)),
                  pl.BlockSpec((CONV_WIDTH, qk2), lambda b, t: (0, 0)),
                  pl.BlockSpec((1, qk2), lambda b, t: (0, 0)),
                  pl.BlockSpec((1, mw), lambda b, t: (0, 0))],
        out_specs=pl.BlockSpec((tc, mw), lambda b, t: (rows(b, t), 0)),
        out_shape=jax.ShapeDtypeStruct((n, mw), BF16),
        scratch_shapes=[pltpu.VMEM((SUBLANES + tc, qk2), F32),
                        pltpu.VMEM((heads, dk, dv + LANES), F32),
                        pltpu.VMEM((SUBLANES, LANES), F32),
                        pltpu.VMEM((2, tc, LANES), F32),
                        pltpu.VMEM((2, SUBLANES, tc), F32),
                        pltpu.VMEM((2, SUBLANES, tc), F32)],
        compiler_params=_params(("arbitrary", "arbitrary")),
        name="mlstm",
    )(proj, proj, proj, proj, small, small, conv_w, conv_b.reshape(1, qk2), norm_g.reshape(1, mw))


def _merge_kernel(oa_ref, hm_ref, u_ref, h_ref, gt_ref, wa_ref, wm_ref, wga_ref, wgm_ref, wo_ref, o_ref, *, rows):
    j = pl.program_id(1)
    nj = pl.num_programs(1)
    tm = h_ref.shape[0]

    def step(first, last):
        for r in range(tm // rows):
            sl = slice(r * rows, (r + 1) * rows)
            u = u_ref[sl, :]
            y_att = _dot(oa_ref[sl, :], wa_ref[...])
            y_m = _dot(hm_ref[sl, :], wm_ref[...])
            g_att = jax.nn.sigmoid(_dot(u, wga_ref[...]))
            g_m = jax.nn.sigmoid(_dot(u, wgm_ref[...]))
            acc = _dot((g_att * y_att + g_m * y_m).astype(BF16), wo_ref[...])
            if not first:
                acc = o_ref[sl, :] + acc
            if last:
                acc = h_ref[sl, :] + gt_ref[0] * acc
            o_ref[sl, :] = acc

    pl.when(j == 0)(lambda: step(True, False))
    pl.when((j > 0) & (j < nj - 1))(lambda: step(False, False))
    pl.when(j == nj - 1)(lambda: step(False, True))


def _merge(o_att, hm, u, h, seq, mod3, mod_idx, w_att_up, w_mlstm_up, w_gates, gates_col, w_out, tm=MERGE_TM,
           tn=MERGE_TN, rows=MERGE_TM):
    n, d = h.shape
    aw = o_att.shape[1]
    mw = hm.shape[1]
    nj = d // tn
    assert nj >= 2 and gates_col % tn == 0
    g0 = gates_col // tn
    tiles_per_seq = seq // tm
    return pl.pallas_call(
        functools.partial(_merge_kernel, rows=rows),
        grid=(n // tm, nj),
        in_specs=[pl.BlockSpec((tm, aw), lambda i, j: (i, 0)),
                  pl.BlockSpec((tm, mw), lambda i, j: (i, 0)),
                  pl.BlockSpec((tm, d), lambda i, j: (i, 0)),
                  pl.BlockSpec((tm, d), lambda i, j: (i, 0)),
                  pl.BlockSpec((1, 1, d), lambda i, j: ((i // tiles_per_seq) * N_MOD + mod_idx, 0, 0)),
                  pl.BlockSpec((aw, tn), lambda i, j: (0, j)),
                  pl.BlockSpec((mw, tn), lambda i, j: (0, j)),
                  pl.BlockSpec((d, tn), lambda i, j: (0, g0 + j)),
                  pl.BlockSpec((d, tn), lambda i, j: (0, g0 + nj + j)),
                  pl.BlockSpec((tn, d), lambda i, j: (j, 0))],
        out_specs=pl.BlockSpec((tm, d), lambda i, j: (i, 0)),
        out_shape=jax.ShapeDtypeStruct((n, d), F32),
        compiler_params=_params(("arbitrary", "arbitrary")),
        name="merge",
    )(o_att, hm, u, h, mod3, w_att_up, w_mlstm_up, w_gates, w_gates, w_out)


def kernel(x, c, positions, w_ada, b_ada, norm1_g, ffn1_w_in, ffn1_w_out, norm2_g, w_in, conv_w, conv_b,
           b_igate, b_fgate, sinks, mlstm_norm_g, w_att_up, w_mlstm_up, w_out, norm3_g, ffn2_w_in,
           ffn2_w_out, final_g):
    bsz, seq, d = x.shape
    depth = w_ada.shape[0]
    n = bsz * seq
    aw = ATT_HEADS * ATT_HEAD_DIM
    kvw = ATT_KV_HEADS * ATT_HEAD_DIM
    qk2 = 2 * M_HEADS * M_QK_DIM
    mw = M_HEADS * M_V_DIM

    sizes = (aw, kvw, kvw, qk2, mw, M_HEADS, M_HEADS, mw, 2 * d)
    offs = [0]
    for s in sizes:
        offs.append(offs[-1] + s)
    (o_q, o_k, o_v, o_qk, o_mv, o_i, o_f, o_mo, o_g) = offs[:-1]
    q_col, qk_col, mv_col, mo_col = 0, aw, aw + qk2, aw + qk2 + mw
    k_col, v_col, if_col = 0, kvw, 2 * kvw

    half = ROPE_DIM // 2
    inv_freq = jnp.power(ROPE_THETA, -jnp.arange(half, dtype=F32) * 2.0 / ROPE_DIM)
    inv_freq = jnp.broadcast_to(inv_freq[:, None], (half, WINDOW))
    pos3 = positions.reshape(n // WINDOW, 1, WINDOW)

    h = x.reshape(n, d)
    for l in range(depth):
        mod3 = _ada(c, w_ada[l], b_ada[l]).reshape(bsz * N_MOD, 1, d)

        n_main = aw + qk2 + 2 * mw
        pieces = ((o_q, aw, aw), (o_qk, qk2, qk2), (o_mv, mw, mw), (o_mo, mw, mw),
                  (o_g, 2 * d, 2 * d),
                  (o_k, 2 * kvw, 2 * kvw), (o_i, 2 * kvw, 2 * M_HEADS))
        h, (ffn2_in_b, ffn2_out_b, w_all) = _ffn(
            h, seq, norm1_g[l], mod3, 0, ffn1_w_in[l].astype(BF16), ffn1_w_out[l].astype(BF16),
            casts=(ffn2_w_in[l], ffn2_w_out[l]), repack=(jnp.swapaxes(w_in[l], 0, 1), pieces))
        g_col, s_col = n_main, n_main + 2 * d
        b_small = jnp.concatenate(
            [jnp.zeros((2 * kvw,), F32), b_igate[l], b_fgate[l], jnp.zeros((2 * kvw - 2 * M_HEADS,), F32)])[None, :]
        proj, small, u, (att_up_b, mlstm_up_b, out_b) = _proj(
            h, seq, norm2_g[l], mod3, 3, w_all, n_main, s_col, b_small, casts=(w_att_up[l], w_mlstm_up[l], w_out[l]))

        o_att = _attn(proj, small, pos3, inv_freq, sinks[l], bsz, seq, q_col, k_col, v_col)
        hm = _mlstm(proj, small, conv_w[l], conv_b[l], mlstm_norm_g[l], bsz, seq, qk_col, mv_col, mo_col, if_col)
        h = _merge(o_att, hm, u, h, seq, mod3, 5, att_up_b, mlstm_up_b, w_all, g_col, out_b)

        last = l == depth - 1
        h, _ = _ffn(h, seq, norm3_g[l], mod3, 6, ffn2_in_b, ffn2_out_b, final_g=final_g if last else None)
    return h.reshape(bsz, seq, d)
```

```python
import functools

import jax
import jax.numpy as jnp
from jax import lax
from jax.experimental import pallas as pl
from jax.experimental.pallas import tpu as pltpu

F32 = jnp.float32
BF16 = jnp.bfloat16

ATT_HEADS = 16
ATT_KV_HEADS = 2
ATT_HEAD_DIM = 64
WINDOW = 128
ROPE_DIM = ATT_HEAD_DIM // 4
ROPE_THETA = 500000.0
M_HEADS = 4
M_QK_DIM = 128
M_V_DIM = 256
M_CHUNK = 64
CONV_WIDTH = 4
NORM_EPS = 1e-6
NEG_INF = -1e30
N_MOD = 9

LANES = 128
SUBLANES = 8
VMEM_LIMIT = 60 * 1024 * 1024

ADA_K = 128
FFN_TM, FFN_TF = 1024, 512
PROJ_TM, PROJ_TN = 1024, 1024
MERGE_TM, MERGE_TN = 512, 512
MLSTM_T = 512
ATTN_BLOCKS = 2
REPACK_TB = 128
FFN_CHUNK, PROJ_CHUNK = 512, 256


def _dot(a, b):
    return jnp.dot(a, b, preferred_element_type=F32)


def _dot_nt(a, b):
    return lax.dot_general(a, b, (((1,), (1,)), ((), ())), preferred_element_type=F32)


def _dot_tn(a, b):
    return lax.dot_general(a, b, (((0,), (0,)), ((), ())), preferred_element_type=F32)


def _split3(x):
    hi = x.astype(BF16)
    r1 = x - hi.astype(F32)
    mid = r1.astype(BF16)
    lo = (r1 - mid.astype(F32)).astype(BF16)
    return hi, mid, lo


def _rms(x):
    return x * lax.rsqrt(jnp.mean(x * x, axis=-1, keepdims=True) + NORM_EPS)


def _params(sem):
    return pltpu.CompilerParams(dimension_semantics=sem, vmem_limit_bytes=VMEM_LIMIT)


def _ada_kernel(c_ref, b_ref, w_ref, o_ref):
    @pl.when(pl.program_id(0) == 0)
    def _():
        o_ref[...] = jnp.broadcast_to(b_ref[...], o_ref.shape)

    c = c_ref[...]
    hi, mid, lo = _split3(c * jax.nn.sigmoid(c))
    w = w_ref[...].astype(BF16)
    o_ref[...] += _dot(hi, w) + _dot(mid, w) + _dot(lo, w)


def _ada(c, w_ada, b_ada, tk=ADA_K):
    bsz, d = c.shape
    n = w_ada.shape[1]
    cp = jnp.zeros((SUBLANES, d), F32).at[:bsz].set(c)
    out = pl.pallas_call(
        _ada_kernel,
        grid=(d // tk,),
        in_specs=[pl.BlockSpec((SUBLANES, tk), lambda k: (0, k)),
                  pl.BlockSpec((1, n), lambda k: (0, 0)),
                  pl.BlockSpec((tk, n), lambda k: (k, 0))],
        out_specs=pl.BlockSpec((SUBLANES, n), lambda k: (0, 0)),
        out_shape=jax.ShapeDtypeStruct((SUBLANES, n), F32),
        compiler_params=_params(("arbitrary",)),
        name="ada",
    )(cp, b_ada.reshape(1, n), w_ada)
    return out[:bsz]


def _lookup(table, i):
    runs = []
    for k, v in enumerate(table):
        if runs and len(runs[-1]) == 2 and k == runs[-1][0] + 1:
            runs[-1] = (runs[-1][0], runs[-1][1], v - runs[-1][1])
        elif runs and len(runs[-1]) == 3 and v == runs[-1][1] + (k - runs[-1][0]) * runs[-1][2]:
            pass
        else:
            runs.append((k, v))
    runs = [r if len(r) == 3 else (r[0], r[1], 0) for r in runs]
    out = jnp.int32(0)
    for first, base, step in runs:
        out = jnp.where(i >= first, base + (i - first) * step, out)
    return out


def _repack_tables(pieces, n_features, tb):
    starts, valids = [], []
    for src, width, valid in pieces:
        assert width % tb == 0 and src % SUBLANES == 0 and src + width <= n_features
        for b in range(width // tb):
            starts.append(src + b * tb)
            valids.append(min(max(valid - b * tb, 0), tb))
    return starts, valids


def _ffn_kernel(x_ref, g_ref, sh_ref, sc_ref, gt_ref, wa_ref, wb_ref, wo_ref, *rest, final_norm, rows, n_cast,
                repack_valids):
    rest = list(rest)
    fg_ref = rest.pop(0) if final_norm else None
    n_rp = 1 if repack_valids else 0
    n_in = n_cast + n_rp
    side_in, o_ref, side_out, u_ref = rest[:n_in], rest[n_in], rest[n_in + 1:2 * n_in + 1], rest[-1]
    j = pl.program_id(1)
    nj = pl.num_programs(1)
    tm = x_ref.shape[0]

    def side_jobs():
        for src, dst in zip(side_in[:n_cast], side_out[:n_cast]):
            dst[...] = src[...].astype(BF16)
        if repack_valids:
            blk = jnp.minimum(pl.program_id(0) * nj + j, len(repack_valids) - 1)
            x_t = side_in[n_cast][...].T
            lane = lax.broadcasted_iota(jnp.int32, (1, x_t.shape[1]), 1)
            side_out[n_cast][...] = jnp.where(lane < _lookup(repack_valids, blk), x_t, 0.0).astype(BF16)

    def step(first, last):
        side_jobs()
        chunk = rows if (first or last) else tm
        for r in range(tm // chunk):
            sl = slice(r * chunk, (r + 1) * chunk)
            if first:
                y = _rms(x_ref[sl, :]) * g_ref[...]
                u = (y * (1.0 + sc_ref[0]) + sh_ref[0]).astype(BF16)
                u_ref[sl, :] = u
            else:
                u = u_ref[sl, :]
            a = _dot(u, wa_ref[...])
            b = _dot(u, wb_ref[...])
            acc = _dot((a * jax.nn.sigmoid(a) * b).astype(BF16), wo_ref[...])
            if not first:
                acc = o_ref[sl, :] + acc
            if last:
                acc = x_ref[sl, :] + (0.5 * gt_ref[0]) * acc
                if final_norm:
                    acc = _rms(acc) * fg_ref[...]
            o_ref[sl, :] = acc

    pl.when(j == 0)(lambda: step(True, False))
    pl.when((j > 0) & (j < nj - 1))(lambda: step(False, False))
    pl.when(j == nj - 1)(lambda: step(False, True))


def _cast_spec(shape, n_i, n_j):
    r, c = shape
    sub, lanes = 2 * SUBLANES, LANES
    if r % n_i == 0 and c % n_j == 0 and (r // n_i) % sub == 0 and (c // n_j) % lanes == 0:
        return pl.BlockSpec((r // n_i, c // n_j), lambda i, j: (i, j))
    if r % n_j == 0 and c % n_i == 0 and (r // n_j) % sub == 0 and (c // n_i) % lanes == 0:
        return pl.BlockSpec((r // n_j, c // n_i), lambda i, j: (j, i))
    assert r % n_i == 0 and (r // n_i) % sub == 0, shape
    return pl.BlockSpec((r // n_i, c), lambda i, j: (i, 0))


def _ffn(h, seq, norm_g, mod3, mod_base, w_in, w_out, final_g=None, casts=(), repack=None, tm=FFN_TM, tf=FFN_TF,
         rows=FFN_CHUNK, tb=REPACK_TB):
    n, d = h.shape
    dff = w_out.shape[0]
    nj = dff // tf
    assert nj >= 2
    tiles_per_seq = seq // tm
    n_i = n // tm

    def mod_spec(k):
        return pl.BlockSpec((1, 1, d), lambda i, j: ((i // tiles_per_seq) * N_MOD + mod_base + k, 0, 0))

    in_specs = [pl.BlockSpec((tm, d), lambda i, j: (i, 0)),
                pl.BlockSpec((1, d), lambda i, j: (0, 0)),
                mod_spec(0), mod_spec(1), mod_spec(2),
                pl.BlockSpec((d, tf), lambda i, j: (0, j)),
                pl.BlockSpec((d, tf), lambda i, j: (0, j + nj)),
                pl.BlockSpec((tf, d), lambda i, j: (j, 0))]
    args = [h, norm_g.reshape(1, d), mod3, mod3, mod3, w_in, w_in, w_out]
    if final_g is not None:
        in_specs.append(pl.BlockSpec((1, d), lambda i, j: (0, 0)))
        args.append(final_g.reshape(1, d))
    side_in = [_cast_spec(w.shape, n_i, nj) for w in casts]
    side_out = [_cast_spec(w.shape, n_i, nj) for w in casts]
    side_shapes = [jax.ShapeDtypeStruct(w.shape, BF16) for w in casts]
    side_args = list(casts)
    valids = None
    if repack is not None:
        w_t, pieces = repack
        starts, valids = _repack_tables(pieces, w_t.shape[0], tb)
        nb = len(starts)
        assert nb <= n_i * nj and w_t.shape[1] == d

        def blk(i, j):
            return jnp.minimum(i * nj + j, nb - 1)

        side_in.append(pl.BlockSpec((pl.Element(tb), pl.Element(d)),
                                    lambda i, j: (pl.multiple_of(_lookup(starts, blk(i, j)), SUBLANES), 0)))
        side_out.append(pl.BlockSpec((d, tb), lambda i, j: (0, blk(i, j))))
        side_shapes.append(jax.ShapeDtypeStruct((d, nb * tb), BF16))
        side_args.append(w_t)
    outs = pl.pallas_call(
        functools.partial(_ffn_kernel, final_norm=final_g is not None, rows=rows, n_cast=len(casts),
                          repack_valids=valids),
        grid=(n_i, nj),
        in_specs=in_specs + side_in,
        out_specs=[pl.BlockSpec((tm, d), lambda i, j: (i, 0))] + side_out,
        out_shape=[jax.ShapeDtypeStruct((n, d), F32)] + side_shapes,
        scratch_shapes=[pltpu.VMEM((tm, d), BF16)],
        compiler_params=_params(("arbitrary", "arbitrary")),
        name="ffn_final" if final_g is not None else "ffn",
    )(*args, *side_args)
    return outs[0], tuple(outs[1:])


def _proj_kernel(x_ref, g_ref, sh_ref, sc_ref, w_ref, ws_ref, bs_ref, *rest, rows, n_cast):
    cast_in, (p_ref, s_ref, u_ref), cast_out = rest[:n_cast], rest[n_cast:n_cast + 3], rest[n_cast + 3:]
    j = pl.program_id(1)
    tm = x_ref.shape[0]

    def side_jobs():
        for src, dst in zip(cast_in, cast_out):
            dst[...] = src[...].astype(BF16)

    @pl.when(j == 0)
    def _():
        side_jobs()
        for r in range(tm // rows):
            sl = slice(r * rows, (r + 1) * rows)
            y = _rms(x_ref[sl, :]) * g_ref[...]
            u = (y * (1.0 + sc_ref[0]) + sh_ref[0]).astype(BF16)
            u_ref[sl, :] = u
            s_ref[sl, :] = _dot(u, ws_ref[...]) + bs_ref[...]
            p_ref[sl, :] = _dot(u, w_ref[...])

    @pl.when(j > 0)
    def _():
        side_jobs()
        p_ref[...] = _dot(u_ref[...], w_ref[...])


def _proj(h, seq, norm_g, mod3, mod_base, w_all, ncol, small_col, b_small, casts=(), tm=PROJ_TM, tn=PROJ_TN,
          rows=PROJ_CHUNK):
    n, d = h.shape
    nsmall = b_small.shape[1]
    assert small_col % nsmall == 0 and ncol % tn == 0
    tiles_per_seq = seq // tm
    n_i, n_j = n // tm, ncol // tn

    def mod_spec(k):
        return pl.BlockSpec((1, 1, d), lambda i, j: ((i // tiles_per_seq) * N_MOD + mod_base + k, 0, 0))

    outs = pl.pallas_call(
        functools.partial(_proj_kernel, rows=rows, n_cast=len(casts)),
        grid=(n_i, n_j),
        in_specs=[pl.BlockSpec((tm, d), lambda i, j: (i, 0)),
                  pl.BlockSpec((1, d), lambda i, j: (0, 0)),
                  mod_spec(0), mod_spec(1),
                  pl.BlockSpec((d, tn), lambda i, j: (0, j)),
                  pl.BlockSpec((d, nsmall), lambda i, j: (0, small_col // nsmall)),
                  pl.BlockSpec((1, nsmall), lambda i, j: (0, 0))]
        + [_cast_spec(w.shape, n_i, n_j) for w in casts],
        out_specs=[pl.BlockSpec((tm, tn), lambda i, j: (i, j)),
                   pl.BlockSpec((tm, nsmall), lambda i, j: (i, 0)),
                   pl.BlockSpec((tm, d), lambda i, j: (i, 0))]
        + [_cast_spec(w.shape, n_i, n_j) for w in casts],
        out_shape=[jax.ShapeDtypeStruct((n, ncol), F32),
                   jax.ShapeDtypeStruct((n, nsmall), F32),
                   jax.ShapeDtypeStruct((n, d), BF16)]
        + [jax.ShapeDtypeStruct(w.shape, BF16) for w in casts],
        compiler_params=_params(("arbitrary", "arbitrary")),
        name="proj",
    )(h, norm_g.reshape(1, d), mod3, mod3, w_all, w_all, b_small, *casts)
    return outs[0], outs[1], outs[2], tuple(outs[3:])


def _attn_kernel(q_ref, kc_ref, vc_ref, kp_ref, vp_ref, posc_ref, posp_ref, invf_ref, sink_ref, o_ref, *, nsub):
    blk = WINDOW
    lane = lax.broadcasted_iota(jnp.int32, (1, LANES), 1)
    hl = lane % ATT_HEAD_DIM
    half = ROPE_DIM // 2
    lo_half = lane < ATT_HEAD_DIM

    def rope_tables(pos):
        pos = pos.astype(F32)
        ang = invf_ref[...] * pos
        reps = LANES // half
        ct = jnp.tile(jnp.cos(ang), (reps, 1)).T
        st = jnp.tile(jnp.sin(ang), (reps, 1)).T
        c = jnp.where(hl < ROPE_DIM, ct, 1.0)
        s = jnp.where(hl < ROPE_DIM, st, 0.0)
        return c, s

    def swap(x):
        return jnp.where(hl < half, pltpu.roll(x, LANES - half, 1),
                         jnp.where(hl < ROPE_DIM, -pltpu.roll(x, half, 1), 0.0))

    def rope(x, c, s):
        return x * c - swap(x) * s

    tabs = [rope_tables(posp_ref[0])] + [rope_tables(posc_ref[t]) for t in range(nsub)]
    k = jnp.concatenate([rope(kp_ref[...], *tabs[0])]
                        + [rope(kc_ref[t * blk:(t + 1) * blk, :], *tabs[t + 1]) for t in range(nsub)], axis=0)
    v = jnp.concatenate([vp_ref[...], vc_ref[...]], axis=0)
    ks = swap(k)
    k_sw, ks_sw = pltpu.roll(k, ATT_HEAD_DIM, 1), pltpu.roll(ks, ATT_HEAD_DIM, 1)
    v_sw = pltpu.roll(v, ATT_HEAD_DIM, 1)

    qi = lax.broadcasted_iota(jnp.int32, (blk, 2 * blk), 0) + blk
    kj = lax.broadcasted_iota(jnp.int32, (blk, 2 * blk), 1)
    band = (kj <= qi) & (kj > qi - WINDOW)
    slot0 = kj[0:1, :] == 0
    not_slot0 = lax.broadcasted_iota(jnp.int32, (2 * blk, 1), 0) > 0
    k_min = jnp.where(pl.program_id(1) == 0, blk, 0)
    band_first = band & (kj >= k_min)

    scale = ATT_HEAD_DIM ** -0.5
    pairs_per_group = ATT_HEADS // ATT_KV_HEADS // 2
    ones = jnp.ones((2 * blk, LANES), F32)
    one_rhs = jnp.concatenate([jnp.where(lo_half, ones, 0.0), jnp.where(lo_half, 0.0, ones)], axis=0)
    for g in range(ATT_KV_HEADS):
        k_g, k_o = (k, k_sw) if g == 0 else (k_sw, k)
        ks_g, ks_o = (ks, ks_sw) if g == 0 else (ks_sw, ks)
        v_g, v_o = (v, v_sw) if g == 0 else (v_sw, v)
        k_lo = jnp.concatenate([jnp.where(lo_half, k_g, 0.0), jnp.where(lo_half, ks_g, 0.0)], axis=1).astype(BF16)
        k_hi = jnp.concatenate([jnp.where(lo_half, 0.0, k_o), jnp.where(lo_half, 0.0, ks_o)], axis=1).astype(BF16)
        v_lo, v_hi = jnp.where(lo_half, v_g, 0.0), jnp.where(lo_half, 0.0, v_o)
        for t in range(nsub):
            rows = slice(t * blk, (t + 1) * blk)
            keys = slice(t * blk, (t + 2) * blk)
            mask = band_first if t == 0 else band
            k_rhs = jnp.concatenate([k_lo[keys], k_hi[keys]], axis=0)
            c_q, s_q = tabs[t + 1][0] * scale, tabs[t + 1][1] * scale
            v_rhs = jnp.concatenate(
                [jnp.concatenate([jnp.where(not_slot0, v_lo[keys], 0.0), jnp.where(not_slot0, v_hi[keys], 0.0)], axis=0),
                 one_rhs], axis=1).astype(BF16)
            q_rows = []
            for pp in range(pairs_per_group):
                p = g * pairs_per_group + pp
                q_p = q_ref[rows, p * LANES:(p + 1) * LANES]
                q_rows.append(jnp.concatenate([q_p * c_q, q_p * s_q], axis=1).astype(BF16))
            s_all = _dot_nt(jnp.concatenate(q_rows, axis=0), k_rhs)
            p_rows = []
            for pp in range(pairs_per_group):
                p = g * pairs_per_group + pp
                halves = []
                for e in range(2):
                    s = s_all[pp * blk:(pp + 1) * blk, e * 2 * blk:(e + 1) * 2 * blk]
                    s = jnp.where(mask, s, jnp.where(slot0, sink_ref[2 * p + e], NEG_INF))
                    m = jnp.max(s, axis=-1, keepdims=True)
                    halves.append(jnp.exp(s - m).astype(BF16))
                p_rows.append(jnp.concatenate(halves, axis=1))
            o_all = _dot(jnp.concatenate(p_rows, axis=0), v_rhs)
            o_all = o_all[:, :LANES] * (1.0 / o_all[:, LANES:])
            for pp in range(pairs_per_group):
                p = g * pairs_per_group + pp
                o_ref[rows, p * LANES:(p + 1) * LANES] = o_all[pp * blk:(pp + 1) * blk].astype(BF16)


def _attn(q_src, kv_src, pos3, inv_freq, sinks, bsz, seq, q_col, k_col, v_col, nsub=ATTN_BLOCKS):
    blk = WINDOW
    aw = ATT_HEADS * ATT_HEAD_DIM
    ns = seq // (nsub * blk)
    n = bsz * seq
    half = ROPE_DIM // 2

    def cur(b, i):
        return b * ns + i

    def prv(b, i):
        return b * ns * nsub + jnp.maximum(i * nsub - 1, 0)

    return pl.pallas_call(
        functools.partial(_attn_kernel, nsub=nsub),
        grid=(bsz, ns),
        in_specs=[pl.BlockSpec((nsub * blk, aw), lambda b, i: (cur(b, i), q_col // aw)),
                  pl.BlockSpec((nsub * blk, LANES), lambda b, i: (cur(b, i), k_col // LANES)),
                  pl.BlockSpec((nsub * blk, LANES), lambda b, i: (cur(b, i), v_col // LANES)),
                  pl.BlockSpec((blk, LANES), lambda b, i: (prv(b, i), k_col // LANES)),
                  pl.BlockSpec((blk, LANES), lambda b, i: (prv(b, i), v_col // LANES)),
                  pl.BlockSpec((nsub, 1, blk), lambda b, i: (cur(b, i), 0, 0)),
                  pl.BlockSpec((1, 1, blk), lambda b, i: (prv(b, i), 0, 0)),
                  pl.BlockSpec((half, blk), lambda b, i: (0, 0)),
                  pl.BlockSpec(memory_space=pltpu.SMEM)],
        out_specs=pl.BlockSpec((nsub * blk, aw), lambda b, i: (cur(b, i), 0)),
        out_shape=jax.ShapeDtypeStruct((n, aw), BF16),
        compiler_params=_params(("arbitrary", "arbitrary")),
        name="attn",
    )(q_src, kv_src, kv_src, kv_src, kv_src, pos3, pos3, inv_freq, sinks)


def _log_sigmoid(x):
    return jnp.minimum(x, 0.0) - jnp.log1p(jnp.exp(-jnp.abs(x)))


def _mlstm_kernel(xqk_ref, xprev_ref, v_ref, mo_ref, if_ref, ifn_ref, cw_ref, cb_ref, ng_ref, o_ref,
                  xbuf_ref, c_ref, m_ref, cols_ref, rt_ref, dec_ref, *, tc):
    heads, dk, dv, ch = M_HEADS, M_QK_DIM, M_V_DIM, M_CHUNK
    qkw = heads * dk
    nc = tc // ch
    assert ch * 2 == LANES and nc % 2 == 0 and 2 * heads <= SUBLANES
    t_idx = pl.program_id(1)

    lane = lax.broadcasted_iota(jnp.int32, (1, LANES), 1)
    nblk = tc // LANES
    slot = t_idx % 2

    def gate_prep(g_ref, dst):
        gates = g_ref[...]
        logf = _log_sigmoid(gates)
        row = lax.broadcasted_iota(jnp.int32, (tc, tc), 0)
        col = lax.broadcasted_iota(jnp.int32, (tc, tc), 1)
        tri = jnp.where((row // ch == col // ch) & (col <= row), 1.0, 0.0).astype(BF16)
        hi, mid, lo = _split3(logf)
        bcum = _dot(tri, hi) + _dot(tri, mid) + _dot(tri, lo)
        packed_t = jnp.where(lane < heads, gates, bcum).T

        gt8 = packed_t[0:SUBLANES, :]
        bc_t = pltpu.roll(gt8, heads, 0)
        r_t = gt8 - bc_t
        pos = lax.broadcasted_iota(jnp.int32, (1, tc), 1) % ch

        def lane_blocks(a, f):
            return jnp.concatenate([f(a[:, j * LANES:(j + 1) * LANES]) for j in range(nblk)], axis=1)

        rmax_t = r_t
        for sh in (1, 2, 4, 8, 16, 32):
            back = lane_blocks(rmax_t, lambda blk, sh=sh: pltpu.roll(blk, sh, 1))
            rmax_t = jnp.maximum(rmax_t, jnp.where(pos >= sh, back, NEG_INF))

        last = [c * ch + ch - 1 for c in range(nc)]
        b_last = [bc_t[:, i:i + 1] for i in last]
        g_max = [bc_t[:, i:i + 1] + rmax_t[:, i:i + 1] for i in last]
        m_state = m_ref[:, 0:1]
        m_in, m_out = [], []
        for c in range(nc):
            m_in.append(m_state)
            m_state = jnp.maximum(b_last[c] + m_state, g_max[c])
            m_out.append(m_state)
        m_ref[...] = jnp.broadcast_to(m_state, m_ref.shape)
        decay = [jnp.exp(b_last[c] + m_in[c] - m_out[c]) for c in range(nc)]

        def per_time(cols):
            return jnp.concatenate(
                [jnp.where(lane < ch, cols[2 * j], cols[2 * j + 1]) for j in range(nblk)], axis=1)

        m_in_t, m_out_t, b_last_t = per_time(m_in), per_time(m_out), per_time(b_last)
        w_t = jnp.exp(b_last_t + r_t - m_out_t)
        a_log_t = bc_t + m_in_t
        m_t_t = jnp.maximum(a_log_t, bc_t + rmax_t)
        inter_t = jnp.exp(a_log_t - m_t_t)
        eneg_t = jnp.exp(-m_t_t)
        sub = lax.broadcasted_iota(jnp.int32, (SUBLANES, 1), 0)
        cols_t = jnp.concatenate(
            [jnp.where(sub < heads, w_t, pltpu.roll(inter_t, heads, 0)),
             jnp.where(sub < heads, eneg_t, pltpu.roll(m_t_t, heads, 0)),
             bc_t,
             jnp.zeros((LANES - 3 * SUBLANES, tc), F32)], axis=0)
        cols_ref[dst] = cols_t.T
        rt_ref[dst] = r_t
        dec_ref[dst] = per_time(decay)

    @pl.when(t_idx == 0)
    def _():
        c_ref[...] = jnp.zeros_like(c_ref)
        m_ref[...] = jnp.full_like(m_ref, NEG_INF)
        gate_prep(if_ref, 0)

    cols = cols_ref[slot]
    r_t = rt_ref[slot]
    dec_t = dec_ref[slot]
    gate_prep(ifn_ref, 1 - slot)

    pad = SUBLANES
    xbuf_ref[0:pad, :] = jnp.where(t_idx > 0, xprev_ref[...], 0.0)
    xbuf_ref[pad:pad + tc, :] = xqk_ref[...]
    y = cb_ref[...]
    for j in range(CONV_WIDTH):
        off = pad - (CONV_WIDTH - 1) + j
        y = y + cw_ref[j:j + 1, :] * xbuf_ref[off:off + tc, :]
    y = y * jax.nn.sigmoid(y)
    q_all = y[:, :qkw].astype(BF16)
    k_all = y[:, qkw:] * (dk ** -0.5)
    k_all_b = k_all.astype(BF16)

    causal = (lax.broadcasted_iota(jnp.int32, (ch, ch), 1) <= lax.broadcasted_iota(jnp.int32, (ch, ch), 0))
    ng = ng_ref[...]
    ones_v = jnp.ones((tc, LANES), BF16)
    ones_sq = jnp.ones((dv, LANES), BF16)

    def chunks(a):
        return a.reshape(nc, ch, a.shape[-1])

    for h in range(heads):
        w_col = cols[:, h:h + 1]
        inter = cols[:, heads + h:heads + h + 1]
        eneg = cols[:, 2 * heads + h:2 * heads + h + 1]
        m_t = cols[:, 3 * heads + h:3 * heads + h + 1]
        bc = cols[:, 4 * heads + h:4 * heads + h + 1]
        r_row = jnp.stack([r_t[h:h + 1, c * ch:(c + 1) * ch] for c in range(nc)])
        q = chunks(q_all[:, h * dk:(h + 1) * dk])
        kb = chunks(k_all_b[:, h * dk:(h + 1) * dk])
        v = chunks(jnp.concatenate([v_ref[:, h * dv:(h + 1) * dv].astype(BF16), ones_v], axis=1))
        wkb = chunks((w_col * k_all[:, h * dk:(h + 1) * dk]).astype(BF16))

        c_state = c_ref[h]
        c_in = []
        for c in range(nc):
            c_in.append(c_state.astype(BF16))
            c_state = dec_t[h:h + 1, c * ch:c * ch + 1] * c_state + _dot_tn(wkb[c], v[c])
        c_ref[h] = c_state

        d = jnp.where(causal, chunks(bc) + r_row, NEG_INF)
        qk = jnp.stack([_dot_nt(q[c], kb[c]) for c in range(nc)]) * jnp.exp(d - chunks(m_t))
        qkb = qk.astype(BF16)
        num = chunks(inter) * jnp.stack([_dot(q[c], c_in[c]) for c in range(nc)]) \
            + jnp.stack([_dot(qkb[c], v[c]) for c in range(nc)])
        num = num.reshape(tc, dv + LANES)
        inv = 1.0 / jnp.maximum(jnp.abs(num[:, dv:]), eneg)
        halves = [num[:, i * LANES:(i + 1) * LANES] * inv for i in range(dv // LANES)]
        sq = jnp.concatenate([x * x for x in halves], axis=1).astype(BF16)
        rs = lax.rsqrt(_dot(sq, ones_sq) * (1.0 / dv) + NORM_EPS)
        for i, x in enumerate(halves):
            sl = slice(h * dv + i * LANES, h * dv + (i + 1) * LANES)
            o_ref[:, sl] = (x * rs * ng[:, sl] * jax.nn.sigmoid(mo_ref[:, sl])).astype(BF16)


def _mlstm(proj, small, conv_w, conv_b, norm_g, bsz, seq, qk_col, v_col, o_col, if_col, tc=MLSTM_T):
    heads, dk, dv = M_HEADS, M_QK_DIM, M_V_DIM
    qk2 = 2 * heads * dk
    mw = heads * dv
    nt = seq // tc
    n = bsz * seq

    def rows(b, t):
        return b * nt + t

    def prev_rows(b, t):
        return jnp.maximum((b * nt + t) * (tc // SUBLANES) - 1, 0)

    return pl.pallas_call(
        functools.partial(_mlstm_kernel, tc=tc),
        grid=(bsz, nt),
        in_specs=[pl.BlockSpec((tc, qk2), lambda b, t: (rows(b, t), qk_col // qk2)),
                  pl.BlockSpec((SUBLANES, qk2), lambda b, t: (prev_rows(b, t), qk_col // qk2)),
                  pl.BlockSpec((tc, mw), lambda b, t: (rows(b, t), v_col // mw)),
                  pl.BlockSpec((tc, mw), lambda b, t: (rows(b, t), o_col // mw)),
                  pl.BlockSpec((tc, LANES), lambda b, t: (rows(b, t), if_col // LANES)),
                  pl.BlockSpec((tc, LANES), lambda b, t: (rows(b, jnp.minimum(t + 1, nt - 1)), if_col // LANES)),
                  pl.BlockSpec((CONV_WIDTH, qk2), lambda b, t: (0, 0)),
                  pl.BlockSpec((1, qk2), lambda b, t: (0, 0)),
                  pl.BlockSpec((1, mw), lambda b, t: (0, 0))],
        out_specs=pl.BlockSpec((tc, mw), lambda b, t: (rows(b, t), 0)),
        out_shape=jax.ShapeDtypeStruct((n, mw), BF16),
        scratch_shapes=[pltpu.VMEM((SUBLANES + tc, qk2), F32),
                        pltpu.VMEM((heads, dk, dv + LANES), F32),
                        pltpu.VMEM((SUBLANES, LANES), F32),
                        pltpu.VMEM((2, tc, LANES), F32),
                        pltpu.VMEM((2, SUBLANES, tc), F32),
                        pltpu.VMEM((2, SUBLANES, tc), F32)],
        compiler_params=_params(("arbitrary", "arbitrary")),
        name="mlstm",
    )(proj, proj, proj, proj, small, small, conv_w, conv_b.reshape(1, qk2), norm_g.reshape(1, mw))


def _merge_kernel(oa_ref, hm_ref, u_ref, h_ref, gt_ref, wa_ref, wm_ref, wga_ref, wgm_ref, wo_ref, o_ref, *, rows):
    j = pl.program_id(1)
    nj = pl.num_programs(1)
    tm = h_ref.shape[0]

    def step(first, last):
        for r in range(tm // rows):
            sl = slice(r * rows, (r + 1) * rows)
            u = u_ref[sl, :]
            y_att = _dot(oa_ref[sl, :], wa_ref[...])
            y_m = _dot(hm_ref[sl, :], wm_ref[...])
            g_att = jax.nn.sigmoid(_dot(u, wga_ref[...]))
            g_m = jax.nn.sigmoid(_dot(u, wgm_ref[...]))
            acc = _dot((g_att * y_att + g_m * y_m).astype(BF16), wo_ref[...])
            if not first:
                acc = o_ref[sl, :] + acc
            if last:
                acc = h_ref[sl, :] + gt_ref[0] * acc
            o_ref[sl, :] = acc

    pl.when(j == 0)(lambda: step(True, False))
    pl.when((j > 0) & (j < nj - 1))(lambda: step(False, False))
    pl.when(j == nj - 1)(lambda: step(False, True))


def _merge(o_att, hm, u, h, seq, mod3, mod_idx, w_att_up, w_mlstm_up, w_gates, gates_col, w_out, tm=MERGE_TM,
           tn=MERGE_TN, rows=MERGE_TM):
    n, d = h.shape
    aw = o_att.shape[1]
    mw = hm.shape[1]
    nj = d // tn
    assert nj >= 2 and gates_col % tn == 0
    g0 = gates_col // tn
    tiles_per_seq = seq // tm
    return pl.pallas_call(
        functools.partial(_merge_kernel, rows=rows),
        grid=(n // tm, nj),
        in_specs=[pl.BlockSpec((tm, aw), lambda i, j: (i, 0)),
                  pl.BlockSpec((tm, mw), lambda i, j: (i, 0)),
                  pl.BlockSpec((tm, d), lambda i, j: (i, 0)),
                  pl.BlockSpec((tm, d), lambda i, j: (i, 0)),
                  pl.BlockSpec((1, 1, d), lambda i, j: ((i // tiles_per_seq) * N_MOD + mod_idx, 0, 0)),
                  pl.BlockSpec((aw, tn), lambda i, j: (0, j)),
                  pl.BlockSpec((mw, tn), lambda i, j: (0, j)),
                  pl.BlockSpec((d, tn), lambda i, j: (0, g0 + j)),
                  pl.BlockSpec((d, tn), lambda i, j: (0, g0 + nj + j)),
                  pl.BlockSpec((tn, d), lambda i, j: (j, 0))],
        out_specs=pl.BlockSpec((tm, d), lambda i, j: (i, 0)),
        out_shape=jax.ShapeDtypeStruct((n, d), F32),
        compiler_params=_params(("arbitrary", "arbitrary")),
        name="merge",
    )(o_att, hm, u, h, mod3, w_att_up, w_mlstm_up, w_gates, w_gates, w_out)


def kernel(x, c, positions, w_ada, b_ada, norm1_g, ffn1_w_in, ffn1_w_out, norm2_g, w_in, conv_w, conv_b,
           b_igate, b_fgate, sinks, mlstm_norm_g, w_att_up, w_mlstm_up, w_out, norm3_g, ffn2_w_in,
           ffn2_w_out, final_g):
    bsz, seq, d = x.shape
    depth = w_ada.shape[0]
    n = bsz * seq
    aw = ATT_HEADS * ATT_HEAD_DIM
    kvw = ATT_KV_HEADS * ATT_HEAD_DIM
    qk2 = 2 * M_HEADS * M_QK_DIM
    mw = M_HEADS * M_V_DIM

    sizes = (aw, kvw, kvw, qk2, mw, M_HEADS, M_HEADS, mw, 2 * d)
    offs = [0]
    for s in sizes:
        offs.append(offs[-1] + s)
    (o_q, o_k, o_v, o_qk, o_mv, o_i, o_f, o_mo, o_g) = offs[:-1]
    q_col, qk_col, mv_col, mo_col = 0, aw, aw + qk2, aw + qk2 + mw
    k_col, v_col, if_col = 0, kvw, 2 * kvw

    half = ROPE_DIM // 2
    inv_freq = jnp.power(ROPE_THETA, -jnp.arange(half, dtype=F32) * 2.0 / ROPE_DIM)
    inv_freq = jnp.broadcast_to(inv_freq[:, None], (half, WINDOW))
    pos3 = positions.reshape(n // WINDOW, 1, WINDOW)

    h = x.reshape(n, d)
    for l in range(depth):
        mod3 = _ada(c, w_ada[l], b_ada[l]).reshape(bsz * N_MOD, 1, d)

        n_main = aw + qk2 + 2 * mw
        pieces = ((o_q, aw, aw), (o_qk, qk2, qk2), (o_mv, mw, mw), (o_mo, mw, mw),
                  (o_g, 2 * d, 2 * d),
                  (o_k, 2 * kvw, 2 * kvw), (o_i, 2 * kvw, 2 * M_HEADS))
        h, (ffn2_in_b, ffn2_out_b, w_all) = _ffn(
            h, seq, norm1_g[l], mod3, 0, ffn1_w_in[l].astype(BF16), ffn1_w_out[l].astype(BF16),
            casts=(ffn2_w_in[l], ffn2_w_out[l]), repack=(jnp.swapaxes(w_in[l], 0, 1), pieces))
        g_col, s_col = n_main, n_main + 2 * d
        b_small = jnp.concatenate(
            [jnp.zeros((2 * kvw,), F32), b_igate[l], b_fgate[l], jnp.zeros((2 * kvw - 2 * M_HEADS,), F32)])[None, :]
        proj, small, u, (att_up_b, mlstm_up_b, out_b) = _proj(
            h, seq, norm2_g[l], mod3, 3, w_all, n_main, s_col, b_small, casts=(w_att_up[l], w_mlstm_up[l], w_out[l]))

        o_att = _attn(proj, small, pos3, inv_freq, sinks[l], bsz, seq, q_col, k_col, v_col)
        hm = _mlstm(proj, small, conv_w[l], conv_b[l], mlstm_norm_g[l], bsz, seq, qk_col, mv_col, mo_col, if_col)
        h = _merge(o_att, hm, u, h, seq, mod3, 5, att_up_b, mlstm_up_b, w_all, g_col, out_b)

        last = l == depth - 1
        h, _ = _ffn(h, seq, norm3_g[l], mod3, 6, ffn2_in_b, ffn2_out_b, final_g=final_g if last else None)
    return h.reshape(bsz, seq, d)
```

```python
import functools

import jax
import jax.numpy as jnp
from jax import lax
from jax.experimental import pallas as pl
from jax.experimental.pallas import tpu as pltpu

F32 = jnp.float32
BF16 = jnp.bfloat16

ATT_HEADS = 16
ATT_KV_HEADS = 2
ATT_HEAD_DIM = 64
WINDOW = 128
ROPE_DIM = ATT_HEAD_DIM // 4
ROPE_THETA = 500000.0
M_HEADS = 4
M_QK_DIM = 128
M_V_DIM = 256
M_CHUNK = 64
CONV_WIDTH = 4
NORM_EPS = 1e-6
NEG_INF = -1e30
N_MOD = 9

LANES = 128
SUBLANES = 8
VMEM_LIMIT = 60 * 1024 * 1024

ADA_K = 128
FFN_TM, FFN_TF = 1024, 512
PROJ_TM, PROJ_TN = 1024, 1024
MERGE_TM, MERGE_TN = 1024, 256
MLSTM_T = 512
ATTN_BLOCKS = 1
REPACK_TB = 128
FFN_CHUNK, PROJ_CHUNK = 512, 256


def _dot(a, b):
    return jnp.dot(a, b, preferred_element_type=F32)


def _dot_nt(a, b):
    return lax.dot_general(a, b, (((1,), (1,)), ((), ())), preferred_element_type=F32)


def _dot_tn(a, b):
    return lax.dot_general(a, b, (((0,), (0,)), ((), ())), preferred_element_type=F32)


def _split3(x):
    hi = x.astype(BF16)
    r1 = x - hi.astype(F32)
    mid = r1.astype(BF16)
    lo = (r1 - mid.astype(F32)).astype(BF16)
    return hi, mid, lo


def _rms(x):
    return x * lax.rsqrt(jnp.mean(x * x, axis=-1, keepdims=True) + NORM_EPS)


def _params(sem):
    return pltpu.CompilerParams(dimension_semantics=sem, vmem_limit_bytes=VMEM_LIMIT)


def _ada_kernel(c_ref, b_ref, w_ref, o_ref):
    @pl.when(pl.program_id(0) == 0)
    def _():
        o_ref[...] = jnp.broadcast_to(b_ref[...], o_ref.shape)

    c = c_ref[...]
    hi, mid, lo = _split3(c * jax.nn.sigmoid(c))
    w = w_ref[...].astype(BF16)
    o_ref[...] += _dot(hi, w) + _dot(mid, w) + _dot(lo, w)


def _ada(c, w_ada, b_ada, tk=ADA_K):
    bsz, d = c.shape
    n = w_ada.shape[1]
    cp = jnp.zeros((SUBLANES, d), F32).at[:bsz].set(c)
    out = pl.pallas_call(
        _ada_kernel,
        grid=(d // tk,),
        in_specs=[pl.BlockSpec((SUBLANES, tk), lambda k: (0, k)),
                  pl.BlockSpec((1, n), lambda k: (0, 0)),
                  pl.BlockSpec((tk, n), lambda k: (k, 0))],
        out_specs=pl.BlockSpec((SUBLANES, n), lambda k: (0, 0)),
        out_shape=jax.ShapeDtypeStruct((SUBLANES, n), F32),
        compiler_params=_params(("arbitrary",)),
        name="ada",
    )(cp, b_ada.reshape(1, n), w_ada)
    return out[:bsz]


def _lookup(table, i):
    runs = []
    for k, v in enumerate(table):
        if runs and len(runs[-1]) == 2 and k == runs[-1][0] + 1:
            runs[-1] = (runs[-1][0], runs[-1][1], v - runs[-1][1])
        elif runs and len(runs[-1]) == 3 and v == runs[-1][1] + (k - runs[-1][0]) * runs[-1][2]:
            pass
        else:
            runs.append((k, v))
    runs = [r if len(r) == 3 else (r[0], r[1], 0) for r in runs]
    out = jnp.int32(0)
    for first, base, step in runs:
        out = jnp.where(i >= first, base + (i - first) * step, out)
    return out


def _repack_tables(pieces, n_features, tb):
    starts, valids = [], []
    for src, width, valid in pieces:
        assert width % tb == 0 and src % SUBLANES == 0 and src + width <= n_features
        for b in range(width // tb):
            starts.append(src + b * tb)
            valids.append(min(max(valid - b * tb, 0), tb))
    return starts, valids


def _ffn_kernel(x_ref, g_ref, sh_ref, sc_ref, gt_ref, wa_ref, wb_ref, wo_ref, *rest, final_norm, rows, n_cast,
                repack_valids):
    rest = list(rest)
    fg_ref = rest.pop(0) if final_norm else None
    n_rp = 1 if repack_valids else 0
    n_in = n_cast + n_rp
    side_in, o_ref, side_out, u_ref = rest[:n_in], rest[n_in], rest[n_in + 1:2 * n_in + 1], rest[-1]
    j = pl.program_id(1)
    nj = pl.num_programs(1)
    tm = x_ref.shape[0]

    def side_jobs():
        for src, dst in zip(side_in[:n_cast], side_out[:n_cast]):
            dst[...] = src[...].astype(BF16)
        if repack_valids:
            blk = jnp.minimum(pl.program_id(0) * nj + j, len(repack_valids) - 1)
            x_t = side_in[n_cast][...].T
            lane = lax.broadcasted_iota(jnp.int32, (1, x_t.shape[1]), 1)
            side_out[n_cast][...] = jnp.where(lane < _lookup(repack_valids, blk), x_t, 0.0).astype(BF16)

    def step(first, last):
        side_jobs()
        chunk = rows if (first or last) else tm
        for r in range(tm // chunk):
            sl = slice(r * chunk, (r + 1) * chunk)
            if first:
                y = _rms(x_ref[sl, :]) * g_ref[...]
                u = (y * (1.0 + sc_ref[0]) + sh_ref[0]).astype(BF16)
                u_ref[sl, :] = u
            else:
                u = u_ref[sl, :]
            a = _dot(u, wa_ref[...])
            b = _dot(u, wb_ref[...])
            acc = _dot((a * jax.nn.sigmoid(a) * b).astype(BF16), wo_ref[...])
            if not first:
                acc = o_ref[sl, :] + acc
            if last:
                acc = x_ref[sl, :] + (0.5 * gt_ref[0]) * acc
                if final_norm:
                    acc = _rms(acc) * fg_ref[...]
            o_ref[sl, :] = acc

    pl.when(j == 0)(lambda: step(True, False))
    pl.when((j > 0) & (j < nj - 1))(lambda: step(False, False))
    pl.when(j == nj - 1)(lambda: step(False, True))


def _cast_spec(shape, n_i, n_j):
    r, c = shape
    sub, lanes = 2 * SUBLANES, LANES
    if r % n_i == 0 and c % n_j == 0 and (r // n_i) % sub == 0 and (c // n_j) % lanes == 0:
        return pl.BlockSpec((r // n_i, c // n_j), lambda i, j: (i, j))
    if r % n_j == 0 and c % n_i == 0 and (r // n_j) % sub == 0 and (c // n_i) % lanes == 0:
        return pl.BlockSpec((r // n_j, c // n_i), lambda i, j: (j, i))
    assert r % n_i == 0 and (r // n_i) % sub == 0, shape
    return pl.BlockSpec((r // n_i, c), lambda i, j: (i, 0))


def _ffn(h, seq, norm_g, mod3, mod_base, w_in, w_out, final_g=None, casts=(), repack=None, tm=FFN_TM, tf=FFN_TF,
         rows=FFN_CHUNK, tb=REPACK_TB):
    n, d = h.shape
    dff = w_out.shape[0]
    nj = dff // tf
    assert nj >= 2
    tiles_per_seq = seq // tm
    n_i = n // tm

    def mod_spec(k):
        return pl.BlockSpec((1, 1, d), lambda i, j: ((i // tiles_per_seq) * N_MOD + mod_base + k, 0, 0))

    in_specs = [pl.BlockSpec((tm, d), lambda i, j: (i, 0)),
                pl.BlockSpec((1, d), lambda i, j: (0, 0)),
                mod_spec(0), mod_spec(1), mod_spec(2),
                pl.BlockSpec((d, tf), lambda i, j: (0, j)),
                pl.BlockSpec((d, tf), lambda i, j: (0, j + nj)),
                pl.BlockSpec((tf, d), lambda i, j: (j, 0))]
    args = [h, norm_g.reshape(1, d), mod3, mod3, mod3, w_in, w_in, w_out]
    if final_g is not None:
        in_specs.append(pl.BlockSpec((1, d), lambda i, j: (0, 0)))
        args.append(final_g.reshape(1, d))
    side_in = [_cast_spec(w.shape, n_i, nj) for w in casts]
    side_out = [_cast_spec(w.shape, n_i, nj) for w in casts]
    side_shapes = [jax.ShapeDtypeStruct(w.shape, BF16) for w in casts]
    side_args = list(casts)
    valids = None
    if repack is not None:
        w_t, pieces = repack
        starts, valids = _repack_tables(pieces, w_t.shape[0], tb)
        nb = len(starts)
        assert nb <= n_i * nj and w_t.shape[1] == d

        def blk(i, j):
            return jnp.minimum(i * nj + j, nb - 1)

        side_in.append(pl.BlockSpec((pl.Element(tb), pl.Element(d)),
                                    lambda i, j: (pl.multiple_of(_lookup(starts, blk(i, j)), SUBLANES), 0)))
        side_out.append(pl.BlockSpec((d, tb), lambda i, j: (0, blk(i, j))))
        side_shapes.append(jax.ShapeDtypeStruct((d, nb * tb), BF16))
        side_args.append(w_t)
    outs = pl.pallas_call(
        functools.partial(_ffn_kernel, final_norm=final_g is not None, rows=rows, n_cast=len(casts),
                          repack_valids=valids),
        grid=(n_i, nj),
        in_specs=in_specs + side_in,
        out_specs=[pl.BlockSpec((tm, d), lambda i, j: (i, 0))] + side_out,
        out_shape=[jax.ShapeDtypeStruct((n, d), F32)] + side_shapes,
        scratch_shapes=[pltpu.VMEM((tm, d), BF16)],
        compiler_params=_params(("arbitrary", "arbitrary")),
        name="ffn_final" if final_g is not None else "ffn",
    )(*args, *side_args)
    return outs[0], tuple(outs[1:])


def _proj_kernel(x_ref, g_ref, sh_ref, sc_ref, w_ref, ws_ref, bs_ref, *rest, rows, n_cast):
    cast_in, (p_ref, s_ref, u_ref), cast_out = rest[:n_cast], rest[n_cast:n_cast + 3], rest[n_cast + 3:]
    j = pl.program_id(1)
    tm = x_ref.shape[0]

    def side_jobs():
        for src, dst in zip(cast_in, cast_out):
            dst[...] = src[...].astype(BF16)

    @pl.when(j == 0)
    def _():
        side_jobs()
        for r in range(tm // rows):
            sl = slice(r * rows, (r + 1) * rows)
            y = _rms(x_ref[sl, :]) * g_ref[...]
            u = (y * (1.0 + sc_ref[0]) + sh_ref[0]).astype(BF16)
            u_ref[sl, :] = u
            s_ref[sl, :] = _dot(u, ws_ref[...]) + bs_ref[...]
            p_ref[sl, :] = _dot(u, w_ref[...])

    @pl.when(j > 0)
    def _():
        side_jobs()
        p_ref[...] = _dot(u_ref[...], w_ref[...])


def _proj(h, seq, norm_g, mod3, mod_base, w_all, ncol, small_col, b_small, casts=(), tm=PROJ_TM, tn=PROJ_TN,
          rows=PROJ_CHUNK):
    n, d = h.shape
    nsmall = b_small.shape[1]
    assert small_col % nsmall == 0 and ncol % tn == 0
    tiles_per_seq = seq // tm
    n_i, n_j = n // tm, ncol // tn

    def mod_spec(k):
        return pl.BlockSpec((1, 1, d), lambda i, j: ((i // tiles_per_seq) * N_MOD + mod_base + k, 0, 0))

    outs = pl.pallas_call(
        functools.partial(_proj_kernel, rows=rows, n_cast=len(casts)),
        grid=(n_i, n_j),
        in_specs=[pl.BlockSpec((tm, d), lambda i, j: (i, 0)),
                  pl.BlockSpec((1, d), lambda i, j: (0, 0)),
                  mod_spec(0), mod_spec(1),
                  pl.BlockSpec((d, tn), lambda i, j: (0, j)),
                  pl.BlockSpec((d, nsmall), lambda i, j: (0, small_col // nsmall)),
                  pl.BlockSpec((1, nsmall), lambda i, j: (0, 0))]
        + [_cast_spec(w.shape, n_i, n_j) for w in casts],
        out_specs=[pl.BlockSpec((tm, tn), lambda i, j: (i, j)),
                   pl.BlockSpec((tm, nsmall), lambda i, j: (i, 0)),
                   pl.BlockSpec((tm, d), lambda i, j: (i, 0))]
        + [_cast_spec(w.shape, n_i, n_j) for w in casts],
        out_shape=[jax.ShapeDtypeStruct((n, ncol), F32),
                   jax.ShapeDtypeStruct((n, nsmall), F32),
                   jax.ShapeDtypeStruct((n, d), BF16)]
        + [jax.ShapeDtypeStruct(w.shape, BF16) for w in casts],
        compiler_params=_params(("arbitrary", "arbitrary")),
        name="proj",
    )(h, norm_g.reshape(1, d), mod3, mod3, w_all, w_all, b_small, *casts)
    return outs[0], outs[1], outs[2], tuple(outs[3:])


def _attn_kernel(q_ref, kc_ref, vc_ref, kp_ref, vp_ref, posc_ref, posp_ref, invf_ref, sink_ref, o_ref, *, nsub):
    blk = WINDOW
    lane = lax.broadcasted_iota(jnp.int32, (1, LANES), 1)
    hl = lane % ATT_HEAD_DIM
    half = ROPE_DIM // 2
    lo_half = lane < ATT_HEAD_DIM

    def rope_tables(pos):
        pos = pos.astype(F32)
        ang = invf_ref[...] * pos
        reps = LANES // half
        ct = jnp.tile(jnp.cos(ang), (reps, 1)).T
        st = jnp.tile(jnp.sin(ang), (reps, 1)).T
        c = jnp.where(hl < ROPE_DIM, ct, 1.0)
        s = jnp.where(hl < ROPE_DIM, st, 0.0)
        return c, s

    def swap(x):
        return jnp.where(hl < half, pltpu.roll(x, LANES - half, 1),
                         jnp.where(hl < ROPE_DIM, -pltpu.roll(x, half, 1), 0.0))

    def rope(x, c, s):
        return x * c - swap(x) * s

    tabs = [rope_tables(posp_ref[0])] + [rope_tables(posc_ref[t]) for t in range(nsub)]
    k = jnp.concatenate([rope(kp_ref[...], *tabs[0])]
                        + [rope(kc_ref[t * blk:(t + 1) * blk, :], *tabs[t + 1]) for t in range(nsub)], axis=0)
    v = jnp.concatenate([vp_ref[...], vc_ref[...]], axis=0)
    ks = swap(k)
    k_sw, ks_sw = pltpu.roll(k, ATT_HEAD_DIM, 1), pltpu.roll(ks, ATT_HEAD_DIM, 1)
    v_sw = pltpu.roll(v, ATT_HEAD_DIM, 1)

    qi = lax.broadcasted_iota(jnp.int32, (blk, 2 * blk), 0) + blk
    kj = lax.broadcasted_iota(jnp.int32, (blk, 2 * blk), 1)
    band = (kj <= qi) & (kj > qi - WINDOW)
    slot0 = kj[0:1, :] == 0
    not_slot0 = lax.broadcasted_iota(jnp.int32, (2 * blk, 1), 0) > 0
    k_min = jnp.where(pl.program_id(1) == 0, blk, 0)
    band_first = band & (kj >= k_min)

    scale = ATT_HEAD_DIM ** -0.5
    pairs_per_group = ATT_HEADS // ATT_KV_HEADS // 2
    ones = jnp.ones((2 * blk, LANES), F32)
    one_rhs = jnp.concatenate([jnp.where(lo_half, ones, 0.0), jnp.where(lo_half, 0.0, ones)], axis=0)
    for g in range(ATT_KV_HEADS):
        k_g, k_o = (k, k_sw) if g == 0 else (k_sw, k)
        ks_g, ks_o = (ks, ks_sw) if g == 0 else (ks_sw, ks)
        v_g, v_o = (v, v_sw) if g == 0 else (v_sw, v)
        k_lo = jnp.concatenate([jnp.where(lo_half, k_g, 0.0), jnp.where(lo_half, ks_g, 0.0)], axis=1).astype(BF16)
        k_hi = jnp.concatenate([jnp.where(lo_half, 0.0, k_o), jnp.where(lo_half, 0.0, ks_o)], axis=1).astype(BF16)
        v_lo, v_hi = jnp.where(lo_half, v_g, 0.0), jnp.where(lo_half, 0.0, v_o)
        for t in range(nsub):
            rows = slice(t * blk, (t + 1) * blk)
            keys = slice(t * blk, (t + 2) * blk)
            mask = band_first if t == 0 else band
            k_rhs = jnp.concatenate([k_lo[keys], k_hi[keys]], axis=0)
            c_q, s_q = tabs[t + 1][0] * scale, tabs[t + 1][1] * scale
            v_rhs = jnp.concatenate(
                [jnp.concatenate([jnp.where(not_slot0, v_lo[keys], 0.0), jnp.where(not_slot0, v_hi[keys], 0.0)], axis=0),
                 one_rhs], axis=1).astype(BF16)
            q_rows = []
            for pp in range(pairs_per_group):
                p = g * pairs_per_group + pp
                q_p = q_ref[rows, p * LANES:(p + 1) * LANES]
                q_rows.append(jnp.concatenate([q_p * c_q, q_p * s_q], axis=1).astype(BF16))
            s_all = _dot_nt(jnp.concatenate(q_rows, axis=0), k_rhs)
            p_rows = []
            for pp in range(pairs_per_group):
                p = g * pairs_per_group + pp
                halves = []
                for e in range(2):
                    s = s_all[pp * blk:(pp + 1) * blk, e * 2 * blk:(e + 1) * 2 * blk]
                    s = jnp.where(mask, s, jnp.where(slot0, sink_ref[2 * p + e], NEG_INF))
                    m = jnp.max(s, axis=-1, keepdims=True)
                    halves.append(jnp.exp(s - m).astype(BF16))
                p_rows.append(jnp.concatenate(halves, axis=1))
            o_all = _dot(jnp.concatenate(p_rows, axis=0), v_rhs)
            o_all = o_all[:, :LANES] * (1.0 / o_all[:, LANES:])
            for pp in range(pairs_per_group):
                p = g * pairs_per_group + pp
                o_ref[rows, p * LANES:(p + 1) * LANES] = o_all[pp * blk:(pp + 1) * blk].astype(BF16)


def _attn(q_src, kv_src, pos3, inv_freq, sinks, bsz, seq, q_col, k_col, v_col, nsub=ATTN_BLOCKS):
    blk = WINDOW
    aw = ATT_HEADS * ATT_HEAD_DIM
    ns = seq // (nsub * blk)
    n = bsz * seq
    half = ROPE_DIM // 2

    def cur(b, i):
        return b * ns + i

    def prv(b, i):
        return b * ns * nsub + jnp.maximum(i * nsub - 1, 0)

    return pl.pallas_call(
        functools.partial(_attn_kernel, nsub=nsub),
        grid=(bsz, ns),
        in_specs=[pl.BlockSpec((nsub * blk, aw), lambda b, i: (cur(b, i), q_col // aw)),
                  pl.BlockSpec((nsub * blk, LANES), lambda b, i: (cur(b, i), k_col // LANES)),
                  pl.BlockSpec((nsub * blk, LANES), lambda b, i: (cur(b, i), v_col // LANES)),
                  pl.BlockSpec((blk, LANES), lambda b, i: (prv(b, i), k_col // LANES)),
                  pl.BlockSpec((blk, LANES), lambda b, i: (prv(b, i), v_col // LANES)),
                  pl.BlockSpec((nsub, 1, blk), lambda b, i: (cur(b, i), 0, 0)),
                  pl.BlockSpec((1, 1, blk), lambda b, i: (prv(b, i), 0, 0)),
                  pl.BlockSpec((half, blk), lambda b, i: (0, 0)),
                  pl.BlockSpec(memory_space=pltpu.SMEM)],
        out_specs=pl.BlockSpec((nsub * blk, aw), lambda b, i: (cur(b, i), 0)),
        out_shape=jax.ShapeDtypeStruct((n, aw), BF16),
        compiler_params=_params(("arbitrary", "arbitrary")),
        name="attn",
    )(q_src, kv_src, kv_src, kv_src, kv_src, pos3, pos3, inv_freq, sinks)


def _log_sigmoid(x):
    return jnp.minimum(x, 0.0) - jnp.log1p(jnp.exp(-jnp.abs(x)))


def _mlstm_kernel(xqk_ref, xprev_ref, v_ref, mo_ref, if_ref, ifn_ref, cw_ref, cb_ref, ng_ref, o_ref,
                  xbuf_ref, c_ref, m_ref, cols_ref, rt_ref, dec_ref, *, tc):
    heads, dk, dv, ch = M_HEADS, M_QK_DIM, M_V_DIM, M_CHUNK
    qkw = heads * dk
    nc = tc // ch
    assert ch * 2 == LANES and nc % 2 == 0 and 2 * heads <= SUBLANES
    t_idx = pl.program_id(1)

    lane = lax.broadcasted_iota(jnp.int32, (1, LANES), 1)
    nblk = tc // LANES
    slot = t_idx % 2

    def gate_prep(g_ref, dst):
        gates = g_ref[...]
        logf = _log_sigmoid(gates)
        row = lax.broadcasted_iota(jnp.int32, (tc, tc), 0)
        col = lax.broadcasted_iota(jnp.int32, (tc, tc), 1)
        tri = jnp.where((row // ch == col // ch) & (col <= row), 1.0, 0.0).astype(BF16)
        hi, mid, lo = _split3(logf)
        bcum = _dot(tri, hi) + _dot(tri, mid) + _dot(tri, lo)
        packed_t = jnp.where(lane < heads, gates, bcum).T

        gt8 = packed_t[0:SUBLANES, :]
        bc_t = pltpu.roll(gt8, heads, 0)
        r_t = gt8 - bc_t
        pos = lax.broadcasted_iota(jnp.int32, (1, tc), 1) % ch

        def lane_blocks(a, f):
            return jnp.concatenate([f(a[:, j * LANES:(j + 1) * LANES]) for j in range(nblk)], axis=1)

        rmax_t = r_t
        for sh in (1, 2, 4, 8, 16, 32):
            back = lane_blocks(rmax_t, lambda blk, sh=sh: pltpu.roll(blk, sh, 1))
            rmax_t = jnp.maximum(rmax_t, jnp.where(pos >= sh, back, NEG_INF))

        last = [c * ch + ch - 1 for c in range(nc)]
        b_last = [bc_t[:, i:i + 1] for i in last]
        g_max = [bc_t[:, i:i + 1] + rmax_t[:, i:i + 1] for i in last]
        m_state = m_ref[:, 0:1]
        m_in, m_out = [], []
        for c in range(nc):
            m_in.append(m_state)
            m_state = jnp.maximum(b_last[c] + m_state, g_max[c])
            m_out.append(m_state)
        m_ref[...] = jnp.broadcast_to(m_state, m_ref.shape)
        decay = [jnp.exp(b_last[c] + m_in[c] - m_out[c]) for c in range(nc)]

        def per_time(cols):
            return jnp.concatenate(
                [jnp.where(lane < ch, cols[2 * j], cols[2 * j + 1]) for j in range(nblk)], axis=1)

        m_in_t, m_out_t, b_last_t = per_time(m_in), per_time(m_out), per_time(b_last)
        w_t = jnp.exp(b_last_t + r_t - m_out_t)
        a_log_t = bc_t + m_in_t
        m_t_t = jnp.maximum(a_log_t, bc_t + rmax_t)
        inter_t = jnp.exp(a_log_t - m_t_t)
        eneg_t = jnp.exp(-m_t_t)
        sub = lax.broadcasted_iota(jnp.int32, (SUBLANES, 1), 0)
        cols_t = jnp.concatenate(
            [jnp.where(sub < heads, w_t, pltpu.roll(inter_t, heads, 0)),
             jnp.where(sub < heads, eneg_t, pltpu.roll(m_t_t, heads, 0)),
             bc_t,
             jnp.zeros((LANES - 3 * SUBLANES, tc), F32)], axis=0)
        cols_ref[dst] = cols_t.T
        rt_ref[dst] = r_t
        dec_ref[dst] = per_time(decay)

    @pl.when(t_idx == 0)
    def _():
        c_ref[...] = jnp.zeros_like(c_ref)
        m_ref[...] = jnp.full_like(m_ref, NEG_INF)
        gate_prep(if_ref, 0)

    cols = cols_ref[slot]
    r_t = rt_ref[slot]
    dec_t = dec_ref[slot]
    gate_prep(ifn_ref, 1 - slot)

    pad = SUBLANES
    xbuf_ref[0:pad, :] = jnp.where(t_idx > 0, xprev_ref[...], 0.0)
    xbuf_ref[pad:pad + tc, :] = xqk_ref[...]
    y = cb_ref[...]
    for j in range(CONV_WIDTH):
        off = pad - (CONV_WIDTH - 1) + j
        y = y + cw_ref[j:j + 1, :] * xbuf_ref[off:off + tc, :]
    y = y * jax.nn.sigmoid(y)
    q_all = y[:, :qkw].astype(BF16)
    k_all = y[:, qkw:] * (dk ** -0.5)
    k_all_b = k_all.astype(BF16)

    causal = (lax.broadcasted_iota(jnp.int32, (ch, ch), 1) <= lax.broadcasted_iota(jnp.int32, (ch, ch), 0))
    ng = ng_ref[...]
    ones_v = jnp.ones((tc, LANES), BF16)
    ones_sq = jnp.ones((dv, LANES), BF16)

    def chunks(a):
        return a.reshape(nc, ch, a.shape[-1])

    for h in range(heads):
        w_col = cols[:, h:h + 1]
        inter = cols[:, heads + h:heads + h + 1]
        eneg = cols[:, 2 * heads + h:2 * heads + h + 1]
        m_t = cols[:, 3 * heads + h:3 * heads + h + 1]
        bc = cols[:, 4 * heads + h:4 * heads + h + 1]
        r_row = jnp.stack([r_t[h:h + 1, c * ch:(c + 1) * ch] for c in range(nc)])
        q = chunks(q_all[:, h * dk:(h + 1) * dk])
        kb = chunks(k_all_b[:, h * dk:(h + 1) * dk])
        v = chunks(jnp.concatenate([v_ref[:, h * dv:(h + 1) * dv].astype(BF16), ones_v], axis=1))
        wkb = chunks((w_col * k_all[:, h * dk:(h + 1) * dk]).astype(BF16))

        c_state = c_ref[h]
        c_in = []
        for c in range(nc):
            c_in.append(c_state.astype(BF16))
            c_state = dec_t[h:h + 1, c * ch:c * ch + 1] * c_state + _dot_tn(wkb[c], v[c])
        c_ref[h] = c_state

        d = jnp.where(causal, chunks(bc) + r_row, NEG_INF)
        qk = jnp.stack([_dot_nt(q[c], kb[c]) for c in range(nc)]) * jnp.exp(d - chunks(m_t))
        qkb = qk.astype(BF16)
        num = chunks(inter) * jnp.stack([_dot(q[c], c_in[c]) for c in range(nc)]) \
            + jnp.stack([_dot(qkb[c], v[c]) for c in range(nc)])
        num = num.reshape(tc, dv + LANES)
        inv = 1.0 / jnp.maximum(jnp.abs(num[:, dv:]), eneg)
        halves = [num[:, i * LANES:(i + 1) * LANES] * inv for i in range(dv // LANES)]
        sq = jnp.concatenate([x * x for x in halves], axis=1).astype(BF16)
        rs = lax.rsqrt(_dot(sq, ones_sq) * (1.0 / dv) + NORM_EPS)
        for i, x in enumerate(halves):
            sl = slice(h * dv + i * LANES, h * dv + (i + 1) * LANES)
            o_ref[:, sl] = (x * rs * ng[:, sl] * jax.nn.sigmoid(mo_ref[:, sl])).astype(BF16)


def _mlstm(proj, small, conv_w, conv_b, norm_g, bsz, seq, qk_col, v_col, o_col, if_col, tc=MLSTM_T):
    heads, dk, dv = M_HEADS, M_QK_DIM, M_V_DIM
    qk2 = 2 * heads * dk
    mw = heads * dv
    nt = seq // tc
    n = bsz * seq

    def rows(b, t):
        return b * nt + t

    def prev_rows(b, t):
        return jnp.maximum((b * nt + t) * (tc // SUBLANES) - 1, 0)

    return pl.pallas_call(
        functools.partial(_mlstm_kernel, tc=tc),
        grid=(bsz, nt),
        in_specs=[pl.BlockSpec((tc, qk2), lambda b, t: (rows(b, t), qk_col // qk2)),
                  pl.BlockSpec((SUBLANES, qk2), lambda b, t: (prev_rows(b, t), qk_col // qk2)),
                  pl.BlockSpec((tc, mw), lambda b, t: (rows(b, t), v_col // mw)),
                  pl.BlockSpec((tc, mw), lambda b, t: (rows(b, t), o_col // mw)),
                  pl.BlockSpec((tc, LANES), lambda b, t: (rows(b, t), if_col // LANES)),
                  pl.BlockSpec((tc, LANES), lambda b, t: (rows(b, jnp.minimum(t + 1, nt - 1)), if_col // LANES)),
                  pl.BlockSpec((CONV_WIDTH, qk2), lambda b, t: (0, 0)),
                  pl.BlockSpec((1, qk2), lambda b, t: (0, 0)),
                  pl.BlockSpec((1, mw), lambda b, t: (0, 0))],
        out_specs=pl.BlockSpec((tc, mw), lambda b, t: (rows(b, t), 0)),
        out_shape=jax.ShapeDtypeStruct((n, mw), BF16),
        scratch_shapes=[pltpu.VMEM((SUBLANES + tc, qk2), F32),
                        pltpu.VMEM((heads, dk, dv + LANES), F32),
                        pltpu.VMEM((SUBLANES, LANES), F32),
                        pltpu.VMEM((2, tc, LANES), F32),
                        pltpu.VMEM((2, SUBLANES, tc), F32),
                        pltpu.VMEM((2, SUBLANES, tc), F32)],
        compiler_params=_params(("arbitrary", "arbitrary")),
        name="mlstm",
    )(proj, proj, proj, proj, small, small, conv_w, conv_b.reshape(1, qk2), norm_g.reshape(1, mw))


def _merge_kernel(oa_ref, hm_ref, u_ref, h_ref, gt_ref, wa_ref, wm_ref, wga_ref, wgm_ref, wo_ref, o_ref, *, rows):
    j = pl.program_id(1)
    nj = pl.num_programs(1)
    tm = h_ref.shape[0]

    def step(first, last):
        for r in range(tm // rows):
            sl = slice(r * rows, (r + 1) * rows)
            u = u_ref[sl, :]
            y_att = _dot(oa_ref[sl, :], wa_ref[...])
            y_m = _dot(hm_ref[sl, :], wm_ref[...])
            g_att = jax.nn.sigmoid(_dot(u, wga_ref[...]))
            g_m = jax.nn.sigmoid(_dot(u, wgm_ref[...]))
            acc = _dot((g_att * y_att + g_m * y_m).astype(BF16), wo_ref[...])
            if not first:
                acc = o_ref[sl, :] + acc
            if last:
                acc = h_ref[sl, :] + gt_ref[0] * acc
            o_ref[sl, :] = acc

    pl.when(j == 0)(lambda: step(True, False))
    pl.when((j > 0) & (j < nj - 1))(lambda: step(False, False))
    pl.when(j == nj - 1)(lambda: step(False, True))


def _merge(o_att, hm, u, h, seq, mod3, mod_idx, w_att_up, w_mlstm_up, w_gates, gates_col, w_out, tm=MERGE_TM,
           tn=MERGE_TN, rows=MERGE_TM):
    n, d = h.shape
    aw = o_att.shape[1]
    mw = hm.shape[1]
    nj = d // tn
    assert nj >= 2 and gates_col % tn == 0
    g0 = gates_col // tn
    tiles_per_seq = seq // tm
    return pl.pallas_call(
        functools.partial(_merge_kernel, rows=rows),
        grid=(n // tm, nj),
        in_specs=[pl.BlockSpec((tm, aw), lambda i, j: (i, 0)),
                  pl.BlockSpec((tm, mw), lambda i, j: (i, 0)),
                  pl.BlockSpec((tm, d), lambda i, j: (i, 0)),
                  pl.BlockSpec((tm, d), lambda i, j: (i, 0)),
                  pl.BlockSpec((1, 1, d), lambda i, j: ((i // tiles_per_seq) * N_MOD + mod_idx, 0, 0)),
                  pl.BlockSpec((aw, tn), lambda i, j: (0, j)),
                  pl.BlockSpec((mw, tn), lambda i, j: (0, j)),
                  pl.BlockSpec((d, tn), lambda i, j: (0, g0 + j)),
                  pl.BlockSpec((d, tn), lambda i, j: (0, g0 + nj + j)),
                  pl.BlockSpec((tn, d), lambda i, j: (j, 0))],
        out_specs=pl.BlockSpec((tm, d), lambda i, j: (i, 0)),
        out_shape=jax.ShapeDtypeStruct((n, d), F32),
        compiler_params=_params(("arbitrary", "arbitrary")),
        name="merge",
    )(o_att, hm, u, h, mod3, w_att_up, w_mlstm_up, w_gates, w_gates, w_out)


def kernel(x, c, positions, w_ada, b_ada, norm1_g, ffn1_w_in, ffn1_w_out, norm2_g, w_in, conv_w, conv_b,
           b_igate, b_fgate, sinks, mlstm_norm_g, w_att_up, w_mlstm_up, w_out, norm3_g, ffn2_w_in,
           ffn2_w_out, final_g):
    bsz, seq, d = x.shape
    depth = w_ada.shape[0]
    n = bsz * seq
    aw = ATT_HEADS * ATT_HEAD_DIM
    kvw = ATT_KV_HEADS * ATT_HEAD_DIM
    qk2 = 2 * M_HEADS * M_QK_DIM
    mw = M_HEADS * M_V_DIM

    sizes = (aw, kvw, kvw, qk2, mw, M_HEADS, M_HEADS, mw, 2 * d)
    offs = [0]
    for s in sizes:
        offs.append(offs[-1] + s)
    (o_q, o_k, o_v, o_qk, o_mv, o_i, o_f, o_mo, o_g) = offs[:-1]
    q_col, qk_col, mv_col, mo_col = 0, aw, aw + qk2, aw + qk2 + mw
    k_col, v_col, if_col = 0, kvw, 2 * kvw

    half = ROPE_DIM // 2
    inv_freq = jnp.power(ROPE_THETA, -jnp.arange(half, dtype=F32) * 2.0 / ROPE_DIM)
    inv_freq = jnp.broadcast_to(inv_freq[:, None], (half, WINDOW))
    pos3 = positions.reshape(n // WINDOW, 1, WINDOW)

    h = x.reshape(n, d)
    for l in range(depth):
        mod3 = _ada(c, w_ada[l], b_ada[l]).reshape(bsz * N_MOD, 1, d)

        n_main = aw + qk2 + 2 * mw
        pieces = ((o_q, aw, aw), (o_qk, qk2, qk2), (o_mv, mw, mw), (o_mo, mw, mw),
                  (o_g, 2 * d, 2 * d),
                  (o_k, 2 * kvw, 2 * kvw), (o_i, 2 * kvw, 2 * M_HEADS))
        h, (ffn2_in_b, ffn2_out_b, w_all) = _ffn(
            h, seq, norm1_g[l], mod3, 0, ffn1_w_in[l].astype(BF16), ffn1_w_out[l].astype(BF16),
            casts=(ffn2_w_in[l], ffn2_w_out[l]), repack=(jnp.swapaxes(w_in[l], 0, 1), pieces))
        g_col, s_col = n_main, n_main + 2 * d
        b_small = jnp.concatenate(
            [jnp.zeros((2 * kvw,), F32), b_igate[l], b_fgate[l], jnp.zeros((2 * kvw - 2 * M_HEADS,), F32)])[None, :]
        proj, small, u, (att_up_b, mlstm_up_b, out_b) = _proj(
            h, seq, norm2_g[l], mod3, 3, w_all, n_main, s_col, b_small, casts=(w_att_up[l], w_mlstm_up[l], w_out[l]))

        o_att = _attn(proj, small, pos3, inv_freq, sinks[l], bsz, seq, q_col, k_col, v_col)
        hm = _mlstm(proj, small, conv_w[l], conv_b[l], mlstm_norm_g[l], bsz, seq, qk_col, mv_col, mo_col, if_col)
        h = _merge(o_att, hm, u, h, seq, mod3, 5, att_up_b, mlstm_up_b, w_all, g_col, out_b)

        last = l == depth - 1
        h, _ = _ffn(h, seq, norm3_g[l], mod3, 6, ffn2_in_b, ffn2_out_b, final_g=final_g if last else None)
    return h.reshape(bsz, seq, d)
```

```python
import functools

import jax
import jax.numpy as jnp
from jax import lax
from jax.experimental import pallas as pl
from jax.experimental.pallas import tpu as pltpu

F32 = jnp.float32
BF16 = jnp.bfloat16

ATT_HEADS = 16
ATT_KV_HEADS = 2
ATT_HEAD_DIM = 64
WINDOW = 128
ROPE_DIM = ATT_HEAD_DIM // 4
ROPE_THETA = 500000.0
M_HEADS = 4
M_QK_DIM = 128
M_V_DIM = 256
M_CHUNK = 64
CONV_WIDTH = 4
NORM_EPS = 1e-6
NEG_INF = -1e30
N_MOD = 9

LANES = 128
SUBLANES = 8
VMEM_LIMIT = 60 * 1024 * 1024

ADA_K = 128
FFN_TM, FFN_TF = 1024, 512
PROJ_TM, PROJ_TN = 1024, 1024
MERGE_TM, MERGE_TN = 512, 512
MLSTM_T = 512
ATTN_BLOCKS = 1
REPACK_TB = 128
FFN_CHUNK, PROJ_CHUNK = 512, 256


def _dot(a, b):
    return jnp.dot(a, b, preferred_element_type=F32)


def _dot_nt(a, b):
    return lax.dot_general(a, b, (((1,), (1,)), ((), ())), preferred_element_type=F32)


def _dot_tn(a, b):
    return lax.dot_general(a, b, (((0,), (0,)), ((), ())), preferred_element_type=F32)


def _split3(x):
    hi = x.astype(BF16)
    r1 = x - hi.astype(F32)
    mid = r1.astype(BF16)
    lo = (r1 - mid.astype(F32)).astype(BF16)
    return hi, mid, lo


def _rms(x):
    return x * lax.rsqrt(jnp.mean(x * x, axis=-1, keepdims=True) + NORM_EPS)


def _params(sem):
    return pltpu.CompilerParams(dimension_semantics=sem, vmem_limit_bytes=VMEM_LIMIT)


def _ada_kernel(c_ref, b_ref, w_ref, o_ref):
    @pl.when(pl.program_id(0) == 0)
    def _():
        o_ref[...] = jnp.broadcast_to(b_ref[...], o_ref.shape)

    c = c_ref[...]
    hi, mid, lo = _split3(c * jax.nn.sigmoid(c))
    w = w_ref[...].astype(BF16)
    o_ref[...] += _dot(hi, w) + _dot(mid, w) + _dot(lo, w)


def _ada(c, w_ada, b_ada, tk=ADA_K):
    bsz, d = c.shape
    n = w_ada.shape[1]
    cp = jnp.zeros((SUBLANES, d), F32).at[:bsz].set(c)
    out = pl.pallas_call(
        _ada_kernel,
        grid=(d // tk,),
        in_specs=[pl.BlockSpec((SUBLANES, tk), lambda k: (0, k)),
                  pl.BlockSpec((1, n), lambda k: (0, 0)),
                  pl.BlockSpec((tk, n), lambda k: (k, 0))],
        out_specs=pl.BlockSpec((SUBLANES, n), lambda k: (0, 0)),
        out_shape=jax.ShapeDtypeStruct((SUBLANES, n), F32),
        compiler_params=_params(("arbitrary",)),
        name="ada",
    )(cp, b_ada.reshape(1, n), w_ada)
    return out[:bsz]


def _lookup(table, i):
    runs = []
    for k, v in enumerate(table):
        if runs and len(runs[-1]) == 2 and k == runs[-1][0] + 1:
            runs[-1] = (runs[-1][0], runs[-1][1], v - runs[-1][1])
        elif runs and len(runs[-1]) == 3 and v == runs[-1][1] + (k - runs[-1][0]) * runs[-1][2]:
            pass
        else:
            runs.append((k, v))
    runs = [r if len(r) == 3 else (r[0], r[1], 0) for r in runs]
    out = jnp.int32(0)
    for first, base, step in runs:
        out = jnp.where(i >= first, base + (i - first) * step, out)
    return out


def _repack_tables(pieces, n_features, tb):
    starts, valids = [], []
    for src, width, valid in pieces:
        assert width % tb == 0 and src % SUBLANES == 0 and src + width <= n_features
        for b in range(width // tb):
            starts.append(src + b * tb)
            valids.append(min(max(valid - b * tb, 0), tb))
    return starts, valids


def _ffn_kernel(x_ref, g_ref, sh_ref, sc_ref, gt_ref, wa_ref, wb_ref, wo_ref, *rest, final_norm, rows, n_cast,
                repack_valids):
    rest = list(rest)
    fg_ref = rest.pop(0) if final_norm else None
    n_rp = 1 if repack_valids else 0
    n_in = n_cast + n_rp
    side_in, o_ref, side_out, u_ref = rest[:n_in], rest[n_in], rest[n_in + 1:2 * n_in + 1], rest[-1]
    j = pl.program_id(1)
    nj = pl.num_programs(1)
    tm = x_ref.shape[0]

    def side_jobs():
        for src, dst in zip(side_in[:n_cast], side_out[:n_cast]):
            dst[...] = src[...].astype(BF16)
        if repack_valids:
            blk = jnp.minimum(pl.program_id(0) * nj + j, len(repack_valids) - 1)
            x_t = side_in[n_cast][...].T
            lane = lax.broadcasted_iota(jnp.int32, (1, x_t.shape[1]), 1)
            side_out[n_cast][...] = jnp.where(lane < _lookup(repack_valids, blk), x_t, 0.0).astype(BF16)

    def step(first, last):
        side_jobs()
        chunk = rows if (first or last) else tm
        for r in range(tm // chunk):
            sl = slice(r * chunk, (r + 1) * chunk)
            if first:
                y = _rms(x_ref[sl, :]) * g_ref[...]
                u = (y * (1.0 + sc_ref[0]) + sh_ref[0]).astype(BF16)
                u_ref[sl, :] = u
            else:
                u = u_ref[sl, :]
            a = _dot(u, wa_ref[...])
            b = _dot(u, wb_ref[...])
            acc = _dot((a * jax.nn.sigmoid(a) * b).astype(BF16), wo_ref[...])
            if not first:
                acc = o_ref[sl, :] + acc
            if last:
                acc = x_ref[sl, :] + (0.5 * gt_ref[0]) * acc
                if final_norm:
                    acc = _rms(acc) * fg_ref[...]
            o_ref[sl, :] = acc

    pl.when(j == 0)(lambda: step(True, False))
    pl.when((j > 0) & (j < nj - 1))(lambda: step(False, False))
    pl.when(j == nj - 1)(lambda: step(False, True))


def _cast_spec(shape, n_i, n_j):
    r, c = shape
    sub, lanes = 2 * SUBLANES, LANES
    if r % n_i == 0 and c % n_j == 0 and (r // n_i) % sub == 0 and (c // n_j) % lanes == 0:
        return pl.BlockSpec((r // n_i, c // n_j), lambda i, j: (i, j))
    if r % n_j == 0 and c % n_i == 0 and (r // n_j) % sub == 0 and (c // n_i) % lanes == 0:
        return pl.BlockSpec((r // n_j, c // n_i), lambda i, j: (j, i))
    assert r % n_i == 0 and (r // n_i) % sub == 0, shape
    return pl.BlockSpec((r // n_i, c), lambda i, j: (i, 0))


def _ffn(h, seq, norm_g, mod3, mod_base, w_in, w_out, final_g=None, casts=(), repack=None, tm=FFN_TM, tf=FFN_TF,
         rows=FFN_CHUNK, tb=REPACK_TB):
    n, d = h.shape
    dff = w_out.shape[0]
    nj = dff // tf
    assert nj >= 2
    tiles_per_seq = seq // tm
    n_i = n // tm

    def mod_spec(k):
        return pl.BlockSpec((1, 1, d), lambda i, j: ((i // tiles_per_seq) * N_MOD + mod_base + k, 0, 0))

    in_specs = [pl.BlockSpec((tm, d), lambda i, j: (i, 0)),
                pl.BlockSpec((1, d), lambda i, j: (0, 0)),
                mod_spec(0), mod_spec(1), mod_spec(2),
                pl.BlockSpec((d, tf), lambda i, j: (0, j)),
                pl.BlockSpec((d, tf), lambda i, j: (0, j + nj)),
                pl.BlockSpec((tf, d), lambda i, j: (j, 0))]
    args = [h, norm_g.reshape(1, d), mod3, mod3, mod3, w_in, w_in, w_out]
    if final_g is not None:
        in_specs.append(pl.BlockSpec((1, d), lambda i, j: (0, 0)))
        args.append(final_g.reshape(1, d))
    side_in = [_cast_spec(w.shape, n_i, nj) for w in casts]
    side_out = [_cast_spec(w.shape, n_i, nj) for w in casts]
    side_shapes = [jax.ShapeDtypeStruct(w.shape, BF16) for w in casts]
    side_args = list(casts)
    valids = None
    if repack is not None:
        w_t, pieces = repack
        starts, valids = _repack_tables(pieces, w_t.shape[0], tb)
        nb = len(starts)
        assert nb <= n_i * nj and w_t.shape[1] == d

        def blk(i, j):
            return jnp.minimum(i * nj + j, nb - 1)

        side_in.append(pl.BlockSpec((pl.Element(tb), pl.Element(d)),
                                    lambda i, j: (pl.multiple_of(_lookup(starts, blk(i, j)), SUBLANES), 0)))
        side_out.append(pl.BlockSpec((d, tb), lambda i, j: (0, blk(i, j))))
        side_shapes.append(jax.ShapeDtypeStruct((d, nb * tb), BF16))
        side_args.append(w_t)
    outs = pl.pallas_call(
        functools.partial(_ffn_kernel, final_norm=final_g is not None, rows=rows, n_cast=len(casts),
                          repack_valids=valids),
        grid=(n_i, nj),
        in_specs=in_specs + side_in,
        out_specs=[pl.BlockSpec((tm, d), lambda i, j: (i, 0))] + side_out,
        out_shape=[jax.ShapeDtypeStruct((n, d), F32)] + side_shapes,
        scratch_shapes=[pltpu.VMEM((tm, d), BF16)],
        compiler_params=_params(("arbitrary", "arbitrary")),
        name="ffn_final" if final_g is not None else "ffn",
    )(*args, *side_args)
    return outs[0], tuple(outs[1:])


def _proj_kernel(x_ref, g_ref, sh_ref, sc_ref, w_ref, ws_ref, bs_ref, *rest, rows, n_cast):
    cast_in, (p_ref, s_ref, u_ref), cast_out = rest[:n_cast], rest[n_cast:n_cast + 3], rest[n_cast + 3:]
    j = pl.program_id(1)
    tm = x_ref.shape[0]

    def side_jobs():
        for src, dst in zip(cast_in, cast_out):
            dst[...] = src[...].astype(BF16)

    @pl.when(j == 0)
    def _():
        side_jobs()
        for r in range(tm // rows):
            sl = slice(r * rows, (r + 1) * rows)
            y = _rms(x_ref[sl, :]) * g_ref[...]
            u = (y * (1.0 + sc_ref[0]) + sh_ref[0]).astype(BF16)
            u_ref[sl, :] = u
            s_ref[sl, :] = _dot(u, ws_ref[...]) + bs_ref[...]
            p_ref[sl, :] = _dot(u, w_ref[...])

    @pl.when(j > 0)
    def _():
        side_jobs()
        p_ref[...] = _dot(u_ref[...], w_ref[...])


def _proj(h, seq, norm_g, mod3, mod_base, w_all, ncol, small_col, b_small, casts=(), tm=PROJ_TM, tn=PROJ_TN,
          rows=PROJ_CHUNK):
    n, d = h.shape
    nsmall = b_small.shape[1]
    assert small_col % nsmall == 0 and ncol % tn == 0
    tiles_per_seq = seq // tm
    n_i, n_j = n // tm, ncol // tn

    def mod_spec(k):
        return pl.BlockSpec((1, 1, d), lambda i, j: ((i // tiles_per_seq) * N_MOD + mod_base + k, 0, 0))

    outs = pl.pallas_call(
        functools.partial(_proj_kernel, rows=rows, n_cast=len(casts)),
        grid=(n_i, n_j),
        in_specs=[pl.BlockSpec((tm, d), lambda i, j: (i, 0)),
                  pl.BlockSpec((1, d), lambda i, j: (0, 0)),
                  mod_spec(0), mod_spec(1),
                  pl.BlockSpec((d, tn), lambda i, j: (0, j)),
                  pl.BlockSpec((d, nsmall), lambda i, j: (0, small_col // nsmall)),
                  pl.BlockSpec((1, nsmall), lambda i, j: (0, 0))]
        + [_cast_spec(w.shape, n_i, n_j) for w in casts],
        out_specs=[pl.BlockSpec((tm, tn), lambda i, j: (i, j)),
                   pl.BlockSpec((tm, nsmall), lambda i, j: (i, 0)),
                   pl.BlockSpec((tm, d), lambda i, j: (i, 0))]
        + [_cast_spec(w.shape, n_i, n_j) for w in casts],
        out_shape=[jax.ShapeDtypeStruct((n, ncol), F32),
                   jax.ShapeDtypeStruct((n, nsmall), F32),
                   jax.ShapeDtypeStruct((n, d), BF16)]
        + [jax.ShapeDtypeStruct(w.shape, BF16) for w in casts],
        compiler_params=_params(("arbitrary", "arbitrary")),
        name="proj",
    )(h, norm_g.reshape(1, d), mod3, mod3, w_all, w_all, b_small, *casts)
    return outs[0], outs[1], outs[2], tuple(outs[3:])


def _attn_kernel(q_ref, kc_ref, vc_ref, kp_ref, vp_ref, posc_ref, posp_ref, invf_ref, sink_ref, o_ref, *, nsub):
    blk = WINDOW
    lane = lax.broadcasted_iota(jnp.int32, (1, LANES), 1)
    hl = lane % ATT_HEAD_DIM
    half = ROPE_DIM // 2
    lo_half = lane < ATT_HEAD_DIM

    def rope_tables(pos):
        pos = pos.astype(F32)
        ang = invf_ref[...] * pos
        reps = LANES // half
        ct = jnp.tile(jnp.cos(ang), (reps, 1)).T
        st = jnp.tile(jnp.sin(ang), (reps, 1)).T
        c = jnp.where(hl < ROPE_DIM, ct, 1.0)
        s = jnp.where(hl < ROPE_DIM, st, 0.0)
        return c, s

    def swap(x):
        return jnp.where(hl < half, pltpu.roll(x, LANES - half, 1),
                         jnp.where(hl < ROPE_DIM, -pltpu.roll(x, half, 1), 0.0))

    def rope(x, c, s):
        return x * c - swap(x) * s

    tabs = [rope_tables(posp_ref[0])] + [rope_tables(posc_ref[t]) for t in range(nsub)]
    k = jnp.concatenate([rope(kp_ref[...], *tabs[0])]
                        + [rope(kc_ref[t * blk:(t + 1) * blk, :], *tabs[t + 1]) for t in range(nsub)], axis=0)
    v = jnp.concatenate([vp_ref[...], vc_ref[...]], axis=0)
    ks = swap(k)
    k_sw, ks_sw = pltpu.roll(k, ATT_HEAD_DIM, 1), pltpu.roll(ks, ATT_HEAD_DIM, 1)
    v_sw = pltpu.roll(v, ATT_HEAD_DIM, 1)

    qi = lax.broadcasted_iota(jnp.int32, (blk, 2 * blk), 0) + blk
    kj = lax.broadcasted_iota(jnp.int32, (blk, 2 * blk), 1)
    band = (kj <= qi) & (kj > qi - WINDOW)
    slot0 = kj[0:1, :] == 0
    not_slot0 = lax.broadcasted_iota(jnp.int32, (2 * blk, 1), 0) > 0
    k_min = jnp.where(pl.program_id(1) == 0, blk, 0)
    band_first = band & (kj >= k_min)

    scale = ATT_HEAD_DIM ** -0.5
    pairs_per_group = ATT_HEADS // ATT_KV_HEADS // 2
    ones = jnp.ones((2 * blk, LANES), F32)
    one_rhs = jnp.concatenate([jnp.where(lo_half, ones, 0.0), jnp.where(lo_half, 0.0, ones)], axis=0)
    for g in range(ATT_KV_HEADS):
        k_g, k_o = (k, k_sw) if g == 0 else (k_sw, k)
        ks_g, ks_o = (ks, ks_sw) if g == 0 else (ks_sw, ks)
        v_g, v_o = (v, v_sw) if g == 0 else (v_sw, v)
        k_lo = jnp.concatenate([jnp.where(lo_half, k_g, 0.0), jnp.where(lo_half, ks_g, 0.0)], axis=1).astype(BF16)
        k_hi = jnp.concatenate([jnp.where(lo_half, 0.0, k_o), jnp.where(lo_half, 0.0, ks_o)], axis=1).astype(BF16)
        v_lo, v_hi = jnp.where(lo_half, v_g, 0.0), jnp.where(lo_half, 0.0, v_o)
        for t in range(nsub):
            rows = slice(t * blk, (t + 1) * blk)
            keys = slice(t * blk, (t + 2) * blk)
            mask = band_first if t == 0 else band
            k_rhs = jnp.concatenate([k_lo[keys], k_hi[keys]], axis=0)
            c_q, s_q = tabs[t + 1][0] * scale, tabs[t + 1][1] * scale
            v_rhs = jnp.concatenate(
                [jnp.concatenate([jnp.where(not_slot0, v_lo[keys], 0.0), jnp.where(not_slot0, v_hi[keys], 0.0)], axis=0),
                 one_rhs], axis=1).astype(BF16)
            q_rows = []
            for pp in range(pairs_per_group):
                p = g * pairs_per_group + pp
                q_p = q_ref[rows, p * LANES:(p + 1) * LANES]
                q_rows.append(jnp.concatenate([q_p * c_q, q_p * s_q], axis=1).astype(BF16))
            s_all = _dot_nt(jnp.concatenate(q_rows, axis=0), k_rhs)
            p_rows = []
            for pp in range(pairs_per_group):
                p = g * pairs_per_group + pp
                halves = []
                for e in range(2):
                    s = s_all[pp * blk:(pp + 1) * blk, e * 2 * blk:(e + 1) * 2 * blk]
                    s = jnp.where(mask, s, jnp.where(slot0, sink_ref[2 * p + e], NEG_INF))
                    m = jnp.max(s, axis=-1, keepdims=True)
                    halves.append(jnp.exp(s - m).astype(BF16))
                p_rows.append(jnp.concatenate(halves, axis=1))
            o_all = _dot(jnp.concatenate(p_rows, axis=0), v_rhs)
            o_all = o_all[:, :LANES] * (1.0 / o_all[:, LANES:])
            for pp in range(pairs_per_group):
                p = g * pairs_per_group + pp
                o_ref[rows, p * LANES:(p + 1) * LANES] = o_all[pp * blk:(pp + 1) * blk].astype(BF16)


def _attn(q_src, kv_src, pos3, inv_freq, sinks, bsz, seq, q_col, k_col, v_col, nsub=ATTN_BLOCKS):
    blk = WINDOW
    aw = ATT_HEADS * ATT_HEAD_DIM
    ns = seq // (nsub * blk)
    n = bsz * seq
    half = ROPE_DIM // 2

    def cur(b, i):
        return b * ns + i

    def prv(b, i):
        return b * ns * nsub + jnp.maximum(i * nsub - 1, 0)

    return pl.pallas_call(
        functools.partial(_attn_kernel, nsub=nsub),
        grid=(bsz, ns),
        in_specs=[pl.BlockSpec((nsub * blk, aw), lambda b, i: (cur(b, i), q_col // aw)),
                  pl.BlockSpec((nsub * blk, LANES), lambda b, i: (cur(b, i), k_col // LANES)),
                  pl.BlockSpec((nsub * blk, LANES), lambda b, i: (cur(b, i), v_col // LANES)),
                  pl.BlockSpec((blk, LANES), lambda b, i: (prv(b, i), k_col // LANES)),
                  pl.BlockSpec((blk, LANES), lambda b, i: (prv(b, i), v_col // LANES)),
                  pl.BlockSpec((nsub, 1, blk), lambda b, i: (cur(b, i), 0, 0)),
                  pl.BlockSpec((1, 1, blk), lambda b, i: (prv(b, i), 0, 0)),
                  pl.BlockSpec((half, blk), lambda b, i: (0, 0)),
                  pl.BlockSpec(memory_space=pltpu.SMEM)],
        out_specs=pl.BlockSpec((nsub * blk, aw), lambda b, i: (cur(b, i), 0)),
        out_shape=jax.ShapeDtypeStruct((n, aw), BF16),
        compiler_params=_params(("arbitrary", "arbitrary")),
        name="attn",
    )(q_src, kv_src, kv_src, kv_src, kv_src, pos3, pos3, inv_freq, sinks)


def _log_sigmoid(x):
    return jnp.minimum(x, 0.0) - jnp.log1p(jnp.exp(-jnp.abs(x)))


def _mlstm_kernel(xqk_ref, xprev_ref, v_ref, mo_ref, if_ref, ifn_ref, cw_ref, cb_ref, ng_ref, o_ref,
                  xbuf_ref, c_ref, m_ref, cols_ref, rt_ref, dec_ref, *, tc):
    heads, dk, dv, ch = M_HEADS, M_QK_DIM, M_V_DIM, M_CHUNK
    qkw = heads * dk
    nc = tc // ch
    assert ch * 2 == LANES and nc % 2 == 0 and 2 * heads <= SUBLANES
    t_idx = pl.program_id(1)

    lane = lax.broadcasted_iota(jnp.int32, (1, LANES), 1)
    nblk = tc // LANES
    slot = t_idx % 2

    def gate_prep(g_ref, dst):
        gates = g_ref[...]
        logf = _log_sigmoid(gates)
        row = lax.broadcasted_iota(jnp.int32, (tc, tc), 0)
        col = lax.broadcasted_iota(jnp.int32, (tc, tc), 1)
        tri = jnp.where((row // ch == col // ch) & (col <= row), 1.0, 0.0).astype(BF16)
        hi, mid, lo = _split3(logf)
        bcum = _dot(tri, hi) + _dot(tri, mid) + _dot(tri, lo)
        packed_t = jnp.where(lane < heads, gates, bcum).T

        gt8 = packed_t[0:SUBLANES, :]
        bc_t = pltpu.roll(gt8, heads, 0)
        r_t = gt8 - bc_t
        pos = lax.broadcasted_iota(jnp.int32, (1, tc), 1) % ch

        def lane_blocks(a, f):
            return jnp.concatenate([f(a[:, j * LANES:(j + 1) * LANES]) for j in range(nblk)], axis=1)

        rmax_t = r_t
        for sh in (1, 2, 4, 8, 16, 32):
            back = lane_blocks(rmax_t, lambda blk, sh=sh: pltpu.roll(blk, sh, 1))
            rmax_t = jnp.maximum(rmax_t, jnp.where(pos >= sh, back, NEG_INF))

        last = [c * ch + ch - 1 for c in range(nc)]
        b_last = [bc_t[:, i:i + 1] for i in last]
        g_max = [bc_t[:, i:i + 1] + rmax_t[:, i:i + 1] for i in last]
        m_state = m_ref[:, 0:1]
        m_in, m_out = [], []
        for c in range(nc):
            m_in.append(m_state)
            m_state = jnp.maximum(b_last[c] + m_state, g_max[c])
            m_out.append(m_state)
        m_ref[...] = jnp.broadcast_to(m_state, m_ref.shape)
        decay = [jnp.exp(b_last[c] + m_in[c] - m_out[c]) for c in range(nc)]

        def per_time(cols):
            return jnp.concatenate(
                [jnp.where(lane < ch, cols[2 * j], cols[2 * j + 1]) for j in range(nblk)], axis=1)

        m_in_t, m_out_t, b_last_t = per_time(m_in), per_time(m_out), per_time(b_last)
        w_t = jnp.exp(b_last_t + r_t - m_out_t)
        a_log_t = bc_t + m_in_t
        m_t_t = jnp.maximum(a_log_t, bc_t + rmax_t)
        inter_t = jnp.exp(a_log_t - m_t_t)
        eneg_t = jnp.exp(-m_t_t)
        sub = lax.broadcasted_iota(jnp.int32, (SUBLANES, 1), 0)
        cols_t = jnp.concatenate(
            [jnp.where(sub < heads, w_t, pltpu.roll(inter_t, heads, 0)),
             jnp.where(sub < heads, eneg_t, pltpu.roll(m_t_t, heads, 0)),
             bc_t,
             jnp.zeros((LANES - 3 * SUBLANES, tc), F32)], axis=0)
        cols_ref[dst] = cols_t.T
        rt_ref[dst] = r_t
        dec_ref[dst] = per_time(decay)

    @pl.when(t_idx == 0)
    def _():
        c_ref[...] = jnp.zeros_like(c_ref)
        m_ref[...] = jnp.full_like(m_ref, NEG_INF)
        gate_prep(if_ref, 0)

    cols = cols_ref[slot]
    r_t = rt_ref[slot]
    dec_t = dec_ref[slot]
    gate_prep(ifn_ref, 1 - slot)

    pad = SUBLANES
    xbuf_ref[0:pad, :] = jnp.where(t_idx > 0, xprev_ref[...], 0.0)
    xbuf_ref[pad:pad + tc, :] = xqk_ref[...]
    y = cb_ref[...]
    for j in range(CONV_WIDTH):
        off = pad - (CONV_WIDTH - 1) + j
        y = y + cw_ref[j:j + 1, :] * xbuf_ref[off:off + tc, :]
    y = y * jax.nn.sigmoid(y)
    q_all = y[:, :qkw].astype(BF16)
    k_all = y[:, qkw:] * (dk ** -0.5)
    k_all_b = k_all.astype(BF16)

    causal = (lax.broadcasted_iota(jnp.int32, (ch, ch), 1) <= lax.broadcasted_iota(jnp.int32, (ch, ch), 0))
    ng = ng_ref[...]
    ones_v = jnp.ones((tc, LANES), BF16)
    ones_sq = jnp.ones((dv, LANES), BF16)

    def chunks(a):
        return a.reshape(nc, ch, a.shape[-1])

    for h in range(heads):
        w_col = cols[:, h:h + 1]
        inter = cols[:, heads + h:heads + h + 1]
        eneg = cols[:, 2 * heads + h:2 * heads + h + 1]
        m_t = cols[:, 3 * heads + h:3 * heads + h + 1]
        bc = cols[:, 4 * heads + h:4 * heads + h + 1]
        r_row = jnp.stack([r_t[h:h + 1, c * ch:(c + 1) * ch] for c in range(nc)])
        q = chunks(q_all[:, h * dk:(h + 1) * dk])
        kb = chunks(k_all_b[:, h * dk:(h + 1) * dk])
        v = chunks(jnp.concatenate([v_ref[:, h * dv:(h + 1) * dv].astype(BF16), ones_v], axis=1))
        wkb = chunks((w_col * k_all[:, h * dk:(h + 1) * dk]).astype(BF16))

        c_state = c_ref[h]
        c_in = []
        for c in range(nc):
            c_in.append(c_state.astype(BF16))
            c_state = dec_t[h:h + 1, c * ch:c * ch + 1] * c_state + _dot_tn(wkb[c], v[c])
        c_ref[h] = c_state

        d = jnp.where(causal, chunks(bc) + r_row, NEG_INF)
        qk = jnp.stack([_dot_nt(q[c], kb[c]) for c in range(nc)]) * jnp.exp(d - chunks(m_t))
        qkb = qk.astype(BF16)
        num = chunks(inter) * jnp.stack([_dot(q[c], c_in[c]) for c in range(nc)]) \
            + jnp.stack([_dot(qkb[c], v[c]) for c in range(nc)])
        num = num.reshape(tc, dv + LANES)
        inv = 1.0 / jnp.maximum(jnp.abs(num[:, dv:]), eneg)
        halves = [num[:, i * LANES:(i + 1) * LANES] * inv for i in range(dv // LANES)]
        sq = jnp.concatenate([x * x for x in halves], axis=1).astype(BF16)
        rs = lax.rsqrt(_dot(sq, ones_sq) * (1.0 / dv) + NORM_EPS)
        for i, x in enumerate(halves):
            sl = slice(h * dv + i * LANES, h * dv + (i + 1) * LANES)
            o_ref[:, sl] = (x * rs * ng[:, sl] * jax.nn.sigmoid(mo_ref[:, sl])).astype(BF16)


def _mlstm(proj, small, conv_w, conv_b, norm_g, bsz, seq, qk_col, v_col, o_col, if_col, tc=MLSTM_T):
    heads, dk, dv = M_HEADS, M_QK_DIM, M_V_DIM
    qk2 = 2 * heads * dk
    mw = heads * dv
    nt = seq // tc
    n = bsz * seq

    def rows(b, t):
        return b * nt + t

    def prev_rows(b, t):
        return jnp.maximum((b * nt + t) * (tc // SUBLANES) - 1, 0)

    return pl.pallas_call(
        functools.partial(_mlstm_kernel, tc=tc),
        grid=(bsz, nt),
        in_specs=[pl.BlockSpec((tc, qk2), lambda b, t: (rows(b, t), qk_col // qk2)),
                  pl.BlockSpec((SUBLANES, qk2), lambda b, t: (prev_rows(b, t), qk_col // qk2)),
                  pl.BlockSpec((tc, mw), lambda b, t: (rows(b, t), v_col // mw)),
                  pl.BlockSpec((tc, mw), lambda b, t: (rows(b, t), o_col // mw)),
                  pl.BlockSpec((tc, LANES), lambda b, t: (rows(b, t), if_col // LANES)),
                  pl.BlockSpec((tc, LANES), lambda b, t: (rows(b, jnp.minimum(t + 1, nt - 1)), if_col // LANES)),
                  pl.BlockSpec((CONV_WIDTH, qk2), lambda b, t: (0, 0)),
                  pl.BlockSpec((1, qk2), lambda b, t: (0, 0)),
                  pl.BlockSpec((1, mw), lambda b, t: (0, 0))],
        out_specs=pl.BlockSpec((tc, mw), lambda b, t: (rows(b, t), 0)),
        out_shape=jax.ShapeDtypeStruct((n, mw), BF16),
        scratch_shapes=[pltpu.VMEM((SUBLANES + tc, qk2), F32),
                        pltpu.VMEM((heads, dk, dv + LANES), F32),
                        pltpu.VMEM((SUBLANES, LANES), F32),
                        pltpu.VMEM((2, tc, LANES), F32),
                        pltpu.VMEM((2, SUBLANES, tc), F32),
                        pltpu.VMEM((2, SUBLANES, tc), F32)],
        compiler_params=_params(("arbitrary", "arbitrary")),
        name="mlstm",
    )(proj, proj, proj, proj, small, small, conv_w, conv_b.reshape(1, qk2), norm_g.reshape(1, mw))


def _merge_kernel(oa_ref, hm_ref, u_ref, h_ref, gt_ref, wa_ref, wm_ref, wga_ref, wgm_ref, wo_ref, o_ref, *, rows):
    j = pl.program_id(1)
    nj = pl.num_programs(1)
    tm = h_ref.shape[0]

    def step(first, last):
        for r in range(tm // rows):
            sl = slice(r * rows, (r + 1) * rows)
            u = u_ref[sl, :]
            y_att = _dot(oa_ref[sl, :], wa_ref[...])
            y_m = _dot(hm_ref[sl, :], wm_ref[...])
            g_att = jax.nn.sigmoid(_dot(u, wga_ref[...]))
            g_m = jax.nn.sigmoid(_dot(u, wgm_ref[...]))
            acc = _dot((g_att * y_att + g_m * y_m).astype(BF16), wo_ref[...])
            if not first:
                acc = o_ref[sl, :] + acc
            if last:
                acc = h_ref[sl, :] + gt_ref[0] * acc
            o_ref[sl, :] = acc

    pl.when(j == 0)(lambda: step(True, False))
    pl.when((j > 0) & (j < nj - 1))(lambda: step(False, False))
    pl.when(j == nj - 1)(lambda: step(False, True))


def _merge(o_att, hm, u, h, seq, mod3, mod_idx, w_att_up, w_mlstm_up, w_gates, gates_col, w_out, tm=MERGE_TM,
           tn=MERGE_TN, rows=MERGE_TM):
    n, d = h.shape
    aw = o_att.shape[1]
    mw = hm.shape[1]
    nj = d // tn
    assert nj >= 2 and gates_col % tn == 0
    g0 = gates_col // tn
    tiles_per_seq = seq // tm
    return pl.pallas_call(
        functools.partial(_merge_kernel, rows=rows),
        grid=(n // tm, nj),
        in_specs=[pl.BlockSpec((tm, aw), lambda i, j: (i, 0)),
                  pl.BlockSpec((tm, mw), lambda i, j: (i, 0)),
                  pl.BlockSpec((tm, d), lambda i, j: (i, 0)),
                  pl.BlockSpec((tm, d), lambda i, j: (i, 0)),
                  pl.BlockSpec((1, 1, d), lambda i, j: ((i // tiles_per_seq) * N_MOD + mod_idx, 0, 0)),
                  pl.BlockSpec((aw, tn), lambda i, j: (0, j)),
                  pl.BlockSpec((mw, tn), lambda i, j: (0, j)),
                  pl.BlockSpec((d, tn), lambda i, j: (0, g0 + j)),
                  pl.BlockSpec((d, tn), lambda i, j: (0, g0 + nj + j)),
                  pl.BlockSpec((tn, d), lambda i, j: (j, 0))],
        out_specs=pl.BlockSpec((tm, d), lambda i, j: (i, 0)),
        out_shape=jax.ShapeDtypeStruct((n, d), F32),
        compiler_params=_params(("arbitrary", "arbitrary")),
        name="merge",
    )(o_att, hm, u, h, mod3, w_att_up, w_mlstm_up, w_gates, w_gates, w_out)


def kernel(x, c, positions, w_ada, b_ada, norm1_g, ffn1_w_in, ffn1_w_out, norm2_g, w_in, conv_w, conv_b,
           b_igate, b_fgate, sinks, mlstm_norm_g, w_att_up, w_mlstm_up, w_out, norm3_g, ffn2_w_in,
           ffn2_w_out, final_g):
    bsz, seq, d = x.shape
    depth = w_ada.shape[0]
    n = bsz * seq
    aw = ATT_HEADS * ATT_HEAD_DIM
    kvw = ATT_KV_HEADS * ATT_HEAD_DIM
    qk2 = 2 * M_HEADS * M_QK_DIM
    mw = M_HEADS * M_V_DIM

    sizes = (aw, kvw, kvw, qk2, mw, M_HEADS, M_HEADS, mw, 2 * d)
    offs = [0]
    for s in sizes:
        offs.append(offs[-1] + s)
    (o_q, o_k, o_v, o_qk, o_mv, o_i, o_f, o_mo, o_g) = offs[:-1]
    q_col, qk_col, mv_col, mo_col = 0, aw, aw + qk2, aw + qk2 + mw
    k_col, v_col, if_col = 0, kvw, 2 * kvw

    half = ROPE_DIM // 2
    inv_freq = jnp.power(ROPE_THETA, -jnp.arange(half, dtype=F32) * 2.0 / ROPE_DIM)
    inv_freq = jnp.broadcast_to(inv_freq[:, None], (half, WINDOW))
    pos3 = positions.reshape(n // WINDOW, 1, WINDOW)

    h = x.reshape(n, d)
    for l in range(depth):
        mod3 = _ada(c, w_ada[l], b_ada[l]).reshape(bsz * N_MOD, 1, d)

        n_main = aw + qk2 + 2 * mw
        pieces = ((o_q, aw, aw), (o_qk, qk2, qk2), (o_mv, mw, mw), (o_mo, mw, mw),
                  (o_g, 2 * d, 2 * d),
                  (o_k, 2 * kvw, 2 * kvw), (o_i, 2 * kvw, 2 * M_HEADS))
        h, (ffn2_in_b, ffn2_out_b, w_all) = _ffn(
            h, seq, norm1_g[l], mod3, 0, ffn1_w_in[l].astype(BF16), ffn1_w_out[l].astype(BF16),
            casts=(ffn2_w_in[l], ffn2_w_out[l]), repack=(jnp.swapaxes(w_in[l], 0, 1), pieces))
        g_col, s_col = n_main, n_main + 2 * d
        b_small = jnp.concatenate(
            [jnp.zeros((2 * kvw,), F32), b_igate[l], b_fgate[l], jnp.zeros((2 * kvw - 2 * M_HEADS,), F32)])[None, :]
        proj, small, u, (att_up_b, mlstm_up_b, out_b) = _proj(
            h, seq, norm2_g[l], mod3, 3, w_all, n_main, s_col, b_small, casts=(w_att_up[l], w_mlstm_up[l], w_out[l]))

        o_att = _attn(proj, small, pos3, inv_freq, sinks[l], bsz, seq, q_col, k_col, v_col)
        hm = _mlstm(proj, small, conv_w[l], conv_b[l], mlstm_norm_g[l], bsz, seq, qk_col, mv_col, mo_col, if_col)
        h = _merge(o_att, hm, u, h, seq, mod3, 5, att_up_b, mlstm_up_b, w_all, g_col, out_b)

        last = l == depth - 1
        h, _ = _ffn(h, seq, norm3_g[l], mod3, 6, ffn2_in_b, ffn2_out_b, final_g=final_g if last else None)
    return h.reshape(bsz, seq, d)
```
